```python
import math
import numpy as np
import jax
import jax.numpy as jnp
from jax import lax

D_MODEL = 1024
BATCH = 2
SEQ = 8192
DEPTH = 1

HEAD_DIM = 64
DIFF_WIDTH = D_MODEL // 2
NSA_WIDTH = D_MODEL - DIFF_WIDTH
DIFF_HEADS = DIFF_WIDTH // (2 * HEAD_DIM)
DIFF_V_DIM = 2 * HEAD_DIM
NSA_HEADS = NSA_WIDTH // HEAD_DIM
NSA_KV_HEADS = 2
NSA_GROUP = NSA_HEADS // NSA_KV_HEADS
CMP_BLOCK = 32
CMP_STRIDE = 16
CMP_HIDDEN = 2 * HEAD_DIM
SLC_BLOCK = 64
SLC_TOP = 16
WINDOW = 512
Q_BLOCK = 128
N_GROUPS = 8
EXPERTS_PER_GROUP = 8
N_EXPERTS = N_GROUPS * EXPERTS_PER_GROUP
TOP_K = 2
D_EXPERT = D_MODEL // 2
MOE_BLOCK = 128
NEG = -1e30
FORCE_BONUS = 1e4
EPS = 1e-6

IN_SPLITS = (DIFF_HEADS * 2 * HEAD_DIM, DIFF_HEADS * 2 * HEAD_DIM, DIFF_HEADS * DIFF_V_DIM,
             NSA_HEADS * HEAD_DIM) + (NSA_KV_HEADS * HEAD_DIM,) * 6 + (3 * NSA_HEADS,)
IN_COLS = sum(IN_SPLITS)

kernel_name = 'hymba_diffattn_nsa_hiermoe_block'


def rmsnorm(x, g):
    xf = x.astype(jnp.float32)
    y = xf * lax.rsqrt(jnp.mean(xf * xf, axis=-1, keepdims=True) + EPS)
    return (y * g.astype(jnp.float32)).astype(x.dtype)


def masked_softmax(s, mask):
    s = jnp.where(mask, s.astype(jnp.float32), NEG)
    m = jnp.max(s, axis=-1, keepdims=True)
    e = jnp.where(mask, jnp.exp(s - m), 0.0)
    return e / jnp.maximum(jnp.sum(e, axis=-1, keepdims=True), 1e-30)


def alibi_slopes(n):
    return np.array([2.0 ** (-8.0 * (i + 1) / n) for i in range(n)], dtype=np.float32)


def diff_attention(q, k, v, lam_q1, lam_k1, lam_q2, lam_k2, g_subln, lambda_init):
    B, S = q.shape[0], q.shape[1]
    nblk = S // Q_BLOCK
    scale = HEAD_DIM ** -0.5
    f32 = jnp.float32
    lam = (jnp.exp(jnp.sum(lam_q1.astype(f32) * lam_k1.astype(f32)))
           - jnp.exp(jnp.sum(lam_q2.astype(f32) * lam_k2.astype(f32))) + lambda_init)
    slopes = jnp.asarray(alibi_slopes(DIFF_HEADS))[:, None, None, None]
    qb = q.reshape(B, nblk, Q_BLOCK, DIFF_HEADS, 2, HEAD_DIM).transpose(1, 0, 3, 4, 2, 5)
    kt = k.transpose(0, 2, 3, 1, 4)
    vt = v.transpose(0, 2, 1, 3)
    kpos = jnp.arange(S)

    def block(args):
        q_blk, i = args
        qpos = i * Q_BLOCK + jnp.arange(Q_BLOCK)
        dist = qpos[:, None] - kpos[None, :]
        s = jnp.einsum('bhcqd,bhckd->bhcqk', q_blk, kt).astype(f32) * scale
        s = s - slopes * dist.astype(f32)
        p = masked_softmax(s, dist >= 0)
        a = p[:, :, 0] - lam * p[:, :, 1]
        return jnp.einsum('bhqk,bhkd->bhqd', a.astype(vt.dtype), vt)

    o = lax.map(block, (qb, jnp.arange(nblk)))
    o = o.transpose(1, 0, 3, 2, 4).reshape(B, S, DIFF_HEADS, DIFF_V_DIM)
    o = rmsnorm(o, g_subln) * (1.0 - lambda_init)
    return o.reshape(B, S, DIFF_WIDTH)


def nsa_attention(q, k_cmp, v_cmp, k_slc, v_slc, k_win, v_win, gates,
                  pos_k, pos_v, w1_k, w1_v, w2_k, w2_v):
    B, S = q.shape[0], q.shape[1]
    nblk = S // Q_BLOCK
    scale = HEAD_DIM ** -0.5
    f32 = jnp.float32
    slopes = jnp.asarray(alibi_slopes(NSA_HEADS).reshape(NSA_KV_HEADS, NSA_GROUP))[None, :, :, None, None]

    n_cmp = (S - CMP_BLOCK) // CMP_STRIDE + 1
    cidx = np.arange(n_cmp)[:, None] * CMP_STRIDE + np.arange(CMP_BLOCK)[None, :]
    cend = jnp.asarray(cidx[:, -1])

    def compress(t, pos, w1, w2):
        blocks = t[:, cidx] + pos[None, None, :, None, :]
        flat = blocks.transpose(0, 1, 3, 2, 4).reshape(B, n_cmp, NSA_KV_HEADS, CMP_BLOCK * HEAD_DIM)
        return jax.nn.gelu(flat @ w1) @ w2

    kc = compress(k_cmp, pos_k, w1_k, w2_k)
    vc = compress(v_cmp, pos_v, w1_v, w2_v)

    n_slc = S // SLC_BLOCK
    n_top = min(SLC_TOP, n_slc)
    cstart = cidx[:, 0]
    sstart = np.arange(n_slc) * SLC_BLOCK
    overlap = jnp.asarray(((cstart[:, None] < sstart[None, :] + SLC_BLOCK)
                           & (cstart[:, None] + CMP_BLOCK > sstart[None, :])).astype(np.float32))
    jblk = jnp.arange(n_slc)
    slc_kt = k_slc.transpose(0, 2, 1, 3)
    slc_vt = v_slc.transpose(0, 2, 1, 3)
    bi = jnp.arange(B)[:, None, None, None]
    hi = jnp.arange(NSA_KV_HEADS)[None, :, None, None]

    pad = ((0, 0), (WINDOW, 0), (0, 0), (0, 0))
    kw_pad = jnp.pad(k_win, pad)
    vw_pad = jnp.pad(v_win, pad)

    qb = q.reshape(B, nblk, Q_BLOCK, NSA_KV_HEADS, NSA_GROUP, HEAD_DIM).transpose(1, 0, 2, 3, 4, 5)
    gb = gates.reshape(B, nblk, Q_BLOCK, NSA_KV_HEADS, NSA_GROUP, 3).transpose(1, 0, 2, 3, 4, 5)

    def block(args):
        q_blk, g_blk, i = args
        q0 = i * Q_BLOCK
        qpos = q0 + jnp.arange(Q_BLOCK)
        dc = qpos[:, None] - cend[None, :]
        s = jnp.einsum('bqhgd,bkhd->bhgqk', q_blk, kc).astype(f32) * scale - slopes * dc.astype(f32)
        p_c = masked_softmax(s, dc >= 0)
        o_c = jnp.einsum('bhgqk,bkhd->bqhgd', p_c.astype(vc.dtype), vc)
        imp = jnp.einsum('bhgqk,kj->bhqj', p_c, overlap)
        cur = qpos[:, None] // SLC_BLOCK
        forced = (jblk[None, :] == 0) | (jblk[None, :] == cur) | (jblk[None, :] == cur - 1)
        valid = jblk[None, :] * SLC_BLOCK <= qpos[:, None]
        score = jnp.where(valid, imp + FORCE_BONUS * forced.astype(f32), NEG)
        _, top = lax.top_k(score, n_top)
        tok = (top[..., None] * SLC_BLOCK + jnp.arange(SLC_BLOCK)).reshape(
            B, NSA_KV_HEADS, Q_BLOCK, n_top * SLC_BLOCK)
        k_sel = slc_kt[bi, hi, tok]
        v_sel = slc_vt[bi, hi, tok]
        ds = qpos[None, None, :, None] - tok
        s = (jnp.einsum('bqhgd,bhqnd->bhgqn', q_blk, k_sel).astype(f32) * scale
             - slopes * ds[:, :, None].astype(f32))
        p_s = masked_softmax(s, (ds >= 0)[:, :, None])
        o_s = jnp.einsum('bhgqn,bhqnd->bqhgd', p_s.astype(v_sel.dtype), v_sel)
        kw = lax.dynamic_slice_in_dim(kw_pad, q0, WINDOW + Q_BLOCK, axis=1)
        vw = lax.dynamic_slice_in_dim(vw_pad, q0, WINDOW + Q_BLOCK, axis=1)
        wpos = q0 - WINDOW + jnp.arange(WINDOW + Q_BLOCK)
        dw = qpos[:, None] - wpos[None, :]
        s = jnp.einsum('bqhgd,bkhd->bhgqk', q_blk, kw).astype(f32) * scale - slopes * dw.astype(f32)
        p_w = masked_softmax(s, (dw >= 0) & (dw < WINDOW) & (wpos[None, :] >= 0))
        o_w = jnp.einsum('bhgqk,bkhd->bqhgd', p_w.astype(vw.dtype), vw)
        return g_blk[..., 0:1] * o_c + g_blk[..., 1:2] * o_s + g_blk[..., 2:3] * o_w

    o = lax.map(block, (qb, gb, jnp.arange(nblk)))
    return o.transpose(1, 0, 2, 3, 4, 5).reshape(B, S, NSA_WIDTH)


def hier_moe(h, w_rg, b_rg, w_re, b_re, w_gate, w_up, w_down):
    B, S, D = h.shape
    f32 = jnp.float32
    t = h.reshape(-1, D)
    T = t.shape[0]
    lg = (t @ w_rg).astype(f32) + b_rg.astype(f32)
    pg = jax.nn.softmax(lg, axis=-1)
    grp = jnp.argmax(lg, axis=-1)
    p_grp = jnp.take_along_axis(pg, grp[:, None], axis=-1)
    le = ((t @ w_re).astype(f32) + b_re.astype(f32)).reshape(T, N_GROUPS, EXPERTS_PER_GROUP)
    le = jnp.take_along_axis(le, grp[:, None, None], axis=1)[:, 0]
    top_v, top_i = lax.top_k(le, TOP_K)
    wts = jax.nn.softmax(top_v, axis=-1) * p_grp
    eidx = grp[:, None] * EXPERTS_PER_GROUP + top_i

    A = T * TOP_K
    flat_e = eidx.reshape(-1)
    order = jnp.argsort(flat_e)
    sorted_e = flat_e[order]
    counts = jnp.zeros((N_EXPERTS,), jnp.int32).at[flat_e].add(1)
    padded = (counts + MOE_BLOCK - 1) // MOE_BLOCK * MOE_BLOCK
    pad_end = jnp.cumsum(padded)
    pad_start = pad_end - padded
    start = jnp.cumsum(counts) - counts
    dest = pad_start[sorted_e] + (jnp.arange(A) - start[sorted_e])
    n_blocks = -(-A // MOE_BLOCK) + N_EXPERTS
    R = n_blocks * MOE_BLOCK
    xbuf = jnp.zeros((R, D), t.dtype).at[dest].set(t[order // TOP_K])
    blk_e = jnp.minimum(jnp.searchsorted(pad_end, jnp.arange(n_blocks) * MOE_BLOCK, side='right'),
                        N_EXPERTS - 1)

    def expert_block(args):
        xb, e = args
        hid = jax.nn.silu(xb @ w_gate[e]) * (xb @ w_up[e])
        return hid @ w_down[e]

    ybuf = lax.map(expert_block, (xbuf.reshape(n_blocks, MOE_BLOCK, D), blk_e)).reshape(R, D)
    y = jnp.zeros((A, D), ybuf.dtype).at[order].set(ybuf[dest])
    out = jnp.einsum('tkd,tk->td', y.reshape(T, TOP_K, D), wts.astype(y.dtype))
    return out.reshape(B, S, D)


def setup_inputs(seed: int = 0) -> dict:
    key = jax.random.key(seed)
    ks = jax.random.split(key, 32)
    f32 = jnp.float32
    L, D = DEPTH, D_MODEL

    def nrm(k, shape, scale):
        return jax.random.normal(k, shape, f32) * scale

    return {
        'x': nrm(ks[0], (BATCH, SEQ, D), 1.0),
        'c': nrm(ks[1], (BATCH, D), 1.0),
        'w_ada': nrm(ks[2], (L, D, 6 * D), 0.5 * D ** -0.5),
        'b_ada': nrm(ks[3], (L, 6 * D), 0.02),
        'g_pre_mix': 1.0 + nrm(ks[4], (L, D), 0.05),
        'g_post_mix': 1.0 + nrm(ks[5], (L, D), 0.05),
        'g_pre_ffn': 1.0 + nrm(ks[6], (L, D), 0.05),
        'g_post_ffn': 1.0 + nrm(ks[7], (L, D), 0.05),
        'w_in': nrm(ks[8], (L, D, IN_COLS), D ** -0.5),
        'lam_q1': nrm(ks[9], (L, HEAD_DIM), 0.1),
        'lam_k1': nrm(ks[10], (L, HEAD_DIM), 0.1),
        'lam_q2': nrm(ks[11], (L, HEAD_DIM), 0.1),
        'lam_k2': nrm(ks[12], (L, HEAD_DIM), 0.1),
        'g_subln': 1.0 + nrm(ks[13], (L, DIFF_V_DIM), 0.05),
        'cmp_pos_k': nrm(ks[14], (L, CMP_BLOCK, HEAD_DIM), 0.1),
        'cmp_pos_v': nrm(ks[15], (L, CMP_BLOCK, HEAD_DIM), 0.1),
        'cmp_w1_k': nrm(ks[16], (L, CMP_BLOCK * HEAD_DIM, CMP_HIDDEN), (CMP_BLOCK * HEAD_DIM) ** -0.5),
        'cmp_w1_v': nrm(ks[17], (L, CMP_BLOCK * HEAD_DIM, CMP_HIDDEN), (CMP_BLOCK * HEAD_DIM) ** -0.5),
        'cmp_w2_k': nrm(ks[18], (L, CMP_HIDDEN, HEAD_DIM), CMP_HIDDEN ** -0.5),
        'cmp_w2_v': nrm(ks[19], (L, CMP_HIDDEN, HEAD_DIM), CMP_HIDDEN ** -0.5),
        'w_out': nrm(ks[20], (L, D, D), D ** -0.5),
        'w_router_grp': nrm(ks[21], (L, D, N_GROUPS), D ** -0.5),
        'b_router_grp': nrm(ks[22], (L, N_GROUPS), 0.01),
        'w_router_exp': nrm(ks[23], (L, D, N_EXPERTS), D ** -0.5),
        'b_router_exp': nrm(ks[24], (L, N_EXPERTS), 0.01),
        'w_exp_gate': nrm(ks[25], (L, N_EXPERTS, D, D_EXPERT), D ** -0.5),
        'w_exp_up': nrm(ks[26], (L, N_EXPERTS, D, D_EXPERT), D ** -0.5),
        'w_exp_down': nrm(ks[27], (L, N_EXPERTS, D_EXPERT, D), D_EXPERT ** -0.5),
    }


def reference(x, c, w_ada, b_ada, g_pre_mix, g_post_mix, g_pre_ffn, g_post_ffn, w_in,
              lam_q1, lam_k1, lam_q2, lam_k2, g_subln, cmp_pos_k, cmp_pos_v,
              cmp_w1_k, cmp_w1_v, cmp_w2_k, cmp_w2_v, w_out,
              w_router_grp, b_router_grp, w_router_exp, b_router_exp,
              w_exp_gate, w_exp_up, w_exp_down):
    B, S = x.shape[0], x.shape[1]
    split_at = np.cumsum(IN_SPLITS)[:-1].tolist()
    for l in range(DEPTH):
        lambda_init = 0.8 - 0.6 * math.exp(-0.3 * l)
        mod = jax.nn.silu(c) @ w_ada[l] + b_ada[l]
        sh_a, sc_a, gt_a, sh_f, sc_f, gt_f = jnp.split(mod, 6, axis=-1)

        h = rmsnorm(x, g_pre_mix[l]) * (1.0 + sc_a[:, None, :]) + sh_a[:, None, :]
        u = jnp.einsum('bsd,de->bse', h, w_in[l])
        dq, dk, dv, nq, ck, cv, sk, sv, wk, wv, ng = jnp.split(u, split_at, axis=-1)
        kvs = (B, S, NSA_KV_HEADS, HEAD_DIM)
        o_a = diff_attention(dq.reshape(B, S, DIFF_HEADS, 2, HEAD_DIM),
                             dk.reshape(B, S, DIFF_HEADS, 2, HEAD_DIM),
                             dv.reshape(B, S, DIFF_HEADS, DIFF_V_DIM),
                             lam_q1[l], lam_k1[l], lam_q2[l], lam_k2[l], g_subln[l], lambda_init)
        gates = jax.nn.sigmoid(ng.astype(jnp.float32)).astype(u.dtype).reshape(B, S, NSA_HEADS, 3)
        o_b = nsa_attention(nq.reshape(B, S, NSA_HEADS, HEAD_DIM),
                            ck.reshape(kvs), cv.reshape(kvs), sk.reshape(kvs), sv.reshape(kvs),
                            wk.reshape(kvs), wv.reshape(kvs), gates,
                            cmp_pos_k[l], cmp_pos_v[l], cmp_w1_k[l], cmp_w1_v[l],
                            cmp_w2_k[l], cmp_w2_v[l])
        mix = jnp.concatenate([o_a, o_b], axis=-1) @ w_out[l]
        x = x + gt_a[:, None, :] * rmsnorm(mix, g_post_mix[l])

        h = rmsnorm(x, g_pre_ffn[l]) * (1.0 + sc_f[:, None, :]) + sh_f[:, None, :]
        f = hier_moe(h, w_router_grp[l], b_router_grp[l], w_router_exp[l], b_router_exp[l],
                     w_exp_gate[l], w_exp_up[l], w_exp_down[l])
        x = x + gt_f[:, None, :] * rmsnorm(f, g_post_ffn[l])
    return x
```

```python
import functools
import math

import numpy as np
import jax
import jax.numpy as jnp
from jax import lax
from jax.experimental import pallas as pl
from jax.experimental.pallas import tpu as pltpu

F32 = jnp.float32
BF16 = jnp.bfloat16
HIGHEST = lax.Precision.HIGHEST

D_MODEL = 1024
HEAD_DIM = 64
LANES = 128
DIFF_HEADS = 4
NSA_HEADS = 8
NSA_KV_HEADS = 2
NSA_GROUP = NSA_HEADS // NSA_KV_HEADS
CMP_BLOCK = 32
CMP_STRIDE = 16
CMP_HIDDEN = 2 * HEAD_DIM
SLC_BLOCK = 64
SLC_TOP = 16
WINDOW = 512
N_GROUPS = 8
EXPERTS_PER_GROUP = 8
N_EXPERTS = N_GROUPS * EXPERTS_PER_GROUP
TOP_K = 2
D_EXPERT = D_MODEL // 2
NEG = -1e30
FORCE_BONUS = 1e4
EPS = 1e-6

ROW_TILE = 512
DIFF_TQ = 256
NSA_TQ = 128
KEY_TILE = 512
MOE_ROWS = 256
VMEM_LIMIT = 48 * 1024 * 1024

_OFF_DQ, _OFF_DK, _OFF_DV, _OFF_NQ = 0, 512, 1024, 1536
_OFF_KV = 2048
_OFF_NG = 2816
_N_GATES = 3 * NSA_HEADS
_W_DQ, _W_DK, _W_DV, _W_NQ, _W_KV, _W_GT = 1024, 512, 512, 1024, 768, 128
_W_ALL = _W_DQ + _W_DK + _W_DV + _W_NQ + _W_KV + _W_GT


def _alibi_slopes(n):
    return np.array([2.0 ** (-8.0 * (i + 1) / n) for i in range(n)], dtype=np.float32)


def _dot_nt(a, b):
    return lax.dot_general(a, b, (((1,), (1,)), ((), ())), preferred_element_type=F32)


def _rms(v):
    return v * lax.rsqrt(jnp.mean(v * v, axis=-1, keepdims=True) + EPS)


def _params(sem, vmem=VMEM_LIMIT):
    return pltpu.CompilerParams(dimension_semantics=sem, vmem_limit_bytes=vmem)


def _adaln_kernel(c_ref, w_ref, b_ref, o_ref):
    c = c_ref[...]
    s = c * jax.nn.sigmoid(c)
    o_ref[...] = jnp.dot(s, w_ref[...], preferred_element_type=F32, precision=HIGHEST) + b_ref[...]


def _adaln(c, w_ada, b_ada):
    B, D = c.shape
    rows = 8
    cp = jnp.zeros((rows, D), F32).at[:B].set(c)
    n_out = w_ada.shape[1]
    out = pl.pallas_call(
        _adaln_kernel,
        grid=(n_out // D,),
        in_specs=[pl.BlockSpec((rows, D), lambda j: (0, 0)),
                  pl.BlockSpec((D, D), lambda j: (0, j)),
                  pl.BlockSpec((1, D), lambda j: (0, j))],
        out_specs=pl.BlockSpec((rows, D), lambda j: (0, j)),
        out_shape=jax.ShapeDtypeStruct((rows, n_out), F32),
        compiler_params=_params(("parallel",)),
        name="adaln",
    )(cp, w_ada, b_ada.reshape(1, n_out))
    return out[:B]


def _inproj_kernel(x_ref, g_ref, sc_ref, sh_ref, w_ref, dq_ref, dk_ref, dv_ref, nq_ref, kv_ref, gt_ref):
    x = x_ref[...]
    h = (_rms(x) * g_ref[...] * (1.0 + sc_ref[0]) + sh_ref[0]).astype(BF16)
    off = 0
    for ref, width in ((dq_ref, _W_DQ), (dk_ref, _W_DK), (dv_ref, _W_DV), (nq_ref, _W_NQ), (kv_ref, _W_KV)):
        for c in range(0, width, 512):
            cw = min(512, width - c)
            ref[:, c:c + cw] = jnp.dot(h, w_ref[:, off + c:off + c + cw],
                                       preferred_element_type=F32).astype(ref.dtype)
        off += width
    gt_ref[...] = jax.nn.sigmoid(jnp.dot(h, w_ref[:, off:off + _W_GT], preferred_element_type=F32))


def _build_w_big(w_in):
    scale = HEAD_DIM ** -0.5
    z = jnp.zeros((D_MODEL, HEAD_DIM), F32)
    cols = []
    for h in range(DIFF_HEADS):
        q1 = w_in[:, _OFF_DQ + h * 128:_OFF_DQ + h * 128 + 64] * scale
        q2 = w_in[:, _OFF_DQ + h * 128 + 64:_OFF_DQ + h * 128 + 128] * scale
        cols += [q1, z, z, q2]
    cols.append(w_in[:, _OFF_DK:_OFF_DK + 512])
    cols.append(w_in[:, _OFF_DV:_OFF_DV + 512])
    for hq in range(NSA_HEADS):
        q = w_in[:, _OFF_NQ + hq * 64:_OFF_NQ + hq * 64 + 64] * scale
        cols += [q, z] if hq // NSA_GROUP == 0 else [z, q]
    cols.append(w_in[:, _OFF_KV:_OFF_KV + _W_KV])
    cols.append(w_in[:, _OFF_NG:_OFF_NG + _N_GATES])
    cols.append(jnp.zeros((D_MODEL, _W_GT - _N_GATES), F32))
    return jnp.concatenate(cols, axis=1).astype(BF16)


def _in_proj(x2, g_pre, sc, sh, w_big, B, S):
    T, D = x2.shape
    tm = ROW_TILE
    per_b = S // tm
    row = lambda i: (i, 0)
    bat = lambda i: (i // per_b, 0, 0)
    widths = (_W_DQ, _W_DK, _W_DV, _W_NQ, _W_KV)
    outs = pl.pallas_call(
        _inproj_kernel,
        grid=(T // tm,),
        in_specs=[pl.BlockSpec((tm, D), row),
                  pl.BlockSpec((1, D), lambda i: (0, 0)),
                  pl.BlockSpec((1, 1, D), bat),
                  pl.BlockSpec((1, 1, D), bat),
                  pl.BlockSpec((D, _W_ALL), lambda i: (0, 0))],
        out_specs=[pl.BlockSpec((tm, w), row) for w in widths] + [pl.BlockSpec((tm, _W_GT), row)],
        out_shape=[jax.ShapeDtypeStruct((T, w), BF16) for w in widths]
        + [jax.ShapeDtypeStruct((T, _W_GT), F32)],
        compiler_params=_params(("parallel",)),
        name="in_proj",
    )(x2, g_pre.reshape(1, D), sc.reshape(B, 1, D), sh.reshape(B, 1, D), w_big)
    return outs


def _compress_kernel(r_ref, pos_ref, w1_ref, w2_ref, o_ref):
    half = CMP_STRIDE * HEAD_DIM
    pos = pos_ref[0]
    w1 = w1_ref[0]
    w2 = w2_ref[0]
    outs = []
    for h in range(NSA_KV_HEADS):
        r = r_ref[0, 0, h].astype(F32)
        a = jnp.dot((r + pos[:, :half]).astype(BF16), w1[:half], preferred_element_type=F32)
        b = jnp.dot((r + pos[:, half:]).astype(BF16), w1[half:], preferred_element_type=F32)
        n = a.shape[0]
        hid = a + pltpu.roll(b, n - 1, 0)
        g = jax.nn.gelu(hid)
        outs.append(jnp.dot(g.astype(BF16), w2, preferred_element_type=F32))
    o_ref[0, 0] = jnp.concatenate(outs, axis=-1).astype(o_ref.dtype)


def _compress(ck, cv, pos_k, pos_v, w1_k, w1_v, w2_k, w2_v, B, S):
    n_rows = S // CMP_STRIDE
    wide = CMP_STRIDE * HEAD_DIM

    def rows(t):
        return t.reshape(B, n_rows, CMP_STRIDE, NSA_KV_HEADS, HEAD_DIM).transpose(0, 3, 1, 2, 4).reshape(
            B, NSA_KV_HEADS, n_rows, wide)

    r = jnp.stack([rows(ck), rows(cv)])
    pos = jnp.stack([pos_k.reshape(1, -1), pos_v.reshape(1, -1)]).astype(F32)
    w1 = jnp.stack([w1_k, w1_v]).astype(BF16)
    w2 = jnp.stack([w2_k, w2_v]).astype(BF16)
    out = pl.pallas_call(
        _compress_kernel,
        grid=(2, B),
        in_specs=[pl.BlockSpec((1, 1, NSA_KV_HEADS, n_rows, wide), lambda t, b: (t, b, 0, 0, 0)),
                  pl.BlockSpec((1, 1, 2 * wide), lambda t, b: (t, 0, 0)),
                  pl.BlockSpec((1, 2 * wide, CMP_HIDDEN), lambda t, b: (t, 0, 0)),
                  pl.BlockSpec((1, CMP_HIDDEN, HEAD_DIM), lambda t, b: (t, 0, 0))],
        out_specs=pl.BlockSpec((1, 1, n_rows, NSA_KV_HEADS * HEAD_DIM), lambda t, b: (t, b, 0, 0)),
        out_shape=jax.ShapeDtypeStruct((2, B, n_rows, NSA_KV_HEADS * HEAD_DIM), BF16),
        compiler_params=_params(("parallel", "parallel")),
        name="nsa_compress",
    )(r, pos, w1, w2)
    return out[0], out[1]


def _online(s, m_ref, l_ref, rows):
    m_old = m_ref[rows, :]
    m_new = jnp.maximum(m_old, jnp.max(s, axis=-1, keepdims=True))
    p = jnp.exp(s - m_new)
    alpha = jnp.exp(m_old - m_new)
    l_ref[rows, :] = alpha * l_ref[rows, :] + jnp.sum(p, axis=-1, keepdims=True)
    m_ref[rows, :] = m_new
    return p, alpha


def _diff_kernel(slope_ref, lam_ref, gsub_ref, q1_ref, q2_ref, k_ref, v_ref, o_ref,
                 m_ref, l_ref, acc_ref, *, lambda_init):
    tq, tk = DIFF_TQ, KEY_TILE
    qi = pl.program_id(2)
    q0 = qi * tq
    slope = slope_ref[0][:, :1]
    m_ref[...] = jnp.full(m_ref.shape, NEG, F32)
    l_ref[...] = jnp.zeros(l_ref.shape, F32)
    acc_ref[...] = jnp.zeros(acc_ref.shape, F32)
    qs = (q1_ref[0], q2_ref[0])
    kcol = lax.broadcasted_iota(jnp.int32, (1, tk), 1)
    qrow = lax.broadcasted_iota(jnp.int32, (tq, 1), 0)

    def tile(j, masked):
        k0 = pl.multiple_of(j * tk, tk)
        kt = k_ref[0, pl.ds(k0, tk), :]
        vt = v_ref[0, pl.ds(k0, tk), :]
        bias = slope * (kcol + (k0 - q0)).astype(F32)
        if masked:
            causal = (kcol + (k0 - q0)) <= qrow
        for c in range(2):
            rows = slice(c * tq, (c + 1) * tq)
            s = _dot_nt(qs[c], kt) + bias
            if masked:
                s = jnp.where(causal, s, NEG)
            p, alpha = _online(s, m_ref, l_ref, rows)
            acc_ref[rows, :] = alpha * acc_ref[rows, :] + jnp.dot(p.astype(BF16), vt,
                                                                  preferred_element_type=F32)

    n_full = q0 // tk

    def body(j, carry):
        tile(j, False)
        return carry

    lax.fori_loop(0, n_full, body, 0)
    tile(n_full, True)

    lv = lam_ref[...]
    lam = (jnp.exp(jnp.sum(lv[0:1] * lv[1:2], axis=-1, keepdims=True))
           - jnp.exp(jnp.sum(lv[2:3] * lv[3:4], axis=-1, keepdims=True)) + lambda_init)
    o1 = acc_ref[0:tq, :] / l_ref[0:tq, :]
    o2 = acc_ref[tq:2 * tq, :] / l_ref[tq:2 * tq, :]
    o = o1 - lam * o2
    o_ref[0] = (_rms(o) * gsub_ref[...] * (1.0 - lambda_init)).astype(o_ref.dtype)


def _diff_attention(dq, dk, dv, lam_vecs, g_subln, lambda_init, B, S):
    tq = DIFF_TQ
    slopes = jnp.asarray(np.repeat(_alibi_slopes(DIFF_HEADS)[:, None, None], LANES, axis=2))
    lam = jnp.zeros((8, LANES), F32).at[:4, :HEAD_DIM].set(jnp.stack(lam_vecs))
    return pl.pallas_call(
        functools.partial(_diff_kernel, lambda_init=lambda_init),
        grid=(B, DIFF_HEADS, S // tq),
        in_specs=[pl.BlockSpec((1, 1, LANES), lambda b, h, i: (h, 0, 0)),
                  pl.BlockSpec((8, LANES), lambda b, h, i: (0, 0)),
                  pl.BlockSpec((1, LANES), lambda b, h, i: (0, 0)),
                  pl.BlockSpec((1, tq, LANES), lambda b, h, i: (b, i, 2 * h)),
                  pl.BlockSpec((1, tq, LANES), lambda b, h, i: (b, i, 2 * h + 1)),
                  pl.BlockSpec((1, S, LANES), lambda b, h, i: (b, 0, h)),
                  pl.BlockSpec((1, S, LANES), lambda b, h, i: (b, 0, h))],
        out_specs=pl.BlockSpec((1, tq, LANES), lambda b, h, i: (b, i, h)),
        out_shape=jax.ShapeDtypeStruct((B, S, DIFF_HEADS * LANES), BF16),
        scratch_shapes=[pltpu.VMEM((2 * tq, 1), F32), pltpu.VMEM((2 * tq, 1), F32),
                        pltpu.VMEM((2 * tq, LANES), F32)],
        compiler_params=_params(("parallel", "parallel", "parallel")),
        name="diff_attention",
    )(slopes, lam, g_subln.reshape(1, LANES).astype(F32), dq, dq, dk, dv)


def _nsa_kernel(slope_ref, q_ref, kc_ref, vc_ref, ka_ref, sv_ref, wk_ref, wv_ref, g_ref, ov_ref, o_ref,
                qa_ref, m_ref, l_ref, acc_ref, p_ref, out_ref):
    tq, tk, G = NSA_TQ, KEY_TILE, NSA_GROUP
    n_cmp = kc_ref.shape[1]
    n_win = WINDOW + tq
    grp = pl.program_id(1)
    qi = pl.program_id(2)
    q0 = pl.multiple_of(qi * tq, tq)
    sl = slope_ref[0]
    slope = [sl[i:i + 1, :1] for i in range(G)]
    slab = [slice(i * tq, (i + 1) * tq) for i in range(G)]
    qrow = lax.broadcasted_iota(jnp.int32, (tq, 1), 0)
    gts = g_ref[0]

    def gate(i, branch):
        a = 3 * i + branch
        b = 3 * (G + i) + branch
        return jnp.where(grp == 0, gts[:, a:a + 1], gts[:, b:b + 1])

    for i in range(G):
        qa_ref[slab[i], :LANES] = q_ref[0, :, i * LANES:(i + 1) * LANES]
    qs = qa_ref[:, :LANES]

    ccol = lax.broadcasted_iota(jnp.int32, (1, n_cmp), 1)
    cend = ccol * CMP_STRIDE + (CMP_BLOCK - 1)
    cmask = (cend - q0) <= qrow
    cbias = (cend - q0).astype(F32)
    sc = _dot_nt(qs, kc_ref[0])
    psum = jnp.zeros((tq, n_cmp), F32)
    for i in range(G):
        s = jnp.where(cmask, sc[slab[i]] + slope[i] * cbias, NEG)
        m = jnp.max(s, axis=-1, keepdims=True)
        e = jnp.where(cmask, jnp.exp(s - m), 0.0)
        p = e / jnp.maximum(jnp.sum(e, axis=-1, keepdims=True), 1e-30)
        psum = psum + p
        p_ref[slab[i], :n_cmp] = p.astype(BF16)
    oc = jnp.dot(p_ref[:, :n_cmp], vc_ref[0], preferred_element_type=F32)
    for i in range(G):
        out_ref[slab[i], :] = gate(i, 0) * oc[slab[i]]

    hi = psum.astype(BF16)
    r1 = psum - hi.astype(F32)
    mid = r1.astype(BF16)
    lo = (r1 - mid.astype(F32)).astype(BF16)
    ov = ov_ref[...]
    imp = (jnp.dot(hi, ov, preferred_element_type=F32) + jnp.dot(mid, ov, preferred_element_type=F32)
           + jnp.dot(lo, ov, preferred_element_type=F32))
    n_slc = imp.shape[1]
    lane = lax.broadcasted_iota(jnp.int32, (tq, n_slc), 1)
    cur = (qrow + q0) // SLC_BLOCK
    forced = jnp.where(lane == 0, 1.0, jnp.where(lane == cur, 1.0, jnp.where(lane == cur - 1, 1.0, 0.0)))
    score = jnp.where(lane <= cur, imp + FORCE_BONUS * forced, NEG)
    lanef = lane.astype(F32)
    sel = jnp.zeros((tq, n_slc), F32)
    for _ in range(SLC_TOP):
        mx = jnp.max(score, axis=-1, keepdims=True)
        idx = jnp.min(jnp.where(score == mx, lanef, float(n_slc)), axis=-1, keepdims=True)
        hit = lanef == idx
        sel = jnp.where(hit, 1.0, sel)
        score = jnp.where(hit, -jnp.inf, score)
    blockmask = ((sel - 1.0) * 1e30).astype(BF16)
    for i in range(G):
        qa_ref[slab[i], LANES:] = blockmask

    m_ref[...] = jnp.full(m_ref.shape, NEG, F32)
    l_ref[...] = jnp.zeros(l_ref.shape, F32)
    acc_ref[...] = jnp.zeros(acc_ref.shape, F32)
    kcol = lax.broadcasted_iota(jnp.int32, (1, tk), 1)

    def tile(j, masked):
        k0 = pl.multiple_of(j * tk, tk)
        kt = ka_ref[0, pl.ds(k0, tk), :]
        vt = sv_ref[0, pl.ds(k0, tk), :]
        s_all = _dot_nt(qa_ref[...], kt)
        kb = (kcol + (k0 - q0)).astype(F32)
        if masked:
            causal = (kcol + (k0 - q0)) <= qrow
        alphas = []
        for i in range(G):
            s = s_all[slab[i]] + slope[i] * kb
            if masked:
                s = jnp.where(causal, s, NEG)
            p, alpha = _online(s, m_ref, l_ref, slab[i])
            p_ref[slab[i], :tk] = p.astype(BF16)
            alphas.append(alpha)
        pv = jnp.dot(p_ref[:, :tk], vt, preferred_element_type=F32)
        for i in range(G):
            acc_ref[slab[i], :] = alphas[i] * acc_ref[slab[i], :] + pv[slab[i]]

    n_full = q0 // tk

    def body(j, carry):
        tile(j, False)
        return carry

    lax.fori_loop(0, n_full, body, 0)
    tile(n_full, True)
    for i in range(G):
        out_ref[slab[i], :] += gate(i, 1) * (acc_ref[slab[i], :] / l_ref[slab[i], :])

    kw = wk_ref[0, pl.ds(q0, n_win), :]
    vw = wv_ref[0, pl.ds(q0, n_win), :]
    wcol = lax.broadcasted_iota(jnp.int32, (1, n_win), 1)
    dw = qrow - wcol + WINDOW
    wmask = (dw >= 0) & (dw < WINDOW) & ((wcol + q0) >= WINDOW)
    wbias = (wcol - WINDOW).astype(F32)
    sw = _dot_nt(qs, kw)
    for i in range(G):
        s = jnp.where(wmask, sw[slab[i]] + slope[i] * wbias, NEG)
        m = jnp.max(s, axis=-1, keepdims=True)
        e = jnp.where(wmask, jnp.exp(s - m), 0.0)
        p = e / jnp.maximum(jnp.sum(e, axis=-1, keepdims=True), 1e-30)
        p_ref[slab[i], :n_win] = p.astype(BF16)
    ow = jnp.dot(p_ref[:, :n_win], vw, preferred_element_type=F32)
    for i in range(G):
        out_ref[slab[i], :] += gate(i, 2) * ow[slab[i]]

    lane_o = lax.broadcasted_iota(jnp.int32, (tq, LANES), 1)
    for pair in range(G // 2):
        a = out_ref[slab[2 * pair], :]
        b = out_ref[slab[2 * pair + 1], :]
        lo_half = jnp.where(grp == 0, a, pltpu.roll(a, HEAD_DIM, 1))
        hi_half = jnp.where(grp == 0, pltpu.roll(b, HEAD_DIM, 1), b)
        o_ref[0, :, pair * LANES:(pair + 1) * LANES] = jnp.where(lane_o < HEAD_DIM, lo_half, hi_half).astype(
            o_ref.dtype)


def _nsa_attention(nq, kc, vc, k_aug, sv, wk_pad, wv_pad, gates, overlap, B, S):
    tq = NSA_TQ
    G = NSA_GROUP
    n_cmp = kc.shape[1]
    n_win = WINDOW + tq
    slopes = jnp.asarray(np.repeat(_alibi_slopes(NSA_HEADS).reshape(NSA_KV_HEADS, G, 1), LANES, axis=2))
    whole = lambda b, g, i: (b, 0, 0)
    return pl.pallas_call(
        _nsa_kernel,
        grid=(B, NSA_KV_HEADS, S // tq),
        in_specs=[pl.BlockSpec((1, G, LANES), lambda b, g, i: (g, 0, 0)),
                  pl.BlockSpec((1, tq, G * LANES), lambda b, g, i: (b, i, g)),
                  pl.BlockSpec((1, n_cmp, LANES), whole),
                  pl.BlockSpec((1, n_cmp, LANES), whole),
                  pl.BlockSpec((1, S, 2 * LANES), whole),
                  pl.BlockSpec((1, S, LANES), whole),
                  pl.BlockSpec((1, S + WINDOW, LANES), whole),
                  pl.BlockSpec((1, S + WINDOW, LANES), whole),
                  pl.BlockSpec((1, tq, LANES), lambda b, g, i: (b, i, 0)),
                  pl.BlockSpec(overlap.shape, lambda b, g, i: (0, 0))],
        out_specs=pl.BlockSpec((1, tq, G * HEAD_DIM), lambda b, g, i: (b, i, g)),
        out_shape=jax.ShapeDtypeStruct((B, S, NSA_HEADS * HEAD_DIM), BF16),
        scratch_shapes=[pltpu.VMEM((G * tq, 2 * LANES), BF16),
                        pltpu.VMEM((G * tq, 1), F32), pltpu.VMEM((G * tq, 1), F32),
                        pltpu.VMEM((G * tq, LANES), F32),
                        pltpu.VMEM((G * tq, n_win), BF16),
                        pltpu.VMEM((G * tq, LANES), F32)],
        compiler_params=_params(("parallel", "parallel", "parallel")),
        name="nsa_attention",
    )(slopes, nq, kc, vc, k_aug, sv, wk_pad, wv_pad, gates, overlap)


def _outproj_kernel(oa_ref, ob_ref, x_ref, wo_ref, gpm_ref, gta_ref, gpf_ref, sc_ref, sh_ref, wr_ref, br_ref,
                    x1_ref, h2_ref, rt_ref):
    half = oa_ref.shape[1]
    mix = (jnp.dot(oa_ref[...], wo_ref[:half], preferred_element_type=F32)
           + jnp.dot(ob_ref[...], wo_ref[half:], preferred_element_type=F32))
    x1 = x_ref[...] + gta_ref[0] * (_rms(mix) * gpm_ref[...])
    x1_ref[...] = x1
    h2 = _rms(x1) * gpf_ref[...] * (1.0 + sc_ref[0]) + sh_ref[0]
    h2_ref[...] = h2
    logits = jnp.dot(h2, wr_ref[...], preferred_element_type=F32, precision=HIGHEST) + br_ref[...]
    tm = logits.shape[0]
    lane = lax.broadcasted_iota(jnp.int32, (tm, LANES), 1)
    lanef = lane.astype(F32)
    is_grp = (lane >= N_EXPERTS) & (lane < N_EXPERTS + N_GROUPS)
    lg = jnp.where(is_grp, logits, -jnp.inf)
    mg = jnp.max(lg, axis=-1, keepdims=True)
    g_lane = jnp.min(jnp.where(lg == mg, lanef, 1e9), axis=-1, keepdims=True)
    p_grp = 1.0 / jnp.sum(jnp.where(is_grp, jnp.exp(lg - mg), 0.0), axis=-1, keepdims=True)
    grp = g_lane.astype(jnp.int32) - N_EXPERTS
    in_grp = (lane < N_EXPERTS) & ((lane // EXPERTS_PER_GROUP) == grp)
    le = jnp.where(in_grp, logits, -jnp.inf)
    v1 = jnp.max(le, axis=-1, keepdims=True)
    i1 = jnp.min(jnp.where(le == v1, lanef, 1e9), axis=-1, keepdims=True)
    le2 = jnp.where(lanef == i1, -jnp.inf, le)
    v2 = jnp.max(le2, axis=-1, keepdims=True)
    i2 = jnp.min(jnp.where(le2 == v2, lanef, 1e9), axis=-1, keepdims=True)
    e2 = jnp.exp(v2 - v1)
    w1 = p_grp / (1.0 + e2)
    w2 = p_grp * e2 / (1.0 + e2)
    rt_ref[...] = jnp.where(lane == 0, i1, jnp.where(lane == 1, i2, jnp.where(lane == 2, w1,
                            jnp.where(lane == 3, w2, 0.0))))


def _out_proj(o_a, o_b, x2, w_out, g_post_mix, gt_a, g_pre_ffn, sc_f, sh_f, w_r, b_r, B, S):
    T, D = x2.shape
    tm = ROW_TILE
    per_b = S // tm
    row = lambda i: (i, 0)
    bat = lambda i: (i // per_b, 0, 0)
    const = lambda i: (0, 0)
    return pl.pallas_call(
        _outproj_kernel,
        grid=(T // tm,),
        in_specs=[pl.BlockSpec((tm, o_a.shape[1]), row), pl.BlockSpec((tm, o_b.shape[1]), row),
                  pl.BlockSpec((tm, D), row), pl.BlockSpec((D, D), const),
                  pl.BlockSpec((1, D), const), pl.BlockSpec((1, 1, D), bat), pl.BlockSpec((1, D), const),
                  pl.BlockSpec((1, 1, D), bat), pl.BlockSpec((1, 1, D), bat),
                  pl.BlockSpec((D, LANES), const), pl.BlockSpec((1, LANES), const)],
        out_specs=[pl.BlockSpec((tm, D), row), pl.BlockSpec((tm, D), row), pl.BlockSpec((tm, LANES), row)],
        out_shape=[jax.ShapeDtypeStruct((T, D), F32), jax.ShapeDtypeStruct((T, D), F32),
                   jax.ShapeDtypeStruct((T, LANES), F32)],
        compiler_params=_params(("parallel",)),
        name="out_proj_router",
    )(o_a, o_b, x2, w_out, g_post_mix.reshape(1, D), gt_a.reshape(B, 1, D), g_pre_ffn.reshape(1, D),
      sc_f.reshape(B, 1, D), sh_f.reshape(B, 1, D), w_r, b_r)


def _slots_kernel(rt_ref, slot_ref, blk_ref, cnt_ref, base_ref, *, n_blk_rows):
    phase = pl.program_id(0)
    step = pl.program_id(1)
    tm = rt_ref.shape[0]
    rt = rt_ref[...]
    lane = lax.broadcasted_iota(jnp.int32, (tm, LANES), 1).astype(F32)
    i1 = rt[:, 0:1]
    i2 = rt[:, 1:2]
    hit1 = lane == i1
    hit2 = lane == i2
    onehot = jnp.where(hit1, 1.0, jnp.where(hit2, 1.0, 0.0))
    colsum = jnp.sum(onehot, axis=0, keepdims=True)

    @pl.when((phase == 0) & (step == 0))
    def _():
        cnt_ref[...] = jnp.zeros(cnt_ref.shape, F32)

    @pl.when(phase == 0)
    def _():
        cnt_ref[...] += colsum

    @pl.when((phase == 1) & (step == 0))
    def _():
        cnt = cnt_ref[...]
        padded = jnp.floor((cnt + (MOE_ROWS - 1)) / MOE_ROWS) * MOE_ROWS
        lane1 = lax.broadcasted_iota(jnp.int32, (1, LANES), 1)
        incl = padded
        sh = 1
        while sh < LANES:
            incl = incl + jnp.where(lane1 >= sh, pltpu.roll(incl, sh, 1), 0.0)
            sh *= 2
        base_ref[...] = incl - padded
        rows = lax.broadcasted_iota(jnp.int32, (n_blk_rows, LANES), 0).astype(F32) * MOE_ROWS
        lane2 = lax.broadcasted_iota(jnp.int32, (n_blk_rows, LANES), 1)
        ended = jnp.where(lane2 < N_EXPERTS, jnp.where(incl <= rows, 1.0, 0.0), 0.0)
        blk_e = jnp.minimum(jnp.sum(ended, axis=-1, keepdims=True), N_EXPERTS - 1.0)
        total = jnp.sum(jnp.where(lane1 == N_EXPERTS - 1, incl, 0.0), axis=-1, keepdims=True)
        n_used = total / MOE_ROWS
        blk_ref[...] = jnp.where(lane2 == 0, blk_e, jnp.where(lane2 == 1, n_used, 0.0)).astype(jnp.int32)

    @pl.when(phase == 1)
    def _():
        r = lax.broadcasted_iota(jnp.int32, (tm, tm), 0)
        c = lax.broadcasted_iota(jnp.int32, (tm, tm), 1)
        tri = jnp.where(c < r, 1.0, 0.0).astype(BF16)
        before = jnp.dot(tri, onehot.astype(BF16), preferred_element_type=F32) + base_ref[...]
        s1 = jnp.sum(jnp.where(hit1, before, 0.0), axis=-1, keepdims=True)
        s2 = jnp.sum(jnp.where(hit2, before, 0.0), axis=-1, keepdims=True)
        lane_i = lax.broadcasted_iota(jnp.int32, (tm, LANES), 1)
        slot_ref[...] = jnp.where(lane_i == 0, s1, jnp.where(lane_i == 1, s2, 0.0)).astype(jnp.int32)
        base_ref[...] += colsum


def _slots(route, n_blocks):
    T = route.shape[0]
    tm = ROW_TILE
    n_blk_rows = -(-n_blocks // 8) * 8
    slot, blk = pl.pallas_call(
        functools.partial(_slots_kernel, n_blk_rows=n_blk_rows),
        grid=(2, T // tm),
        in_specs=[pl.BlockSpec((tm, LANES), lambda p, i: (i, 0))],
        out_specs=[pl.BlockSpec((tm, LANES), lambda p, i: (i * p, 0)),
                   pl.BlockSpec((n_blk_rows, LANES), lambda p, i: (0, 0))],
        out_shape=[jax.ShapeDtypeStruct((T, LANES), jnp.int32),
                   jax.ShapeDtypeStruct((n_blk_rows, LANES), jnp.int32)],
        scratch_shapes=[pltpu.VMEM((1, LANES), F32), pltpu.VMEM((1, LANES), F32)],
        compiler_params=_params(("arbitrary", "arbitrary")),
        name="moe_slots",
    )(route)
    return slot[:, 0], slot[:, 1], blk[:n_blocks, 0], blk[0:1, 1]


def _dispatch_kernel(s1_ref, s2_ref, h_ref, xz_ref, xb_ref, sem):
    del xz_ref
    tm = h_ref.shape[0]
    base = pl.program_id(0) * tm

    def issue(r, carry):
        src = h_ref.at[pl.ds(r, 1), :]
        pltpu.make_async_copy(src, xb_ref.at[pl.ds(s1_ref[base + r], 1), :], sem).start()
        pltpu.make_async_copy(src, xb_ref.at[pl.ds(s2_ref[base + r], 1), :], sem).start()
        return carry

    lax.fori_loop(0, tm, issue, 0)

    def drain(r, carry):
        for _ in range(TOP_K):
            pltpu.make_async_copy(h_ref.at[pl.ds(0, 1), :], xb_ref.at[pl.ds(0, 1), :], sem).wait()
        return carry

    lax.fori_loop(0, tm, drain, 0)


def _dispatch(h2, slot1, slot2, n_rows):
    T, D = h2.shape
    tm = ROW_TILE
    xz = jnp.zeros((n_rows, D), h2.dtype)
    return pl.pallas_call(
        _dispatch_kernel,
        grid_spec=pltpu.PrefetchScalarGridSpec(
            num_scalar_prefetch=2,
            grid=(T // tm,),
            in_specs=[pl.BlockSpec((tm, D), lambda i, a, b: (i, 0)),
                      pl.BlockSpec(memory_space=pl.ANY)],
            out_specs=pl.BlockSpec(memory_space=pl.ANY),
            scratch_shapes=[pltpu.SemaphoreType.DMA(())]),
        out_shape=jax.ShapeDtypeStruct((n_rows, D), h2.dtype),
        input_output_aliases={3: 0},
        compiler_params=_params(("arbitrary",)),
        name="moe_dispatch",
    )(slot1, slot2, h2, xz)


def _expert_kernel(be_ref, nu_ref, x_ref, wg_ref, wu_ref, wd_ref, y_ref, wgb_ref, wub_ref, wdb_ref):
    i = pl.program_id(0)

    @pl.when(i >= nu_ref[0])
    def _():
        y_ref[...] = jnp.zeros(y_ref.shape, y_ref.dtype)

    @pl.when(i < nu_ref[0])
    def _():
        changed = be_ref[i] != be_ref[jnp.maximum(i - 1, 0)]

        @pl.when((i == 0) | changed)
        def _():
            wgb_ref[...] = wg_ref[0].astype(BF16)
            wub_ref[...] = wu_ref[0].astype(BF16)
            wdb_ref[...] = wd_ref[0].astype(BF16)

        x = x_ref[...].astype(BF16)
        g = jnp.dot(x, wgb_ref[...], preferred_element_type=F32)
        u = jnp.dot(x, wub_ref[...], preferred_element_type=F32)
        hid = (g * jax.nn.sigmoid(g) * u).astype(BF16)
        y_ref[...] = jnp.dot(hid, wdb_ref[...], preferred_element_type=F32).astype(y_ref.dtype)


def _experts(xbuf, blk_e, n_used, w_gate, w_up, w_down):
    R, D = xbuf.shape
    E, _, De = w_gate.shape
    n_blocks = R // MOE_ROWS
    blk = lambda i, be, nu: (jnp.minimum(i, nu[0] - 1), 0)
    wsel = lambda i, be, nu: (be[jnp.minimum(i, nu[0] - 1)], 0, 0)
    return pl.pallas_call(
        _expert_kernel,
        grid_spec=pltpu.PrefetchScalarGridSpec(
            num_scalar_prefetch=2,
            grid=(n_blocks,),
            in_specs=[pl.BlockSpec((MOE_ROWS, D), blk),
                      pl.BlockSpec((1, D, De), wsel), pl.BlockSpec((1, D, De), wsel),
                      pl.BlockSpec((1, De, D), wsel)],
            out_specs=pl.BlockSpec((MOE_ROWS, D), lambda i, be, nu: (i, 0)),
            scratch_shapes=[pltpu.VMEM((D, De), BF16), pltpu.VMEM((D, De), BF16), pltpu.VMEM((De, D), BF16)]),
        out_shape=jax.ShapeDtypeStruct((R, D), F32),
        compiler_params=_params(("arbitrary",)),
        name="moe_experts",
    )(blk_e, n_used, xbuf, w_gate, w_up, w_down)


def _combine_kernel(s1_ref, s2_ref, yb_ref, rt_ref, x1_ref, gtf_ref, gpo_ref, o_ref, ya_ref, yc_ref, sem):
    tm = x1_ref.shape[0]
    base = pl.program_id(0) * tm

    def issue(r, carry):
        pltpu.make_async_copy(yb_ref.at[pl.ds(s1_ref[base + r], 1), :], ya_ref.at[pl.ds(r, 1), :], sem).start()
        pltpu.make_async_copy(yb_ref.at[pl.ds(s2_ref[base + r], 1), :], yc_ref.at[pl.ds(r, 1), :], sem).start()
        return carry

    lax.fori_loop(0, tm, issue, 0)

    def drain(r, carry):
        for _ in range(TOP_K):
            pltpu.make_async_copy(yb_ref.at[pl.ds(0, 1), :], ya_ref.at[pl.ds(0, 1), :], sem).wait()
        return carry

    lax.fori_loop(0, tm, drain, 0)
    rt = rt_ref[...]
    f = rt[:, 2:3] * ya_ref[...] + rt[:, 3:4] * yc_ref[...]
    o_ref[...] = x1_ref[...] + gtf_ref[0] * (_rms(f) * gpo_ref[...])


def _combine(ybuf, slot1, slot2, route, x1, gt_f, g_post_ffn, B, S):
    T, D = x1.shape
    tm = ROW_TILE
    per_b = S // tm
    row = lambda i, a, b: (i, 0)
    return pl.pallas_call(
        _combine_kernel,
        grid_spec=pltpu.PrefetchScalarGridSpec(
            num_scalar_prefetch=2,
            grid=(T // tm,),
            in_specs=[pl.BlockSpec(memory_space=pl.ANY),
                      pl.BlockSpec((tm, LANES), row), pl.BlockSpec((tm, D), row),
                      pl.BlockSpec((1, 1, D), lambda i, a, b: (i // per_b, 0, 0)),
                      pl.BlockSpec((1, D), lambda i, a, b: (0, 0))],
            out_specs=pl.BlockSpec((tm, D), row),
            scratch_shapes=[pltpu.VMEM((tm, D), F32), pltpu.VMEM((tm, D), F32), pltpu.SemaphoreType.DMA(())]),
        out_shape=jax.ShapeDtypeStruct((T, D), F32),
        compiler_params=_params(("arbitrary",)),
        name="moe_combine",
    )(slot1, slot2, ybuf, route, x1, gt_f.reshape(B, 1, D), g_post_ffn.reshape(1, D))


def _overlap_matrix(S):
    n_rows = S // CMP_STRIDE
    n_cmp = (S - CMP_BLOCK) // CMP_STRIDE + 1
    n_slc = S // SLC_BLOCK
    cstart = np.arange(n_rows) * CMP_STRIDE
    sstart = np.arange(n_slc) * SLC_BLOCK
    ov = (cstart[:, None] < sstart[None, :] + SLC_BLOCK) & (cstart[:, None] + CMP_BLOCK > sstart[None, :])
    ov &= (np.arange(n_rows) < n_cmp)[:, None]
    return jnp.asarray(ov.astype(np.float32), dtype=BF16)


def _block_onehot(S):
    n_slc = S // SLC_BLOCK
    oh = (np.arange(S)[:, None] // SLC_BLOCK) == np.arange(n_slc)[None, :]
    return jnp.asarray(oh.astype(np.float32), dtype=BF16)


def kernel(x, c, w_ada, b_ada, g_pre_mix, g_post_mix, g_pre_ffn, g_post_ffn, w_in, lam_q1, lam_k1, lam_q2,
           lam_k2, g_subln, cmp_pos_k, cmp_pos_v, cmp_w1_k, cmp_w1_v, cmp_w2_k, cmp_w2_v, w_out,
           w_router_grp, b_router_grp, w_router_exp, b_router_exp, w_exp_gate, w_exp_up, w_exp_down):
    B, S, D = x.shape
    T = B * S
    depth = w_ada.shape[0]
    assert S // SLC_BLOCK == LANES and D == D_MODEL
    overlap = _overlap_matrix(S)
    blk_onehot = _block_onehot(S)
    n_blocks = (T * TOP_K) // MOE_ROWS + N_EXPERTS
    x2 = x.reshape(T, D)
    for l in range(depth):
        lambda_init = 0.8 - 0.6 * math.exp(-0.3 * l)
        mod = _adaln(c, w_ada[l], b_ada[l])
        sh_a, sc_a, gt_a, sh_f, sc_f, gt_f = jnp.split(mod, 6, axis=-1)

        dq, dk, dv, nq, kv, gates = _in_proj(x2, g_pre_mix[l], sc_a, sh_a, _build_w_big(w_in[l]), B, S)
        kv = kv.reshape(B, S, _W_KV)
        ck, cv, sk, sv, wk, wv = (kv[:, :, i * LANES:(i + 1) * LANES] for i in range(6))
        kc, vc = _compress(ck, cv, cmp_pos_k[l], cmp_pos_v[l], cmp_w1_k[l], cmp_w1_v[l],
                           cmp_w2_k[l], cmp_w2_v[l], B, S)
        o_a = _diff_attention(dq.reshape(B, S, _W_DQ), dk.reshape(B, S, _W_DK), dv.reshape(B, S, _W_DV),
                              (lam_q1[l], lam_k1[l], lam_q2[l], lam_k2[l]), g_subln[l], lambda_init, B, S)
        k_aug = jnp.concatenate([sk, jnp.broadcast_to(blk_onehot[None], (B, S, LANES))], axis=-1)
        pad = ((0, 0), (WINDOW, 0), (0, 0))
        o_b = _nsa_attention(nq.reshape(B, S, _W_NQ), kc, vc, k_aug, sv, jnp.pad(wk, pad), jnp.pad(wv, pad),
                             gates.reshape(B, S, _W_GT), overlap, B, S)

        w_r = jnp.zeros((D, LANES), F32).at[:, :N_EXPERTS].set(w_router_exp[l]).at[
            :, N_EXPERTS:N_EXPERTS + N_GROUPS].set(w_router_grp[l])
        b_r = jnp.zeros((1, LANES), F32).at[0, :N_EXPERTS].set(b_router_exp[l]).at[
            0, N_EXPERTS:N_EXPERTS + N_GROUPS].set(b_router_grp[l])
        x1, h2, route = _out_proj(o_a.reshape(T, -1), o_b.reshape(T, -1), x2, w_out[l].astype(BF16),
                                  g_post_mix[l], gt_a, g_pre_ffn[l], sc_f, sh_f, w_r, b_r, B, S)
        slot1, slot2, blk_e, n_used = _slots(route, n_blocks)
        xbuf = _dispatch(h2, slot1, slot2, n_blocks * MOE_ROWS)
        ybuf = _experts(xbuf, blk_e, n_used, w_exp_gate[l], w_exp_up[l], w_exp_down[l])
        x2 = _combine(ybuf, slot1, slot2, route, x1, gt_f, g_post_ffn[l], B, S)
    return x2.reshape(B, S, D)
```

```python
import functools
import math

import numpy as np
import jax
import jax.numpy as jnp
from jax import lax
from jax.experimental import pallas as pl
from jax.experimental.pallas import tpu as pltpu

F32 = jnp.float32
BF16 = jnp.bfloat16
HIGHEST = lax.Precision.HIGHEST

D_MODEL = 1024
HEAD_DIM = 64
LANES = 128
DIFF_HEADS = 4
NSA_HEADS = 8
NSA_KV_HEADS = 2
NSA_GROUP = NSA_HEADS // NSA_KV_HEADS
CMP_BLOCK = 32
CMP_STRIDE = 16
CMP_HIDDEN = 2 * HEAD_DIM
SLC_BLOCK = 64
SLC_TOP = 16
WINDOW = 512
N_GROUPS = 8
EXPERTS_PER_GROUP = 8
N_EXPERTS = N_GROUPS * EXPERTS_PER_GROUP
TOP_K = 2
D_EXPERT = D_MODEL // 2
NEG = -1e30
FORCE_BONUS = 1e4
EPS = 1e-6

ROW_TILE = 512
DIFF_TQ = 256
NSA_TQ = 128
KEY_TILE = 512
SOFTMAX_CHUNK = 64
MOE_ROWS = 256
VMEM_LIMIT = 48 * 1024 * 1024

_OFF_DQ, _OFF_DK, _OFF_DV, _OFF_NQ = 0, 512, 1024, 1536
_OFF_KV = 2048
_OFF_NG = 2816
_N_GATES = 3 * NSA_HEADS
_W_DQ, _W_DK, _W_DV, _W_NQ, _W_KV, _W_GT = 1024, 512, 512, 1024, 768, 128
_W_ALL = _W_DQ + _W_DK + _W_DV + _W_NQ + _W_KV + _W_GT


def _alibi_slopes(n):
    return np.array([2.0 ** (-8.0 * (i + 1) / n) for i in range(n)], dtype=np.float32)


def _dot_nt(a, b):
    return lax.dot_general(a, b, (((1,), (1,)), ((), ())), preferred_element_type=F32)


def _rms(v):
    return v * lax.rsqrt(jnp.mean(v * v, axis=-1, keepdims=True) + EPS)


def _params(sem, vmem=VMEM_LIMIT, flags=None):
    return pltpu.CompilerParams(dimension_semantics=sem, vmem_limit_bytes=vmem, flags=flags)


def _adaln_kernel(c_ref, w_ref, b_ref, o_ref):
    c = c_ref[...]
    s = c * jax.nn.sigmoid(c)
    o_ref[...] = jnp.dot(s, w_ref[...], preferred_element_type=F32, precision=HIGHEST) + b_ref[...]


def _adaln(c, w_ada, b_ada):
    B, D = c.shape
    rows = 8
    cp = jnp.zeros((rows, D), F32).at[:B].set(c)
    n_out = w_ada.shape[1]
    out = pl.pallas_call(
        _adaln_kernel,
        grid=(n_out // D,),
        in_specs=[pl.BlockSpec((rows, D), lambda j: (0, 0)),
                  pl.BlockSpec((D, D), lambda j: (0, j)),
                  pl.BlockSpec((1, D), lambda j: (0, j))],
        out_specs=pl.BlockSpec((rows, D), lambda j: (0, j)),
        out_shape=jax.ShapeDtypeStruct((rows, n_out), F32),
        compiler_params=_params(("parallel",)),
        name="adaln",
    )(cp, w_ada, b_ada.reshape(1, n_out))
    return out[:B]


def _inproj_kernel(x_ref, g_ref, sc_ref, sh_ref, w_ref, dq_ref, dk_ref, dv_ref, nq_ref, kv_ref, gt_ref):
    x = x_ref[...]
    h = (_rms(x) * g_ref[...] * (1.0 + sc_ref[0]) + sh_ref[0]).astype(BF16)
    off = 0
    for ref, width in ((dq_ref, _W_DQ), (dk_ref, _W_DK), (dv_ref, _W_DV), (nq_ref, _W_NQ), (kv_ref, _W_KV)):
        for c in range(0, width, 512):
            cw = min(512, width - c)
            ref[:, c:c + cw] = jnp.dot(h, w_ref[:, off + c:off + c + cw],
                                       preferred_element_type=F32).astype(ref.dtype)
        off += width
    gt_ref[...] = jax.nn.sigmoid(jnp.dot(h, w_ref[:, off:off + _W_GT], preferred_element_type=F32))


def _build_w_big(w_in):
    scale = HEAD_DIM ** -0.5
    z = jnp.zeros((D_MODEL, HEAD_DIM), F32)
    cols = []
    for h in range(DIFF_HEADS):
        q1 = w_in[:, _OFF_DQ + h * 128:_OFF_DQ + h * 128 + 64] * scale
        q2 = w_in[:, _OFF_DQ + h * 128 + 64:_OFF_DQ + h * 128 + 128] * scale
        cols += [q1, z, z, q2]
    cols.append(w_in[:, _OFF_DK:_OFF_DK + 512])
    cols.append(w_in[:, _OFF_DV:_OFF_DV + 512])
    for hq in range(NSA_HEADS):
        cols += [w_in[:, _OFF_NQ + hq * 64:_OFF_NQ + hq * 64 + 64] * scale, z]
    cols.append(w_in[:, _OFF_KV:_OFF_KV + _W_KV])
    cols.append(w_in[:, _OFF_NG:_OFF_NG + _N_GATES])
    cols.append(jnp.zeros((D_MODEL, _W_GT - _N_GATES), F32))
    return jnp.concatenate(cols, axis=1).astype(BF16)


def _in_proj(x2, g_pre, sc, sh, w_big, B, S):
    T, D = x2.shape
    tm = ROW_TILE
    per_b = S // tm
    row = lambda i: (i, 0)
    bat = lambda i: (i // per_b, 0, 0)
    widths = (_W_DQ, _W_DK, _W_DV, _W_NQ, _W_KV)
    outs = pl.pallas_call(
        _inproj_kernel,
        grid=(T // tm,),
        in_specs=[pl.BlockSpec((tm, D), row),
                  pl.BlockSpec((1, D), lambda i: (0, 0)),
                  pl.BlockSpec((1, 1, D), bat),
                  pl.BlockSpec((1, 1, D), bat),
                  pl.BlockSpec((D, _W_ALL), lambda i: (0, 0))],
        out_specs=[pl.BlockSpec((tm, w), row) for w in widths] + [pl.BlockSpec((tm, _W_GT), row)],
        out_shape=[jax.ShapeDtypeStruct((T, w), BF16) for w in widths]
        + [jax.ShapeDtypeStruct((T, _W_GT), F32)],
        compiler_params=_params(("parallel",)),
        name="in_proj",
    )(x2, g_pre.reshape(1, D), sc.reshape(B, 1, D), sh.reshape(B, 1, D), w_big)
    return outs


def _compress_kernel(r_ref, pos_ref, w1_ref, w2_ref, o_ref):
    half = CMP_STRIDE * HEAD_DIM
    pos = pos_ref[0]
    w1 = w1_ref[0]
    w2 = w2_ref[0]
    outs = []
    for h in range(NSA_KV_HEADS):
        r = r_ref[0, 0, h].astype(F32)
        a = jnp.dot((r + pos[:, :half]).astype(BF16), w1[:half], preferred_element_type=F32)
        b = jnp.dot((r + pos[:, half:]).astype(BF16), w1[half:], preferred_element_type=F32)
        n = a.shape[0]
        hid = a + pltpu.roll(b, n - 1, 0)
        g = jax.nn.gelu(hid)
        outs.append(jnp.dot(g.astype(BF16), w2, preferred_element_type=F32))
    o_ref[0, 0] = jnp.concatenate(outs, axis=-1).astype(o_ref.dtype)


def _compress(ck, cv, pos_k, pos_v, w1_k, w1_v, w2_k, w2_v, B, S):
    n_rows = S // CMP_STRIDE
    wide = CMP_STRIDE * HEAD_DIM

    def rows(t):
        return t.reshape(B, n_rows, CMP_STRIDE, NSA_KV_HEADS, HEAD_DIM).transpose(0, 3, 1, 2, 4).reshape(
            B, NSA_KV_HEADS, n_rows, wide)

    r = jnp.stack([rows(ck), rows(cv)])
    pos = jnp.stack([pos_k.reshape(1, -1), pos_v.reshape(1, -1)]).astype(F32)
    w1 = jnp.stack([w1_k, w1_v]).astype(BF16)
    w2 = jnp.stack([w2_k, w2_v]).astype(BF16)
    out = pl.pallas_call(
        _compress_kernel,
        grid=(2, B),
        in_specs=[pl.BlockSpec((1, 1, NSA_KV_HEADS, n_rows, wide), lambda t, b: (t, b, 0, 0, 0)),
                  pl.BlockSpec((1, 1, 2 * wide), lambda t, b: (t, 0, 0)),
                  pl.BlockSpec((1, 2 * wide, CMP_HIDDEN), lambda t, b: (t, 0, 0)),
                  pl.BlockSpec((1, CMP_HIDDEN, HEAD_DIM), lambda t, b: (t, 0, 0))],
        out_specs=pl.BlockSpec((1, 1, n_rows, NSA_KV_HEADS * HEAD_DIM), lambda t, b: (t, b, 0, 0)),
        out_shape=jax.ShapeDtypeStruct((2, B, n_rows, NSA_KV_HEADS * HEAD_DIM), BF16),
        compiler_params=_params(("parallel", "parallel")),
        name="nsa_compress",
    )(r, pos, w1, w2)
    return out[0], out[1]


def _pos_cols(pos, width):
    pos = np.asarray(pos)
    cols = np.zeros((pos.shape[0], width), np.float32)
    cols[:, 0] = pos // LANES
    cols[:, 1] = pos % LANES
    return jnp.asarray(cols, dtype=BF16)


def _slope_cols(slopes, width):
    cols = np.zeros((len(slopes), width), np.float32)
    cols[:, 0] = slopes * LANES
    cols[:, 1] = slopes
    return cols


def _flash_scratch(dv, n_cols):
    tk = KEY_TILE
    return [pltpu.VMEM((tk, n_cols), F32), pltpu.VMEM((tk, n_cols), F32),
            pltpu.VMEM((tk, n_cols), BF16), pltpu.VMEM((tk, n_cols), BF16),
            pltpu.VMEM((1, n_cols), F32), pltpu.VMEM((1, n_cols), F32),
            pltpu.VMEM((1, n_cols), F32), pltpu.VMEM((1, n_cols), F32), pltpu.VMEM((dv, n_cols), F32)]


def _flash_causal(q0, tq, qa_ref, k_tile, v_tile, scratch):
    s_a, s_b, p_a, p_b, a_a, a_b, m_ref, l_ref, acc_ref = scratch
    tk = KEY_TILE
    n_cols = qa_ref.shape[0]
    n_full = q0 // tk
    n = n_full + 1
    m_ref[...] = jnp.full(m_ref.shape, NEG, F32)
    l_ref[...] = jnp.zeros(l_ref.shape, F32)
    acc_ref[...] = jnp.zeros(acc_ref.shape, F32)
    p_b[...] = jnp.zeros(p_b.shape, BF16)
    a_b[...] = jnp.ones(a_b.shape, F32)

    def key_tile(t):
        t = jnp.clip(t, 0, n - 1)
        j = jnp.where(t == 0, n_full, t - 1)
        return j, pl.multiple_of(j * tk, tk)

    def scores(t, s_ref, masked):
        j, k0 = key_tile(t)
        s = _dot_nt(k_tile(j, k0), qa_ref[...])
        if masked:
            krow = lax.broadcasted_iota(jnp.int32, (tk, 1), 0)
            qcol = lax.broadcasted_iota(jnp.int32, (1, n_cols), 1) % tq
            s = jnp.where((krow + (k0 - q0)) <= qcol, s, NEG)
        s_ref[...] = s

    def softmax(s_ref, p_ref, a_ref):
        ch = SOFTMAX_CHUNK
        chunks = [slice(c, c + ch) for c in range(0, tk, ch)]
        mx = None
        for rows in chunks:
            part = jnp.max(s_ref[rows, :].reshape(ch // 8, 8, n_cols), axis=0)
            mx = part if mx is None else jnp.maximum(mx, part)
        m_old = m_ref[...]
        m_new = jnp.maximum(m_old, jnp.max(mx, axis=0, keepdims=True))
        sm = None
        for rows in chunks:
            p = jnp.exp(s_ref[rows, :] - m_new)
            p_ref[rows, :] = p.astype(BF16)
            part = jnp.sum(p.reshape(ch // 8, 8, n_cols), axis=0)
            sm = part if sm is None else sm + part
        alpha = jnp.exp(m_old - m_new)
        a_ref[...] = alpha
        l_ref[...] = alpha * l_ref[...] + jnp.sum(sm, axis=0, keepdims=True)
        m_ref[...] = m_new

    def values(t, p_ref, a_ref):
        j, k0 = key_tile(t)
        acc_ref[...] = a_ref[...] * acc_ref[...] + jnp.dot(v_tile(j, k0), p_ref[...], preferred_element_type=F32)

    scores(0, s_a, True)

    def pair(u, carry):
        i = 2 * u

        @pl.when(i < n)
        def _():
            scores(i + 1, s_b, False)
            softmax(s_a, p_a, a_a)
            values(i - 1, p_b, a_b)

        @pl.when(i + 1 < n)
        def _():
            scores(i + 2, s_a, False)
            softmax(s_b, p_b, a_b)
            values(i, p_a, a_a)

        return carry

    lax.fori_loop(0, (n + 1) // 2, pair, 0)

    @pl.when(n % 2 == 1)
    def _():
        values(n - 1, p_a, a_a)

    @pl.when(n % 2 == 0)
    def _():
        values(n - 1, p_b, a_b)

    return acc_ref, l_ref


def _softmax_cols(s, mask):
    s = jnp.where(mask, s, NEG)
    m = jnp.max(s, axis=0, keepdims=True)
    e = jnp.where(mask, jnp.exp(s - m), 0.0)
    return e / jnp.maximum(jnp.sum(e, axis=0, keepdims=True), 1e-30)


def _diff_kernel(qx_ref, lam_ref, gsub_ref, q1_ref, q2_ref, k_ref, v_ref, o_ref,
                 qa_ref, *scratch, lambda_init):
    tq = DIFF_TQ
    q0 = pl.program_id(2) * tq
    ext = jnp.broadcast_to(qx_ref[0], (tq, LANES)).astype(BF16)
    qa_ref[0:tq, :LANES] = q1_ref[0]
    qa_ref[tq:, :LANES] = q2_ref[0]
    qa_ref[0:tq, LANES:] = ext
    qa_ref[tq:, LANES:] = ext

    acc_ref, l_ref = _flash_causal(q0, tq, qa_ref,
                                   lambda j, k0: k_ref[0, pl.ds(k0, KEY_TILE), :],
                                   lambda j, k0: v_ref[0, 0, j],
                                   scratch)

    lv = lam_ref[...]
    lam = (jnp.exp(jnp.sum(lv[0:1] * lv[1:2], axis=-1, keepdims=True))
           - jnp.exp(jnp.sum(lv[2:3] * lv[3:4], axis=-1, keepdims=True)) + lambda_init)
    o_t = acc_ref[...] / l_ref[...]
    o_t = o_t[:, :tq] - lam * o_t[:, tq:]
    o_t = o_t * lax.rsqrt(jnp.mean(o_t * o_t, axis=0, keepdims=True) + EPS)
    o_ref[0] = (o_t.T * gsub_ref[...] * (1.0 - lambda_init)).astype(o_ref.dtype)


def _diff_attention(dq, dk, dv, lam_vecs, g_subln, lambda_init, B, S):
    tq, tk = DIFF_TQ, KEY_TILE
    H = DIFF_HEADS
    qx = jnp.asarray(_slope_cols(_alibi_slopes(H), LANES)).reshape(H, 1, LANES)
    kx = jnp.broadcast_to(_pos_cols(np.arange(S), LANES)[None, :, None, :], (B, S, H, LANES))
    k_aug = jnp.concatenate([dk.reshape(B, S, H, LANES), kx], axis=-1).reshape(B, S, H * 2 * LANES)
    v_t = dv.reshape(B, S // tk, tk, H, LANES).transpose(0, 3, 1, 4, 2)
    lam = jnp.zeros((8, LANES), F32).at[:4, :HEAD_DIM].set(jnp.stack(lam_vecs))
    return pl.pallas_call(
        functools.partial(_diff_kernel, lambda_init=lambda_init),
        grid=(B, H, S // tq),
        in_specs=[pl.BlockSpec((1, 1, LANES), lambda b, h, i: (h, 0, 0)),
                  pl.BlockSpec((8, LANES), lambda b, h, i: (0, 0)),
                  pl.BlockSpec((1, LANES), lambda b, h, i: (0, 0)),
                  pl.BlockSpec((1, tq, LANES), lambda b, h, i: (b, i, 2 * h)),
                  pl.BlockSpec((1, tq, LANES), lambda b, h, i: (b, i, 2 * h + 1)),
                  pl.BlockSpec((1, S, 2 * LANES), lambda b, h, i: (b, 0, h)),
                  pl.BlockSpec((1, 1, S // tk, LANES, tk), lambda b, h, i: (b, h, 0, 0, 0))],
        out_specs=pl.BlockSpec((1, tq, LANES), lambda b, h, i: (b, i, h)),
        out_shape=jax.ShapeDtypeStruct((B, S, H * LANES), BF16),
        scratch_shapes=[pltpu.VMEM((2 * tq, 2 * LANES), BF16)] + _flash_scratch(LANES, 2 * tq),
        compiler_params=_params(("parallel", "parallel", "parallel")),
        name="diff_attention",
    )(qx, lam, g_subln.reshape(1, LANES).astype(F32), dq, dq, k_aug, v_t)


def _nsa_kernel(qx_ref, q_ref, kc_ref, vc_ref, ka_ref, sv_ref, wk_ref, wv_ref, g_ref, ov_ref, o_ref,
                qa_ref, out_ref, *scratch):
    tq, G, dh = NSA_TQ, NSA_GROUP, HEAD_DIM
    n_cols = G * tq
    n_cmp = kc_ref.shape[2]
    n_win = WINDOW + tq
    grp = pl.program_id(1)
    qi = pl.program_id(2)
    q0 = pl.multiple_of(qi * tq, tq)
    cols = [slice(i * tq, (i + 1) * tq) for i in range(G)]
    qcol = lax.broadcasted_iota(jnp.int32, (1, n_cols), 1) % tq
    g_t = g_ref[0].T

    def gate(i, branch):
        a = 3 * i + branch
        b = 3 * (G + i) + branch
        return jnp.where(grp == 0, g_t[a:a + 1, :], g_t[b:b + 1, :])

    qx = qx_ref[0]
    for i in range(G):
        ext = jnp.broadcast_to(qx[i:i + 1], (tq, LANES)).astype(BF16)
        qa_ref[cols[i], :LANES] = q_ref[0, :, i * LANES:(i + 1) * LANES] + ext
    qs = qa_ref[:, :LANES]

    crow = lax.broadcasted_iota(jnp.int32, (n_cmp, 1), 0)
    cmask = (crow * CMP_STRIDE + (CMP_BLOCK - 1) - q0) <= qcol
    p = _softmax_cols(_dot_nt(kc_ref[0, 0], qs), cmask)
    oc = jnp.dot(vc_ref[0, 0], p.astype(BF16), preferred_element_type=F32)
    for i in range(G):
        out_ref[i * dh:(i + 1) * dh, :] = gate(i, 0) * oc[:, cols[i]]

    psum = p[:, cols[0]]
    for i in range(1, G):
        psum = psum + p[:, cols[i]]
    hi = psum.astype(BF16)
    r1 = psum - hi.astype(F32)
    mid = r1.astype(BF16)
    lo = (r1 - mid.astype(F32)).astype(BF16)
    ov = ov_ref[...]
    imp = (jnp.dot(ov, hi, preferred_element_type=F32) + jnp.dot(ov, mid, preferred_element_type=F32)
           + jnp.dot(ov, lo, preferred_element_type=F32))
    n_slc = imp.shape[0]
    blk = lax.broadcasted_iota(jnp.int32, (n_slc, tq), 0)
    cur = (lax.broadcasted_iota(jnp.int32, (1, tq), 1) + q0) // SLC_BLOCK
    forced = jnp.where(blk == 0, 1.0, jnp.where(blk == cur, 1.0, jnp.where(blk == cur - 1, 1.0, 0.0)))
    score = jnp.where(blk <= cur, imp + FORCE_BONUS * forced, NEG)
    blkf = blk.astype(F32)
    sel = jnp.zeros((n_slc, tq), F32)
    for _ in range(SLC_TOP):
        mx = jnp.max(score, axis=0, keepdims=True)
        idx = jnp.min(jnp.where(score == mx, blkf, float(n_slc)), axis=0, keepdims=True)
        hit = blkf == idx
        sel = jnp.where(hit, 1.0, sel)
        score = jnp.where(hit, -jnp.inf, score)
    blockmask = ((sel.T - 1.0) * 1e30).astype(BF16)
    for i in range(G):
        qa_ref[cols[i], LANES:] = blockmask

    acc_ref, l_ref = _flash_causal(q0, tq, qa_ref,
                                   lambda j, k0: ka_ref[0, 0, pl.ds(k0, KEY_TILE), :],
                                   lambda j, k0: sv_ref[0, 0, j],
                                   scratch)
    os_ = acc_ref[...] / l_ref[...]
    for i in range(G):
        out_ref[i * dh:(i + 1) * dh, :] += gate(i, 1) * os_[:, cols[i]]

    kw = wk_ref[0, 0, pl.ds(q0, n_win), :]
    vw = jnp.concatenate([wv_ref[0, 0, qi + c] for c in range(n_win // LANES)], axis=1)
    wrow = lax.broadcasted_iota(jnp.int32, (n_win, 1), 0)
    dw = qcol - wrow + WINDOW
    wmask = (dw >= 0) & (dw < WINDOW) & ((wrow + q0) >= WINDOW)
    p = _softmax_cols(_dot_nt(kw, qs), wmask)
    ow = jnp.dot(vw, p.astype(BF16), preferred_element_type=F32)
    for i in range(G):
        out_ref[i * dh:(i + 1) * dh, :] += gate(i, 2) * ow[:, cols[i]]

    o_ref[0] = out_ref[...].T.astype(o_ref.dtype)


def _nsa_attention(nq, kc, vc, sk, sv, wk, wv, gates, B, S):
    tq, tk, G, dh = NSA_TQ, KEY_TILE, NSA_GROUP, HEAD_DIM
    KV = NSA_KV_HEADS
    n_cmp = kc.shape[1]
    n_slc = S // SLC_BLOCK
    Sp = S + WINDOW
    qx = jnp.asarray(np.concatenate(
        [np.zeros((NSA_HEADS, dh), np.float32), _slope_cols(_alibi_slopes(NSA_HEADS), dh)], axis=1
    ).reshape(KV, G, LANES))

    def split(t):
        return t.reshape(B, t.shape[1], KV, dh).transpose(0, 2, 1, 3)

    def with_cols(t, extra):
        return jnp.concatenate([t, jnp.broadcast_to(extra[None, None], t.shape[:3] + extra.shape[1:])], axis=-1)

    pad = ((0, 0), (WINDOW, 0), (0, 0))
    blk_onehot = (np.arange(S)[:, None] // SLC_BLOCK) == np.arange(n_slc)[None, :]
    ka = with_cols(split(sk), jnp.concatenate(
        [_pos_cols(np.arange(S), dh), jnp.asarray(blk_onehot.astype(np.float32), dtype=BF16)], axis=1))
    kca = with_cols(split(kc), _pos_cols(np.arange(n_cmp) * CMP_STRIDE + CMP_BLOCK - 1, dh))
    wka = with_cols(split(jnp.pad(wk, pad)), _pos_cols(np.arange(Sp), dh))
    sv_t = split(sv).reshape(B, KV, S // tk, tk, dh).transpose(0, 1, 2, 4, 3)
    vc_t = split(vc).transpose(0, 1, 3, 2)
    wv_t = split(jnp.pad(wv, pad)).reshape(B, KV, Sp // LANES, LANES, dh).transpose(0, 1, 2, 4, 3)
    overlap_t = _overlap_matrix(S).T
    per_bg = lambda b, g, i: (b, g, 0, 0)
    per_bg5 = lambda b, g, i: (b, g, 0, 0, 0)
    return pl.pallas_call(
        _nsa_kernel,
        grid=(B, KV, S // tq),
        in_specs=[pl.BlockSpec((1, G, LANES), lambda b, g, i: (g, 0, 0)),
                  pl.BlockSpec((1, tq, G * LANES), lambda b, g, i: (b, i, g)),
                  pl.BlockSpec((1, 1, n_cmp, LANES), per_bg),
                  pl.BlockSpec((1, 1, dh, n_cmp), per_bg),
                  pl.BlockSpec((1, 1, S, 2 * LANES), per_bg),
                  pl.BlockSpec((1, 1, S // tk, dh, tk), per_bg5),
                  pl.BlockSpec((1, 1, Sp, LANES), per_bg),
                  pl.BlockSpec((1, 1, Sp // LANES, dh, LANES), per_bg5),
                  pl.BlockSpec((1, tq, LANES), lambda b, g, i: (b, i, 0)),
                  pl.BlockSpec((n_slc, n_cmp), lambda b, g, i: (0, 0))],
        out_specs=pl.BlockSpec((1, tq, G * dh), lambda b, g, i: (b, i, g)),
        out_shape=jax.ShapeDtypeStruct((B, S, NSA_HEADS * dh), BF16),
        scratch_shapes=[pltpu.VMEM((G * tq, 2 * LANES), BF16),
                        pltpu.VMEM((G * dh, tq), F32)] + _flash_scratch(dh, G * tq),
        compiler_params=_params(("parallel", "parallel", "parallel")),
        name="nsa_attention",
    )(qx, nq, kca, vc_t, ka, sv_t, wka, wv_t, gates, overlap_t)


def _outproj_kernel(oa_ref, ob_ref, x_ref, wo_ref, gpm_ref, gta_ref, gpf_ref, sc_ref, sh_ref, wr_ref, br_ref,
                    x1_ref, h2_ref, rt_ref):
    half = oa_ref.shape[1]
    mix = (jnp.dot(oa_ref[...], wo_ref[:half], preferred_element_type=F32)
           + jnp.dot(ob_ref[...], wo_ref[half:], preferred_element_type=F32))
    x1 = x_ref[...] + gta_ref[0] * (_rms(mix) * gpm_ref[...])
    x1_ref[...] = x1
    h2 = _rms(x1) * gpf_ref[...] * (1.0 + sc_ref[0]) + sh_ref[0]
    h2_ref[...] = h2
    logits = jnp.dot(h2, wr_ref[...], preferred_element_type=F32, precision=HIGHEST) + br_ref[...]
    tm = logits.shape[0]
    lane = lax.broadcasted_iota(jnp.int32, (tm, LANES), 1)
    lanef = lane.astype(F32)
    is_grp = (lane >= N_EXPERTS) & (lane < N_EXPERTS + N_GROUPS)
    lg = jnp.where(is_grp, logits, -jnp.inf)
    mg = jnp.max(lg, axis=-1, keepdims=True)
    g_lane = jnp.min(jnp.where(lg == mg, lanef, 1e9), axis=-1, keepdims=True)
    p_grp = 1.0 / jnp.sum(jnp.where(is_grp, jnp.exp(lg - mg), 0.0), axis=-1, keepdims=True)
    grp = g_lane.astype(jnp.int32) - N_EXPERTS
    in_grp = (lane < N_EXPERTS) & ((lane // EXPERTS_PER_GROUP) == grp)
    le = jnp.where(in_grp, logits, -jnp.inf)
    v1 = jnp.max(le, axis=-1, keepdims=True)
    i1 = jnp.min(jnp.where(le == v1, lanef, 1e9), axis=-1, keepdims=True)
    le2 = jnp.where(lanef == i1, -jnp.inf, le)
    v2 = jnp.max(le2, axis=-1, keepdims=True)
    i2 = jnp.min(jnp.where(le2 == v2, lanef, 1e9), axis=-1, keepdims=True)
    e2 = jnp.exp(v2 - v1)
    w1 = p_grp / (1.0 + e2)
    w2 = p_grp * e2 / (1.0 + e2)
    rt_ref[...] = jnp.where(lane == 0, i1, jnp.where(lane == 1, i2, jnp.where(lane == 2, w1,
                            jnp.where(lane == 3, w2, 0.0))))


def _out_proj(o_a, o_b, x2, w_out, g_post_mix, gt_a, g_pre_ffn, sc_f, sh_f, w_r, b_r, B, S):
    T, D = x2.shape
    tm = ROW_TILE
    per_b = S // tm
    row = lambda i: (i, 0)
    bat = lambda i: (i // per_b, 0, 0)
    const = lambda i: (0, 0)
    return pl.pallas_call(
        _outproj_kernel,
        grid=(T // tm,),
        in_specs=[pl.BlockSpec((tm, o_a.shape[1]), row), pl.BlockSpec((tm, o_b.shape[1]), row),
                  pl.BlockSpec((tm, D), row), pl.BlockSpec((D, D), const),
                  pl.BlockSpec((1, D), const), pl.BlockSpec((1, 1, D), bat), pl.BlockSpec((1, D), const),
                  pl.BlockSpec((1, 1, D), bat), pl.BlockSpec((1, 1, D), bat),
                  pl.BlockSpec((D, LANES), const), pl.BlockSpec((1, LANES), const)],
        out_specs=[pl.BlockSpec((tm, D), row), pl.BlockSpec((tm, D), row), pl.BlockSpec((tm, LANES), row)],
        out_shape=[jax.ShapeDtypeStruct((T, D), F32), jax.ShapeDtypeStruct((T, D), F32),
                   jax.ShapeDtypeStruct((T, LANES), F32)],
        compiler_params=_params(("parallel",)),
        name="out_proj_router",
    )(o_a, o_b, x2, w_out, g_post_mix.reshape(1, D), gt_a.reshape(B, 1, D), g_pre_ffn.reshape(1, D),
      sc_f.reshape(B, 1, D), sh_f.reshape(B, 1, D), w_r, b_r)


def _slots_kernel(rt_ref, slot_ref, blk_ref, cnt_ref, base_ref, *, n_blk_rows):
    phase = pl.program_id(0)
    step = pl.program_id(1)
    tm = rt_ref.shape[0]
    rt = rt_ref[...]
    lane = lax.broadcasted_iota(jnp.int32, (tm, LANES), 1).astype(F32)
    i1 = rt[:, 0:1]
    i2 = rt[:, 1:2]
    hit1 = lane == i1
    hit2 = lane == i2
    onehot = jnp.where(hit1, 1.0, jnp.where(hit2, 1.0, 0.0))
    colsum = jnp.sum(onehot, axis=0, keepdims=True)

    @pl.when((phase == 0) & (step == 0))
    def _():
        cnt_ref[...] = jnp.zeros(cnt_ref.shape, F32)

    @pl.when(phase == 0)
    def _():
        cnt_ref[...] += colsum

    @pl.when((phase == 1) & (step == 0))
    def _():
        cnt = cnt_ref[...]
        padded = jnp.floor((cnt + (MOE_ROWS - 1)) / MOE_ROWS) * MOE_ROWS
        lane1 = lax.broadcasted_iota(jnp.int32, (1, LANES), 1)
        incl = padded
        sh = 1
        while sh < LANES:
            incl = incl + jnp.where(lane1 >= sh, pltpu.roll(incl, sh, 1), 0.0)
            sh *= 2
        base_ref[...] = incl - padded
        rows = lax.broadcasted_iota(jnp.int32, (n_blk_rows, LANES), 0).astype(F32) * MOE_ROWS
        lane2 = lax.broadcasted_iota(jnp.int32, (n_blk_rows, LANES), 1)
        ended = jnp.where(lane2 < N_EXPERTS, jnp.where(incl <= rows, 1.0, 0.0), 0.0)
        blk_e = jnp.minimum(jnp.sum(ended, axis=-1, keepdims=True), N_EXPERTS - 1.0)
        total = jnp.sum(jnp.where(lane1 == N_EXPERTS - 1, incl, 0.0), axis=-1, keepdims=True)
        n_used = total / MOE_ROWS
        blk_ref[...] = jnp.where(lane2 == 0, blk_e, jnp.where(lane2 == 1, n_used, 0.0)).astype(jnp.int32)

    @pl.when(phase == 1)
    def _():
        r = lax.broadcasted_iota(jnp.int32, (tm, tm), 0)
        c = lax.broadcasted_iota(jnp.int32, (tm, tm), 1)
        tri = jnp.where(c < r, 1.0, 0.0).astype(BF16)
        before = jnp.dot(tri, onehot.astype(BF16), preferred_element_type=F32) + base_ref[...]
        s1 = jnp.sum(jnp.where(hit1, before, 0.0), axis=-1, keepdims=True)
        s2 = jnp.sum(jnp.where(hit2, before, 0.0), axis=-1, keepdims=True)
        lane_i = lax.broadcasted_iota(jnp.int32, (tm, LANES), 1)
        slot_ref[...] = jnp.where(lane_i == 0, s1, jnp.where(lane_i == 1, s2, 0.0)).astype(jnp.int32)
        base_ref[...] += colsum


def _slots(route, n_blocks):
    T = route.shape[0]
    tm = ROW_TILE
    n_blk_rows = -(-n_blocks // 8) * 8
    slot, blk = pl.pallas_call(
        functools.partial(_slots_kernel, n_blk_rows=n_blk_rows),
        grid=(2, T // tm),
        in_specs=[pl.BlockSpec((tm, LANES), lambda p, i: (i, 0))],
        out_specs=[pl.BlockSpec((tm, LANES), lambda p, i: (i * p, 0)),
                   pl.BlockSpec((n_blk_rows, LANES), lambda p, i: (0, 0))],
        out_shape=[jax.ShapeDtypeStruct((T, LANES), jnp.int32),
                   jax.ShapeDtypeStruct((n_blk_rows, LANES), jnp.int32)],
        scratch_shapes=[pltpu.VMEM((1, LANES), F32), pltpu.VMEM((1, LANES), F32)],
        compiler_params=_params(("arbitrary", "arbitrary")),
        name="moe_slots",
    )(route)
    return slot[:, 0], slot[:, 1], blk[:n_blocks, 0], blk[0:1, 1]


def _dispatch_kernel(s1_ref, s2_ref, h_ref, xz_ref, xb_ref, sem):
    del xz_ref
    tm = h_ref.shape[0]
    base = pl.program_id(0) * tm

    def issue(r, carry):
        src = h_ref.at[pl.ds(r, 1), :]
        pltpu.make_async_copy(src, xb_ref.at[pl.ds(s1_ref[base + r], 1), :], sem).start()
        pltpu.make_async_copy(src, xb_ref.at[pl.ds(s2_ref[base + r], 1), :], sem).start()
        return carry

    lax.fori_loop(0, tm, issue, 0)

    def drain(r, carry):
        for _ in range(TOP_K):
            pltpu.make_async_copy(h_ref.at[pl.ds(0, 1), :], xb_ref.at[pl.ds(0, 1), :], sem).wait()
        return carry

    lax.fori_loop(0, tm, drain, 0)


def _dispatch(h2, slot1, slot2, n_rows):
    T, D = h2.shape
    tm = ROW_TILE
    xz = jnp.zeros((n_rows, D), h2.dtype)
    return pl.pallas_call(
        _dispatch_kernel,
        grid_spec=pltpu.PrefetchScalarGridSpec(
            num_scalar_prefetch=2,
            grid=(T // tm,),
            in_specs=[pl.BlockSpec((tm, D), lambda i, a, b: (i, 0)),
                      pl.BlockSpec(memory_space=pl.ANY)],
            out_specs=pl.BlockSpec(memory_space=pl.ANY),
            scratch_shapes=[pltpu.SemaphoreType.DMA(())]),
        out_shape=jax.ShapeDtypeStruct((n_rows, D), h2.dtype),
        input_output_aliases={3: 0},
        compiler_params=_params(("arbitrary",)),
        name="moe_dispatch",
    )(slot1, slot2, h2, xz)


def _expert_kernel(be_ref, nu_ref, x_ref, wg_ref, wu_ref, wd_ref, y_ref, wgb_ref, wub_ref, wdb_ref):
    i = pl.program_id(0)

    @pl.when(i >= nu_ref[0])
    def _():
        y_ref[...] = jnp.zeros(y_ref.shape, y_ref.dtype)

    @pl.when(i < nu_ref[0])
    def _():
        changed = be_ref[i] != be_ref[jnp.maximum(i - 1, 0)]

        @pl.when((i == 0) | changed)
        def _():
            wgb_ref[...] = wg_ref[0].astype(BF16)
            wub_ref[...] = wu_ref[0].astype(BF16)
            wdb_ref[...] = wd_ref[0].astype(BF16)

        x = x_ref[...].astype(BF16)
        g = jnp.dot(x, wgb_ref[...], preferred_element_type=F32)
        u = jnp.dot(x, wub_ref[...], preferred_element_type=F32)
        hid = (g * jax.nn.sigmoid(g) * u).astype(BF16)
        y_ref[...] = jnp.dot(hid, wdb_ref[...], preferred_element_type=F32).astype(y_ref.dtype)


def _experts(xbuf, blk_e, n_used, w_gate, w_up, w_down):
    R, D = xbuf.shape
    E, _, De = w_gate.shape
    n_blocks = R // MOE_ROWS
    blk = lambda i, be, nu: (jnp.minimum(i, nu[0] - 1), 0)
    wsel = lambda i, be, nu: (be[jnp.minimum(i, nu[0] - 1)], 0, 0)
    return pl.pallas_call(
        _expert_kernel,
        grid_spec=pltpu.PrefetchScalarGridSpec(
            num_scalar_prefetch=2,
            grid=(n_blocks,),
            in_specs=[pl.BlockSpec((MOE_ROWS, D), blk),
                      pl.BlockSpec((1, D, De), wsel), pl.BlockSpec((1, D, De), wsel),
                      pl.BlockSpec((1, De, D), wsel)],
            out_specs=pl.BlockSpec((MOE_ROWS, D), lambda i, be, nu: (i, 0)),
            scratch_shapes=[pltpu.VMEM((D, De), BF16), pltpu.VMEM((D, De), BF16), pltpu.VMEM((De, D), BF16)]),
        out_shape=jax.ShapeDtypeStruct((R, D), F32),
        compiler_params=_params(("arbitrary",)),
        name="moe_experts",
    )(blk_e, n_used, xbuf, w_gate, w_up, w_down)


def _combine_kernel(s1_ref, s2_ref, yb_ref, rt_ref, x1_ref, gtf_ref, gpo_ref, o_ref, ya_ref, yc_ref, sem):
    tm = x1_ref.shape[0]
    base = pl.program_id(0) * tm

    def issue(r, carry):
        pltpu.make_async_copy(yb_ref.at[pl.ds(s1_ref[base + r], 1), :], ya_ref.at[pl.ds(r, 1), :], sem).start()
        pltpu.make_async_copy(yb_ref.at[pl.ds(s2_ref[base + r], 1), :], yc_ref.at[pl.ds(r, 1), :], sem).start()
        return carry

    lax.fori_loop(0, tm, issue, 0)

    def drain(r, carry):
        for _ in range(TOP_K):
            pltpu.make_async_copy(yb_ref.at[pl.ds(0, 1), :], ya_ref.at[pl.ds(0, 1), :], sem).wait()
        return carry

    lax.fori_loop(0, tm, drain, 0)
    rt = rt_ref[...]
    f = rt[:, 2:3] * ya_ref[...] + rt[:, 3:4] * yc_ref[...]
    o_ref[...] = x1_ref[...] + gtf_ref[0] * (_rms(f) * gpo_ref[...])


def _combine(ybuf, slot1, slot2, route, x1, gt_f, g_post_ffn, B, S):
    T, D = x1.shape
    tm = ROW_TILE
    per_b = S // tm
    row = lambda i, a, b: (i, 0)
    return pl.pallas_call(
        _combine_kernel,
        grid_spec=pltpu.PrefetchScalarGridSpec(
            num_scalar_prefetch=2,
            grid=(T // tm,),
            in_specs=[pl.BlockSpec(memory_space=pl.ANY),
                      pl.BlockSpec((tm, LANES), row), pl.BlockSpec((tm, D), row),
                      pl.BlockSpec((1, 1, D), lambda i, a, b: (i // per_b, 0, 0)),
                      pl.BlockSpec((1, D), lambda i, a, b: (0, 0))],
            out_specs=pl.BlockSpec((tm, D), row),
            scratch_shapes=[pltpu.VMEM((tm, D), F32), pltpu.VMEM((tm, D), F32), pltpu.SemaphoreType.DMA(())]),
        out_shape=jax.ShapeDtypeStruct((T, D), F32),
        compiler_params=_params(("arbitrary",)),
        name="moe_combine",
    )(slot1, slot2, ybuf, route, x1, gt_f.reshape(B, 1, D), g_post_ffn.reshape(1, D))


def _overlap_matrix(S):
    n_rows = S // CMP_STRIDE
    n_cmp = (S - CMP_BLOCK) // CMP_STRIDE + 1
    n_slc = S // SLC_BLOCK
    cstart = np.arange(n_rows) * CMP_STRIDE
    sstart = np.arange(n_slc) * SLC_BLOCK
    ov = (cstart[:, None] < sstart[None, :] + SLC_BLOCK) & (cstart[:, None] + CMP_BLOCK > sstart[None, :])
    ov &= (np.arange(n_rows) < n_cmp)[:, None]
    return jnp.asarray(ov.astype(np.float32), dtype=BF16)


def kernel(x, c, w_ada, b_ada, g_pre_mix, g_post_mix, g_pre_ffn, g_post_ffn, w_in, lam_q1, lam_k1, lam_q2,
           lam_k2, g_subln, cmp_pos_k, cmp_pos_v, cmp_w1_k, cmp_w1_v, cmp_w2_k, cmp_w2_v, w_out,
           w_router_grp, b_router_grp, w_router_exp, b_router_exp, w_exp_gate, w_exp_up, w_exp_down):
    B, S, D = x.shape
    T = B * S
    depth = w_ada.shape[0]
    assert S // SLC_BLOCK == LANES and D == D_MODEL
    n_blocks = (T * TOP_K) // MOE_ROWS + N_EXPERTS
    x2 = x.reshape(T, D)
    for l in range(depth):
        lambda_init = 0.8 - 0.6 * math.exp(-0.3 * l)
        mod = _adaln(c, w_ada[l], b_ada[l])
        sh_a, sc_a, gt_a, sh_f, sc_f, gt_f = jnp.split(mod, 6, axis=-1)

        dq, dk, dv, nq, kv, gates = _in_proj(x2, g_pre_mix[l], sc_a, sh_a, _build_w_big(w_in[l]), B, S)
        kv = kv.reshape(B, S, _W_KV)
        ck, cv, sk, sv, wk, wv = (kv[:, :, i * LANES:(i + 1) * LANES] for i in range(6))
        kc, vc = _compress(ck, cv, cmp_pos_k[l], cmp_pos_v[l], cmp_w1_k[l], cmp_w1_v[l],
                           cmp_w2_k[l], cmp_w2_v[l], B, S)
        o_a = _diff_attention(dq.reshape(B, S, _W_DQ), dk.reshape(B, S, _W_DK), dv.reshape(B, S, _W_DV),
                              (lam_q1[l], lam_k1[l], lam_q2[l], lam_k2[l]), g_subln[l], lambda_init, B, S)
        o_b = _nsa_attention(nq.reshape(B, S, _W_NQ), kc, vc, sk, sv, wk, wv,
                             gates.reshape(B, S, _W_GT), B, S)

        w_r = jnp.zeros((D, LANES), F32).at[:, :N_EXPERTS].set(w_router_exp[l]).at[
            :, N_EXPERTS:N_EXPERTS + N_GROUPS].set(w_router_grp[l])
        b_r = jnp.zeros((1, LANES), F32).at[0, :N_EXPERTS].set(b_router_exp[l]).at[
            0, N_EXPERTS:N_EXPERTS + N_GROUPS].set(b_router_grp[l])
        x1, h2, route = _out_proj(o_a.reshape(T, -1), o_b.reshape(T, -1), x2, w_out[l].astype(BF16),
                                  g_post_mix[l], gt_a, g_pre_ffn[l], sc_f, sh_f, w_r, b_r, B, S)
        slot1, slot2, blk_e, n_used = _slots(route, n_blocks)
        xbuf = _dispatch(h2, slot1, slot2, n_blocks * MOE_ROWS)
        ybuf = _experts(xbuf, blk_e, n_used, w_exp_gate[l], w_exp_up[l], w_exp_down[l])
        x2 = _combine(ybuf, slot1, slot2, route, x1, gt_f, g_post_ffn[l], B, S)
    return x2.reshape(B, S, D)
```

```python
import functools
import math

import numpy as np
import jax
import jax.numpy as jnp
from jax import lax
from jax.experimental import pallas as pl
from jax.experimental.pallas import tpu as pltpu

F32 = jnp.float32
BF16 = jnp.bfloat16
HIGHEST = lax.Precision.HIGHEST

D_MODEL = 1024
HEAD_DIM = 64
LANES = 128
DIFF_HEADS = 4
NSA_HEADS = 8
NSA_KV_HEADS = 2
NSA_GROUP = NSA_HEADS // NSA_KV_HEADS
CMP_BLOCK = 32
CMP_STRIDE = 16
CMP_HIDDEN = 2 * HEAD_DIM
SLC_BLOCK = 64
SLC_TOP = 16
WINDOW = 512
N_GROUPS = 8
EXPERTS_PER_GROUP = 8
N_EXPERTS = N_GROUPS * EXPERTS_PER_GROUP
TOP_K = 2
D_EXPERT = D_MODEL // 2
NEG = -1e30
FORCE_BONUS = 1e4
EPS = 1e-6

ROW_TILE = 512
KEY_TILE = 512
DIFF_TQ = 512
NSA_TQ = 128
SOFTMAX_CHUNK = 64
MOE_ROWS = 256
DMA_UNROLL = 8
VMEM_LIMIT = 48 * 1024 * 1024
assert ROW_TILE == KEY_TILE == WINDOW

_OFF_DQ, _OFF_DK, _OFF_DV, _OFF_NQ = 0, 512, 1024, 1536
_OFF_KV = 2048
_OFF_NG = 2816
_N_GATES = 3 * NSA_HEADS
_W_PROJ = 2048 + 768 + LANES


def _alibi_slopes(n):
    return np.array([2.0 ** (-8.0 * (i + 1) / n) for i in range(n)], dtype=np.float32)


def _dot_nt(a, b):
    return lax.dot_general(a, b, (((1,), (1,)), ((), ())), preferred_element_type=F32)


def _rms(v):
    return v * lax.rsqrt(jnp.mean(v * v, axis=-1, keepdims=True) + EPS)


def _params(sem, vmem=VMEM_LIMIT):
    return pltpu.CompilerParams(dimension_semantics=sem, vmem_limit_bytes=vmem)


def _pos_lanes(pos, lane):
    half = lane % HEAD_DIM
    return jnp.where(half == 0, pos // LANES, jnp.where(half == 1, pos % LANES, 0)).astype(F32)


def _slope_cols(slopes, width, at):
    cols = np.zeros((len(slopes), width), np.float32)
    cols[:, at] = slopes * LANES
    cols[:, at + 1] = slopes
    return cols


def _adaln_kernel(c_ref, w_ref, b_ref, o_ref):
    c = c_ref[...]
    s = c * jax.nn.sigmoid(c)
    o_ref[...] = jnp.dot(s, w_ref[...], preferred_element_type=F32, precision=HIGHEST) + b_ref[...]


def _adaln(c, w_ada, b_ada):
    B, D = c.shape
    rows = 8
    cp = jnp.zeros((rows, D), F32).at[:B].set(c)
    n_out = w_ada.shape[1]
    out = pl.pallas_call(
        _adaln_kernel,
        grid=(n_out // D,),
        in_specs=[pl.BlockSpec((rows, D), lambda j: (0, 0)),
                  pl.BlockSpec((D, D), lambda j: (0, j)),
                  pl.BlockSpec((1, D), lambda j: (0, j))],
        out_specs=pl.BlockSpec((rows, D), lambda j: (0, j)),
        out_shape=jax.ShapeDtypeStruct((rows, n_out), F32),
        compiler_params=_params(("parallel",)),
        name="adaln",
    )(cp, w_ada, b_ada.reshape(1, n_out))
    return out[:B]


def _inproj_kernel(x_ref, g_ref, sc_ref, sh_ref, w_ref,
                   dq_ref, dk_ref, dvt_ref, nq_ref, ckv_ref, ska_ref, svt_ref, wka_ref, wvt_ref, gt_ref):
    step = pl.program_id(1)
    tm = x_ref.shape[1]
    dh = HEAD_DIM

    @pl.when(step == 0)
    def _():
        wka_ref[...] = jnp.zeros(wka_ref.shape, wka_ref.dtype)
        wvt_ref[...] = jnp.zeros(wvt_ref.shape, wvt_ref.dtype)

    @pl.when(step > 0)
    def _():
        h = (_rms(x_ref[0]) * g_ref[...] * (1.0 + sc_ref[0]) + sh_ref[0]).astype(BF16)

        def proj(off, width):
            return jnp.dot(h, w_ref[:, off:off + width], preferred_element_type=F32)

        lane = lax.broadcasted_iota(jnp.int32, (tm, LANES), 1)
        pos = (step - 1) * tm + lax.broadcasted_iota(jnp.int32, (tm, 1), 0)
        low = lane < dh
        pcol = _pos_lanes(pos, lane)
        wcol = _pos_lanes(pos + WINDOW, lane)

        dq_ref[0] = proj(_OFF_DQ, 512).astype(BF16)
        k = proj(_OFF_DK, 512)
        v = proj(_OFF_DV, 512)
        for hh in range(DIFF_HEADS):
            dk_ref[0, :, hh * 256:hh * 256 + LANES] = k[:, hh * LANES:(hh + 1) * LANES].astype(BF16)
            dk_ref[0, :, hh * 256 + LANES:(hh + 1) * 256] = pcol.astype(BF16)
            dvt_ref[0, hh, 0] = v[:, hh * LANES:(hh + 1) * LANES].T.astype(BF16)
        nq_ref[0] = proj(_OFF_NQ, 512).astype(BF16)
        kv = proj(_OFF_KV, 768)
        ckv_ref[0] = kv[:, :2 * LANES].astype(BF16)
        sk, sv, wk, wv = (kv[:, i * LANES:(i + 1) * LANES] for i in range(2, 6))
        onehot = jnp.where(lane == pos // SLC_BLOCK, 1.0, 0.0).astype(BF16)
        sv_t = sv.T
        wv_t = wv.T
        for g, (keys_sel, keys_win) in enumerate(((jnp.where(low, sk, pcol), jnp.where(low, wk, wcol)),
                                                  (jnp.where(low, pcol, sk), jnp.where(low, wcol, wk)))):
            ska_ref[0, g, :, :LANES] = keys_sel.astype(BF16)
            ska_ref[0, g, :, LANES:] = onehot
            svt_ref[0, g, 0] = sv_t[g * dh:(g + 1) * dh].astype(BF16)
            wka_ref[0, g] = keys_win.astype(BF16)
            for c in range(tm // LANES):
                wvt_ref[0, g, c] = wv_t[g * dh:(g + 1) * dh, c * LANES:(c + 1) * LANES].astype(BF16)
        gt_ref[0] = jax.nn.sigmoid(proj(_OFF_NG, LANES))


def _build_w_proj(w_in):
    scale = HEAD_DIM ** -0.5
    nq = w_in[:, _OFF_NQ:_OFF_NQ + 512].reshape(D_MODEL, NSA_KV_HEADS, NSA_GROUP, HEAD_DIM)
    nq = nq.transpose(0, 2, 1, 3).reshape(D_MODEL, 512) * scale
    cols = [w_in[:, _OFF_DQ:_OFF_DQ + 512] * scale, w_in[:, _OFF_DK:_OFF_NQ], nq,
            w_in[:, _OFF_KV:_OFF_NG + _N_GATES], jnp.zeros((D_MODEL, LANES - _N_GATES), F32)]
    return jnp.concatenate(cols, axis=1).astype(BF16)


def _in_proj(x, g_pre, sc, sh, w_proj):
    B, S, D = x.shape
    tm = ROW_TILE
    n = S // tm
    KV, dh = NSA_KV_HEADS, HEAD_DIM
    tile = lambda b, s: (b, jnp.maximum(s - 1, 0), 0)
    out_shapes = [
        ((B, S, 512), (1, tm, 512), tile, BF16),
        ((B, S, DIFF_HEADS * 256), (1, tm, DIFF_HEADS * 256), tile, BF16),
        ((B, DIFF_HEADS, n, LANES, tm), (1, DIFF_HEADS, 1, LANES, tm),
         lambda b, s: (b, 0, jnp.maximum(s - 1, 0), 0, 0), BF16),
        ((B, S, 512), (1, tm, 512), tile, BF16),
        ((B, S, 2 * LANES), (1, tm, 2 * LANES), tile, BF16),
        ((B, KV, S, 2 * LANES), (1, KV, tm, 2 * LANES),
         lambda b, s: (b, 0, jnp.maximum(s - 1, 0), 0), BF16),
        ((B, KV, n, dh, tm), (1, KV, 1, dh, tm),
         lambda b, s: (b, 0, jnp.maximum(s - 1, 0), 0, 0), BF16),
        ((B, KV, S + WINDOW, LANES), (1, KV, tm, LANES), lambda b, s: (b, 0, s, 0), BF16),
        ((B, KV, (S + WINDOW) // LANES, dh, LANES), (1, KV, tm // LANES, dh, LANES),
         lambda b, s: (b, 0, s, 0, 0), BF16),
        ((B, S, LANES), (1, tm, LANES), tile, F32),
    ]
    return pl.pallas_call(
        _inproj_kernel,
        grid=(B, n + 1),
        in_specs=[pl.BlockSpec((1, tm, D), tile),
                  pl.BlockSpec((1, D), lambda b, s: (0, 0)),
                  pl.BlockSpec((1, 1, D), lambda b, s: (b, 0, 0)),
                  pl.BlockSpec((1, 1, D), lambda b, s: (b, 0, 0)),
                  pl.BlockSpec((D, _W_PROJ), lambda b, s: (0, 0))],
        out_specs=[pl.BlockSpec(blk, idx) for _, blk, idx, _ in out_shapes],
        out_shape=[jax.ShapeDtypeStruct(shape, dt) for shape, _, _, dt in out_shapes],
        compiler_params=_params(("parallel", "arbitrary")),
        name="in_proj",
    )(x, g_pre.reshape(1, D), sc.reshape(B, 1, D), sh.reshape(B, 1, D), w_proj)


def _compress_kernel(r_ref, pos_ref, w1_ref, w2_ref, kca_ref, vct_ref):
    half = CMP_STRIDE * HEAD_DIM
    dh = HEAD_DIM

    def mlp(t, h):
        r = r_ref[t, 0, h].astype(F32)
        pos = pos_ref[t]
        a = jnp.dot((r + pos[:, :half]).astype(BF16), w1_ref[t, :half], preferred_element_type=F32)
        b = jnp.dot((r + pos[:, half:]).astype(BF16), w1_ref[t, half:], preferred_element_type=F32)
        hid = a + pltpu.roll(b, a.shape[0] - 1, 0)
        return jnp.dot(jax.nn.gelu(hid).astype(BF16), w2_ref[t], preferred_element_type=F32)

    n_rows = r_ref.shape[3]
    lane = lax.broadcasted_iota(jnp.int32, (n_rows, LANES), 1)
    cend = lax.broadcasted_iota(jnp.int32, (n_rows, 1), 0) * CMP_STRIDE + (CMP_BLOCK - 1)
    pcol = _pos_lanes(cend, lane)
    kc = jnp.concatenate([mlp(0, 0), mlp(0, 1)], axis=-1)
    low = lane < dh
    kca_ref[0, 0] = jnp.where(low, kc, pcol).astype(BF16)
    kca_ref[0, 1] = jnp.where(low, pcol, kc).astype(BF16)
    vc_t = jnp.concatenate([mlp(1, 0), mlp(1, 1)], axis=-1).T
    for h in range(NSA_KV_HEADS):
        vct_ref[0, h] = vc_t[h * dh:(h + 1) * dh].astype(BF16)


def _compress(ckv, pos_k, pos_v, w1_k, w1_v, w2_k, w2_v):
    B, S, _ = ckv.shape
    KV, dh = NSA_KV_HEADS, HEAD_DIM
    n_rows = S // CMP_STRIDE
    wide = CMP_STRIDE * dh
    r = ckv.reshape(B, n_rows, CMP_STRIDE, 2, KV, dh).transpose(3, 0, 4, 1, 2, 5).reshape(2, B, KV, n_rows, wide)
    pos = jnp.stack([pos_k.reshape(1, -1), pos_v.reshape(1, -1)]).astype(F32)
    w1 = jnp.stack([w1_k, w1_v]).astype(BF16)
    w2 = jnp.stack([w2_k, w2_v]).astype(BF16)
    return pl.pallas_call(
        _compress_kernel,
        grid=(B,),
        in_specs=[pl.BlockSpec((2, 1, KV, n_rows, wide), lambda b: (0, b, 0, 0, 0)),
                  pl.BlockSpec((2, 1, 2 * wide), lambda b: (0, 0, 0)),
                  pl.BlockSpec((2, 2 * wide, CMP_HIDDEN), lambda b: (0, 0, 0)),
                  pl.BlockSpec((2, CMP_HIDDEN, dh), lambda b: (0, 0, 0))],
        out_specs=[pl.BlockSpec((1, KV, n_rows, LANES), lambda b: (b, 0, 0, 0)),
                   pl.BlockSpec((1, KV, dh, n_rows), lambda b: (b, 0, 0, 0))],
        out_shape=[jax.ShapeDtypeStruct((B, KV, n_rows, LANES), BF16),
                   jax.ShapeDtypeStruct((B, KV, dh, n_rows), BF16)],
        compiler_params=_params(("parallel",)),
        name="nsa_compress",
    )(r, pos, w1, w2)


def _flash_scratch(dv, n_cols):
    tk = KEY_TILE
    return [pltpu.VMEM((tk, n_cols), F32), pltpu.VMEM((tk, n_cols), F32),
            pltpu.VMEM((tk, n_cols), BF16), pltpu.VMEM((tk, n_cols), BF16),
            pltpu.VMEM((1, n_cols), F32), pltpu.VMEM((1, n_cols), F32),
            pltpu.VMEM((1, n_cols), F32), pltpu.VMEM((1, n_cols), F32), pltpu.VMEM((dv, n_cols), F32)]


def _flash_causal(q0, tq, qa_ref, k_tile, v_tile, scratch):
    s_a, s_b, p_a, p_b, a_a, a_b, m_ref, l_ref, acc_ref = scratch
    tk = KEY_TILE
    n_cols = qa_ref.shape[0]
    assert tk % tq == 0
    n_full = q0 // tk
    n = n_full + 1
    m_ref[...] = jnp.full(m_ref.shape, NEG, F32)
    l_ref[...] = jnp.zeros(l_ref.shape, F32)
    acc_ref[...] = jnp.zeros(acc_ref.shape, F32)
    p_b[...] = jnp.zeros(p_b.shape, BF16)
    a_b[...] = jnp.ones(a_b.shape, F32)

    def key_tile(t):
        t = jnp.clip(t, 0, n - 1)
        j = jnp.where(t == 0, n_full, t - 1)
        return j, pl.multiple_of(j * tk, tk)

    def scores(t, s_ref, masked):
        j, k0 = key_tile(t)
        s = _dot_nt(k_tile(j, k0), qa_ref[...])
        if masked:
            krow = lax.broadcasted_iota(jnp.int32, (tk, 1), 0)
            qcol = lax.broadcasted_iota(jnp.int32, (1, n_cols), 1) % tq
            s = jnp.where((krow + (k0 - q0)) <= qcol, s, NEG)
        s_ref[...] = s

    def softmax(s_ref, p_ref, a_ref):
        ch = SOFTMAX_CHUNK
        chunks = [slice(c, c + ch) for c in range(0, tk, ch)]
        mx = None
        for rows in chunks:
            part = jnp.max(s_ref[rows, :].reshape(ch // 8, 8, n_cols), axis=0)
            mx = part if mx is None else jnp.maximum(mx, part)
        m_old = m_ref[...]
        m_new = jnp.maximum(m_old, jnp.max(mx, axis=0, keepdims=True))
        sm = None
        for rows in chunks:
            p = jnp.exp(s_ref[rows, :] - m_new)
            p_ref[rows, :] = p.astype(BF16)
            part = jnp.sum(p.reshape(ch // 8, 8, n_cols), axis=0)
            sm = part if sm is None else sm + part
        alpha = jnp.exp(m_old - m_new)
        a_ref[...] = alpha
        l_ref[...] = alpha * l_ref[...] + jnp.sum(sm, axis=0, keepdims=True)
        m_ref[...] = m_new

    def values(t, p_ref, a_ref):
        j, k0 = key_tile(t)
        acc_ref[...] = a_ref[...] * acc_ref[...] + jnp.dot(v_tile(j, k0), p_ref[...], preferred_element_type=F32)

    scores(0, s_a, True)

    def pair(u, carry):
        i = 2 * u
        scores(i + 1, s_b, False)
        softmax(s_a, p_a, a_a)
        values(i - 1, p_b, a_b)

        @pl.when(i + 1 < n)
        def _():
            scores(i + 2, s_a, False)
            softmax(s_b, p_b, a_b)
            values(i, p_a, a_a)

        return carry

    lax.fori_loop(0, (n + 1) // 2, pair, 0)

    @pl.when(n % 2 == 1)
    def _():
        values(n - 1, p_a, a_a)

    @pl.when(n % 2 == 0)
    def _():
        values(n - 1, p_b, a_b)

    return acc_ref, l_ref


def _softmax_cols(s, mask):
    s = jnp.where(mask, s, NEG)
    m = jnp.max(s, axis=0, keepdims=True)
    e = jnp.where(mask, jnp.exp(s - m), 0.0)
    return e / jnp.maximum(jnp.sum(e, axis=0, keepdims=True), 1e-30)


def _diff_kernel(qx_ref, lam_ref, gsub_ref, q_ref, k_ref, v_ref, o_ref, qa_ref, *scratch, lambda_init):
    tq = DIFF_TQ
    q0 = pl.program_id(2) * tq
    q = q_ref[0].astype(F32)
    low = lax.broadcasted_iota(jnp.int32, (tq, LANES), 1) < HEAD_DIM
    ext = jnp.broadcast_to(qx_ref[0], (tq, LANES)).astype(BF16)
    qa_ref[0:tq, :LANES] = jnp.where(low, q, 0.0).astype(BF16)
    qa_ref[tq:, :LANES] = jnp.where(low, 0.0, q).astype(BF16)
    qa_ref[0:tq, LANES:] = ext
    qa_ref[tq:, LANES:] = ext

    acc_ref, l_ref = _flash_causal(q0, tq, qa_ref,
                                   lambda j, k0: k_ref[0, pl.ds(k0, KEY_TILE), :],
                                   lambda j, k0: v_ref[0, 0, j],
                                   scratch)

    lv = lam_ref[...]
    lam = (jnp.exp(jnp.sum(lv[0:1] * lv[1:2], axis=-1, keepdims=True))
           - jnp.exp(jnp.sum(lv[2:3] * lv[3:4], axis=-1, keepdims=True)) + lambda_init)
    o_t = acc_ref[...] / l_ref[...]
    o_t = o_t[:, :tq] - lam * o_t[:, tq:]
    o_t = o_t * lax.rsqrt(jnp.mean(o_t * o_t, axis=0, keepdims=True) + EPS)
    o_ref[0] = (o_t.T * gsub_ref[...] * (1.0 - lambda_init)).astype(o_ref.dtype)


def _diff_attention(dq, dk, dv_t, lam_vecs, g_subln, lambda_init):
    B, S, _ = dq.shape
    tq, tk = DIFF_TQ, KEY_TILE
    H = DIFF_HEADS
    qx = jnp.asarray(_slope_cols(_alibi_slopes(H), LANES, 0)).reshape(H, 1, LANES)
    lam = jnp.zeros((8, LANES), F32).at[:4, :HEAD_DIM].set(jnp.stack(lam_vecs))
    return pl.pallas_call(
        functools.partial(_diff_kernel, lambda_init=lambda_init),
        grid=(B, H, S // tq),
        in_specs=[pl.BlockSpec((1, 1, LANES), lambda b, h, i: (h, 0, 0)),
                  pl.BlockSpec((8, LANES), lambda b, h, i: (0, 0)),
                  pl.BlockSpec((1, LANES), lambda b, h, i: (0, 0)),
                  pl.BlockSpec((1, tq, LANES), lambda b, h, i: (b, i, h)),
                  pl.BlockSpec((1, S, 2 * LANES), lambda b, h, i: (b, 0, h)),
                  pl.BlockSpec((1, 1, S // tk, LANES, tk), lambda b, h, i: (b, h, 0, 0, 0))],
        out_specs=pl.BlockSpec((1, tq, LANES), lambda b, h, i: (b, i, h)),
        out_shape=jax.ShapeDtypeStruct((B, S, H * LANES), BF16),
        scratch_shapes=[pltpu.VMEM((2 * tq, 2 * LANES), BF16)] + _flash_scratch(LANES, 2 * tq),
        compiler_params=_params(("parallel", "parallel", "parallel")),
        name="diff_attention",
    )(qx, lam, g_subln.reshape(1, LANES).astype(F32), dq, dk, dv_t)


def _nsa_kernel(qx_ref, q_ref, kc_ref, vc_ref, ka_ref, sv_ref, wk_ref, wv_ref, g_ref, ov_ref, o_ref,
                qa_ref, out_ref, *scratch):
    tq, G, dh = NSA_TQ, NSA_GROUP, HEAD_DIM
    n_cols = G * tq
    n_cmp = kc_ref.shape[2]
    n_win = WINDOW + tq
    grp = pl.program_id(1)
    qi = pl.program_id(2)
    q0 = pl.multiple_of(qi * tq, tq)
    cols = [slice(i * tq, (i + 1) * tq) for i in range(G)]
    qcol = lax.broadcasted_iota(jnp.int32, (1, n_cols), 1) % tq
    g_t = g_ref[0].T

    def gate(i, branch):
        a = 3 * i + branch
        b = 3 * (G + i) + branch
        return jnp.where(grp == 0, g_t[a:a + 1, :], g_t[b:b + 1, :])

    lane_q = lax.broadcasted_iota(jnp.int32, (tq, LANES), 1)
    mine = (lane_q >= grp * dh) & (lane_q < (grp + 1) * dh)
    qx = qx_ref[0]
    for i in range(G):
        q = q_ref[0, :, i * LANES:(i + 1) * LANES].astype(F32)
        qa_ref[cols[i], :LANES] = jnp.where(mine, q, jnp.broadcast_to(qx[i:i + 1], (tq, LANES))).astype(BF16)
    qs = qa_ref[:, :LANES]

    crow = lax.broadcasted_iota(jnp.int32, (n_cmp, 1), 0)
    cmask = (crow * CMP_STRIDE + (CMP_BLOCK - 1) - q0) <= qcol
    p = _softmax_cols(_dot_nt(kc_ref[0, 0], qs), cmask)
    oc = jnp.dot(vc_ref[0, 0], p.astype(BF16), preferred_element_type=F32)
    for i in range(G):
        out_ref[i * dh:(i + 1) * dh, :] = gate(i, 0) * oc[:, cols[i]]

    psum = p[:, cols[0]]
    for i in range(1, G):
        psum = psum + p[:, cols[i]]
    hi = psum.astype(BF16)
    r1 = psum - hi.astype(F32)
    mid = r1.astype(BF16)
    lo = (r1 - mid.astype(F32)).astype(BF16)
    ov = ov_ref[...]
    imp = (jnp.dot(ov, hi, preferred_element_type=F32) + jnp.dot(ov, mid, preferred_element_type=F32)
           + jnp.dot(ov, lo, preferred_element_type=F32))
    n_slc = imp.shape[0]
    blk = lax.broadcasted_iota(jnp.int32, (n_slc, tq), 0)
    cur = (lax.broadcasted_iota(jnp.int32, (1, tq), 1) + q0) // SLC_BLOCK
    forced = jnp.where(blk == 0, 1.0, jnp.where(blk == cur, 1.0, jnp.where(blk == cur - 1, 1.0, 0.0)))
    score = jnp.where(blk <= cur, imp + FORCE_BONUS * forced, NEG)
    blkf = blk.astype(F32)
    sel = jnp.zeros((n_slc, tq), F32)
    for _ in range(SLC_TOP):
        mx = jnp.max(score, axis=0, keepdims=True)
        idx = jnp.min(jnp.where(score == mx, blkf, float(n_slc)), axis=0, keepdims=True)
        hit = blkf == idx
        sel = jnp.where(hit, 1.0, sel)
        score = jnp.where(hit, -jnp.inf, score)
    blockmask = ((sel.T - 1.0) * 1e30).astype(BF16)
    for i in range(G):
        qa_ref[cols[i], LANES:] = blockmask

    acc_ref, l_ref = _flash_causal(q0, tq, qa_ref,
                                   lambda j, k0: ka_ref[0, 0, pl.ds(k0, KEY_TILE), :],
                                   lambda j, k0: sv_ref[0, 0, j],
                                   scratch)
    os_ = acc_ref[...] / l_ref[...]
    for i in range(G):
        out_ref[i * dh:(i + 1) * dh, :] += gate(i, 1) * os_[:, cols[i]]

    kw = wk_ref[0, 0, pl.ds(q0, n_win), :]
    vw = jnp.concatenate([wv_ref[0, 0, qi + c] for c in range(n_win // LANES)], axis=1)
    wrow = lax.broadcasted_iota(jnp.int32, (n_win, 1), 0)
    dw = qcol - wrow + WINDOW
    wmask = (dw >= 0) & (dw < WINDOW) & ((wrow + q0) >= WINDOW)
    p = _softmax_cols(_dot_nt(kw, qs), wmask)
    ow = jnp.dot(vw, p.astype(BF16), preferred_element_type=F32)
    for i in range(G):
        out_ref[i * dh:(i + 1) * dh, :] += gate(i, 2) * ow[:, cols[i]]

    o_ref[0] = out_ref[...].T.astype(o_ref.dtype)


def _nsa_attention(nq, kca, vc_t, ska, sv_t, wka, wv_t, gates):
    B, S, _ = nq.shape
    tq, tk, G, dh = NSA_TQ, KEY_TILE, NSA_GROUP, HEAD_DIM
    KV = NSA_KV_HEADS
    n_cmp = kca.shape[2]
    n_slc = S // SLC_BLOCK
    Sp = S + WINDOW
    slopes = _alibi_slopes(NSA_HEADS)
    qx = jnp.asarray(np.stack([_slope_cols(slopes[:G], LANES, dh), _slope_cols(slopes[G:], LANES, 0)]))
    overlap_t = _overlap_matrix(S).T
    per_bg = lambda b, g, i: (b, g, 0, 0)
    per_bg5 = lambda b, g, i: (b, g, 0, 0, 0)
    return pl.pallas_call(
        _nsa_kernel,
        grid=(B, KV, S // tq),
        in_specs=[pl.BlockSpec((1, G, LANES), lambda b, g, i: (g, 0, 0)),
                  pl.BlockSpec((1, tq, G * LANES), lambda b, g, i: (b, i, 0)),
                  pl.BlockSpec((1, 1, n_cmp, LANES), per_bg),
                  pl.BlockSpec((1, 1, dh, n_cmp), per_bg),
                  pl.BlockSpec((1, 1, S, 2 * LANES), per_bg),
                  pl.BlockSpec((1, 1, S // tk, dh, tk), per_bg5),
                  pl.BlockSpec((1, 1, Sp, LANES), per_bg),
                  pl.BlockSpec((1, 1, Sp // LANES, dh, LANES), per_bg5),
                  pl.BlockSpec((1, tq, LANES), lambda b, g, i: (b, i, 0)),
                  pl.BlockSpec((n_slc, n_cmp), lambda b, g, i: (0, 0))],
        out_specs=pl.BlockSpec((1, tq, G * dh), lambda b, g, i: (b, i, g)),
        out_shape=jax.ShapeDtypeStruct((B, S, NSA_HEADS * dh), BF16),
        scratch_shapes=[pltpu.VMEM((G * tq, 2 * LANES), BF16),
                        pltpu.VMEM((G * dh, tq), F32)] + _flash_scratch(dh, G * tq),
        compiler_params=_params(("parallel", "parallel", "parallel")),
        name="nsa_attention",
    )(qx, nq, kca, vc_t, ska, sv_t, wka, wv_t, gates, overlap_t)


def _outproj_kernel(oa_ref, ob_ref, x_ref, wo_ref, gpm_ref, gta_ref, gpf_ref, sc_ref, sh_ref, wr_ref, br_ref,
                    x1_ref, h2_ref, rt_ref):
    half = oa_ref.shape[1]
    mix = (jnp.dot(oa_ref[...], wo_ref[:half], preferred_element_type=F32)
           + jnp.dot(ob_ref[...], wo_ref[half:], preferred_element_type=F32))
    x1 = x_ref[...] + gta_ref[0] * (_rms(mix) * gpm_ref[...])
    x1_ref[...] = x1
    h2 = _rms(x1) * gpf_ref[...] * (1.0 + sc_ref[0]) + sh_ref[0]
    h2_ref[...] = h2
    h_hi = h2.astype(BF16)
    h_lo = (h2 - h_hi.astype(F32)).astype(BF16)
    logits = (jnp.dot(h_hi, wr_ref[0], preferred_element_type=F32)
              + jnp.dot(h_lo, wr_ref[0], preferred_element_type=F32)
              + jnp.dot(h_hi, wr_ref[1], preferred_element_type=F32)) + br_ref[...]
    tm = logits.shape[0]
    lane = lax.broadcasted_iota(jnp.int32, (tm, LANES), 1)
    lanef = lane.astype(F32)
    is_grp = (lane >= N_EXPERTS) & (lane < N_EXPERTS + N_GROUPS)
    lg = jnp.where(is_grp, logits, -jnp.inf)
    mg = jnp.max(lg, axis=-1, keepdims=True)
    g_lane = jnp.min(jnp.where(lg == mg, lanef, 1e9), axis=-1, keepdims=True)
    p_grp = 1.0 / jnp.sum(jnp.where(is_grp, jnp.exp(lg - mg), 0.0), axis=-1, keepdims=True)
    grp = g_lane.astype(jnp.int32) - N_EXPERTS
    in_grp = (lane < N_EXPERTS) & ((lane // EXPERTS_PER_GROUP) == grp)
    le = jnp.where(in_grp, logits, -jnp.inf)
    v1 = jnp.max(le, axis=-1, keepdims=True)
    i1 = jnp.min(jnp.where(le == v1, lanef, 1e9), axis=-1, keepdims=True)
    le2 = jnp.where(lanef == i1, -jnp.inf, le)
    v2 = jnp.max(le2, axis=-1, keepdims=True)
    i2 = jnp.min(jnp.where(le2 == v2, lanef, 1e9), axis=-1, keepdims=True)
    e2 = jnp.exp(v2 - v1)
    w1 = p_grp / (1.0 + e2)
    w2 = p_grp * e2 / (1.0 + e2)
    rt_ref[...] = jnp.where(lane == 0, i1, jnp.where(lane == 1, i2, jnp.where(lane == 2, w1,
                            jnp.where(lane == 3, w2, 0.0))))


def _out_proj(o_a, o_b, x2, w_out, g_post_mix, gt_a, g_pre_ffn, sc_f, sh_f, w_r, b_r, B, S):
    T, D = x2.shape
    tm = ROW_TILE
    per_b = S // tm
    row = lambda i: (i, 0)
    bat = lambda i: (i // per_b, 0, 0)
    const = lambda i: (0, 0)
    return pl.pallas_call(
        _outproj_kernel,
        grid=(T // tm,),
        in_specs=[pl.BlockSpec((tm, o_a.shape[1]), row), pl.BlockSpec((tm, o_b.shape[1]), row),
                  pl.BlockSpec((tm, D), row), pl.BlockSpec((D, D), const),
                  pl.BlockSpec((1, D), const), pl.BlockSpec((1, 1, D), bat), pl.BlockSpec((1, D), const),
                  pl.BlockSpec((1, 1, D), bat), pl.BlockSpec((1, 1, D), bat),
                  pl.BlockSpec((2, D, LANES), lambda i: (0, 0, 0)), pl.BlockSpec((1, LANES), const)],
        out_specs=[pl.BlockSpec((tm, D), row), pl.BlockSpec((tm, D), row), pl.BlockSpec((tm, LANES), row)],
        out_shape=[jax.ShapeDtypeStruct((T, D), F32), jax.ShapeDtypeStruct((T, D), F32),
                   jax.ShapeDtypeStruct((T, LANES), F32)],
        compiler_params=_params(("parallel",)),
        name="out_proj_router",
    )(o_a, o_b, x2, w_out, g_post_mix.reshape(1, D), gt_a.reshape(B, 1, D), g_pre_ffn.reshape(1, D),
      sc_f.reshape(B, 1, D), sh_f.reshape(B, 1, D), w_r, b_r)


def _slots_kernel(rt_ref, slot_ref, blk_ref, cnt_ref, base_ref, *, n_blk_rows):
    phase = pl.program_id(0)
    step = pl.program_id(1)
    tm = rt_ref.shape[0]
    rt = rt_ref[...]
    lane = lax.broadcasted_iota(jnp.int32, (tm, LANES), 1).astype(F32)
    i1 = rt[:, 0:1]
    i2 = rt[:, 1:2]
    hit1 = lane == i1
    hit2 = lane == i2
    onehot = jnp.where(hit1, 1.0, jnp.where(hit2, 1.0, 0.0))
    colsum = jnp.sum(onehot, axis=0, keepdims=True)

    @pl.when((phase == 0) & (step == 0))
    def _():
        cnt_ref[...] = jnp.zeros(cnt_ref.shape, F32)

    @pl.when(phase == 0)
    def _():
        cnt_ref[...] += colsum

    @pl.when((phase == 1) & (step == 0))
    def _():
        cnt = cnt_ref[...]
        padded = jnp.floor((cnt + (MOE_ROWS - 1)) / MOE_ROWS) * MOE_ROWS
        lane1 = lax.broadcasted_iota(jnp.int32, (1, LANES), 1)
        incl = padded
        sh = 1
        while sh < LANES:
            incl = incl + jnp.where(lane1 >= sh, pltpu.roll(incl, sh, 1), 0.0)
            sh *= 2
        base_ref[...] = incl - padded
        rows = lax.broadcasted_iota(jnp.int32, (n_blk_rows, LANES), 0).astype(F32) * MOE_ROWS
        lane2 = lax.broadcasted_iota(jnp.int32, (n_blk_rows, LANES), 1)
        ended = jnp.where(lane2 < N_EXPERTS, jnp.where(incl <= rows, 1.0, 0.0), 0.0)
        blk_e = jnp.minimum(jnp.sum(ended, axis=-1, keepdims=True), N_EXPERTS - 1.0)
        total = jnp.sum(jnp.where(lane1 == N_EXPERTS - 1, incl, 0.0), axis=-1, keepdims=True)
        n_used = total / MOE_ROWS
        blk_ref[...] = jnp.where(lane2 == 0, blk_e, jnp.where(lane2 == 1, n_used, 0.0)).astype(jnp.int32)

    @pl.when(phase == 1)
    def _():
        r = lax.broadcasted_iota(jnp.int32, (tm, tm), 0)
        c = lax.broadcasted_iota(jnp.int32, (tm, tm), 1)
        tri = jnp.where(c < r, 1.0, 0.0).astype(BF16)
        before = jnp.dot(tri, onehot.astype(BF16), preferred_element_type=F32) + base_ref[...]
        s1 = jnp.sum(jnp.where(hit1, before, 0.0), axis=-1, keepdims=True)
        s2 = jnp.sum(jnp.where(hit2, before, 0.0), axis=-1, keepdims=True)
        lane_i = lax.broadcasted_iota(jnp.int32, (tm, LANES), 1)
        slot_ref[...] = jnp.where(lane_i == 0, s1, jnp.where(lane_i == 1, s2, 0.0)).astype(jnp.int32)
        base_ref[...] += colsum


def _slots(route, n_blocks):
    T = route.shape[0]
    tm = ROW_TILE
    n_blk_rows = -(-n_blocks // 8) * 8
    slot, blk = pl.pallas_call(
        functools.partial(_slots_kernel, n_blk_rows=n_blk_rows),
        grid=(2, T // tm),
        in_specs=[pl.BlockSpec((tm, LANES), lambda p, i: (i, 0))],
        out_specs=[pl.BlockSpec((tm, LANES), lambda p, i: (i * p, 0)),
                   pl.BlockSpec((n_blk_rows, LANES), lambda p, i: (0, 0))],
        out_shape=[jax.ShapeDtypeStruct((T, LANES), jnp.int32),
                   jax.ShapeDtypeStruct((n_blk_rows, LANES), jnp.int32)],
        scratch_shapes=[pltpu.VMEM((1, LANES), F32), pltpu.VMEM((1, LANES), F32)],
        compiler_params=_params(("arbitrary", "arbitrary")),
        name="moe_slots",
    )(route)
    return slot[:, 0], slot[:, 1], blk[:n_blocks, 0], blk[0:1, 1]


def _dispatch_kernel(s1_ref, s2_ref, h_ref, xz_ref, xb_ref, sem):
    del xz_ref
    tm = h_ref.shape[0]
    base = pl.program_id(0) * tm

    def issue(r, carry):
        src = h_ref.at[pl.ds(r, 1), :]
        pltpu.make_async_copy(src, xb_ref.at[pl.ds(s1_ref[base + r], 1), :], sem).start()
        pltpu.make_async_copy(src, xb_ref.at[pl.ds(s2_ref[base + r], 1), :], sem).start()
        return carry

    lax.fori_loop(0, tm, issue, 0, unroll=DMA_UNROLL)

    def drain(r, carry):
        for _ in range(TOP_K):
            pltpu.make_async_copy(h_ref.at[pl.ds(0, 1), :], xb_ref.at[pl.ds(0, 1), :], sem).wait()
        return carry

    lax.fori_loop(0, tm, drain, 0, unroll=DMA_UNROLL)


def _dispatch(h2, slot1, slot2, n_rows):
    T, D = h2.shape
    tm = ROW_TILE
    xz = jnp.zeros((n_rows, D), h2.dtype)
    return pl.pallas_call(
        _dispatch_kernel,
        grid_spec=pltpu.PrefetchScalarGridSpec(
            num_scalar_prefetch=2,
            grid=(T // tm,),
            in_specs=[pl.BlockSpec((tm, D), lambda i, a, b: (i, 0)),
                      pl.BlockSpec(memory_space=pl.ANY)],
            out_specs=pl.BlockSpec(memory_space=pl.ANY),
            scratch_shapes=[pltpu.SemaphoreType.DMA(())]),
        out_shape=jax.ShapeDtypeStruct((n_rows, D), h2.dtype),
        input_output_aliases={3: 0},
        compiler_params=_params(("arbitrary",)),
        name="moe_dispatch",
    )(slot1, slot2, h2, xz)


def _expert_kernel(be_ref, nu_ref, x_ref, wg_ref, wu_ref, wd_ref, y_ref, wgb_ref, wub_ref, wdb_ref):
    i = pl.program_id(0)

    @pl.when(i >= nu_ref[0])
    def _():
        y_ref[...] = jnp.zeros(y_ref.shape, y_ref.dtype)

    @pl.when(i < nu_ref[0])
    def _():
        changed = be_ref[i] != be_ref[jnp.maximum(i - 1, 0)]

        @pl.when((i == 0) | changed)
        def _():
            wgb_ref[...] = wg_ref[0].astype(BF16)
            wub_ref[...] = wu_ref[0].astype(BF16)
            wdb_ref[...] = wd_ref[0].astype(BF16)

        x = x_ref[...].astype(BF16)
        g = jnp.dot(x, wgb_ref[...], preferred_element_type=F32)
        u = jnp.dot(x, wub_ref[...], preferred_element_type=F32)
        hid = (g * jax.nn.sigmoid(g) * u).astype(BF16)
        y_ref[...] = jnp.dot(hid, wdb_ref[...], preferred_element_type=F32).astype(y_ref.dtype)


def _experts(xbuf, blk_e, n_used, w_gate, w_up, w_down):
    R, D = xbuf.shape
    E, _, De = w_gate.shape
    n_blocks = R // MOE_ROWS
    blk = lambda i, be, nu: (jnp.minimum(i, nu[0] - 1), 0)
    wsel = lambda i, be, nu: (be[jnp.minimum(i, nu[0] - 1)], 0, 0)
    return pl.pallas_call(
        _expert_kernel,
        grid_spec=pltpu.PrefetchScalarGridSpec(
            num_scalar_prefetch=2,
            grid=(n_blocks,),
            in_specs=[pl.BlockSpec((MOE_ROWS, D), blk),
                      pl.BlockSpec((1, D, De), wsel), pl.BlockSpec((1, D, De), wsel),
                      pl.BlockSpec((1, De, D), wsel)],
            out_specs=pl.BlockSpec((MOE_ROWS, D), lambda i, be, nu: (i, 0)),
            scratch_shapes=[pltpu.VMEM((D, De), BF16), pltpu.VMEM((D, De), BF16), pltpu.VMEM((De, D), BF16)]),
        out_shape=jax.ShapeDtypeStruct((R, D), F32),
        compiler_params=_params(("arbitrary",)),
        name="moe_experts",
    )(blk_e, n_used, xbuf, w_gate, w_up, w_down)


def _combine_kernel(s1_ref, s2_ref, yb_ref, rt_ref, x1_ref, gtf_ref, gpo_ref, o_ref, ya_ref, yc_ref, sem):
    tm = x1_ref.shape[0]
    base = pl.program_id(0) * tm

    def issue(r, carry):
        pltpu.make_async_copy(yb_ref.at[pl.ds(s1_ref[base + r], 1), :], ya_ref.at[pl.ds(r, 1), :], sem).start()
        pltpu.make_async_copy(yb_ref.at[pl.ds(s2_ref[base + r], 1), :], yc_ref.at[pl.ds(r, 1), :], sem).start()
        return carry

    lax.fori_loop(0, tm, issue, 0, unroll=DMA_UNROLL)

    def drain(r, carry):
        for _ in range(TOP_K):
            pltpu.make_async_copy(yb_ref.at[pl.ds(0, 1), :], ya_ref.at[pl.ds(0, 1), :], sem).wait()
        return carry

    lax.fori_loop(0, tm, drain, 0, unroll=DMA_UNROLL)
    rt = rt_ref[...]
    f = rt[:, 2:3] * ya_ref[...] + rt[:, 3:4] * yc_ref[...]
    o_ref[...] = x1_ref[...] + gtf_ref[0] * (_rms(f) * gpo_ref[...])


def _combine(ybuf, slot1, slot2, route, x1, gt_f, g_post_ffn, B, S):
    T, D = x1.shape
    tm = ROW_TILE
    per_b = S // tm
    row = lambda i, a, b: (i, 0)
    return pl.pallas_call(
        _combine_kernel,
        grid_spec=pltpu.PrefetchScalarGridSpec(
            num_scalar_prefetch=2,
            grid=(T // tm,),
            in_specs=[pl.BlockSpec(memory_space=pl.ANY),
                      pl.BlockSpec((tm, LANES), row), pl.BlockSpec((tm, D), row),
                      pl.BlockSpec((1, 1, D), lambda i, a, b: (i // per_b, 0, 0)),
                      pl.BlockSpec((1, D), lambda i, a, b: (0, 0))],
            out_specs=pl.BlockSpec((tm, D), row),
            scratch_shapes=[pltpu.VMEM((tm, D), F32), pltpu.VMEM((tm, D), F32), pltpu.SemaphoreType.DMA(())]),
        out_shape=jax.ShapeDtypeStruct((T, D), F32),
        compiler_params=_params(("arbitrary",)),
        name="moe_combine",
    )(slot1, slot2, ybuf, route, x1, gt_f.reshape(B, 1, D), g_post_ffn.reshape(1, D))


def _overlap_matrix(S):
    n_rows = S // CMP_STRIDE
    n_cmp = (S - CMP_BLOCK) // CMP_STRIDE + 1
    n_slc = S // SLC_BLOCK
    cstart = np.arange(n_rows) * CMP_STRIDE
    sstart = np.arange(n_slc) * SLC_BLOCK
    ov = (cstart[:, None] < sstart[None, :] + SLC_BLOCK) & (cstart[:, None] + CMP_BLOCK > sstart[None, :])
    ov &= (np.arange(n_rows) < n_cmp)[:, None]
    return jnp.asarray(ov.astype(np.float32), dtype=BF16)


def kernel(x, c, w_ada, b_ada, g_pre_mix, g_post_mix, g_pre_ffn, g_post_ffn, w_in, lam_q1, lam_k1, lam_q2,
           lam_k2, g_subln, cmp_pos_k, cmp_pos_v, cmp_w1_k, cmp_w1_v, cmp_w2_k, cmp_w2_v, w_out,
           w_router_grp, b_router_grp, w_router_exp, b_router_exp, w_exp_gate, w_exp_up, w_exp_down):
    B, S, D = x.shape
    T = B * S
    depth = w_ada.shape[0]
    assert S // SLC_BLOCK == LANES and D == D_MODEL
    n_blocks = (T * TOP_K) // MOE_ROWS + N_EXPERTS
    for l in range(depth):
        lambda_init = 0.8 - 0.6 * math.exp(-0.3 * l)
        mod = _adaln(c, w_ada[l], b_ada[l])
        sh_a, sc_a, gt_a, sh_f, sc_f, gt_f = jnp.split(mod, 6, axis=-1)

        dq, dk, dv_t, nq, ckv, ska, sv_t, wka, wv_t, gates = _in_proj(x, g_pre_mix[l], sc_a, sh_a,
                                                                      _build_w_proj(w_in[l]))
        kca, vc_t = _compress(ckv, cmp_pos_k[l], cmp_pos_v[l], cmp_w1_k[l], cmp_w1_v[l], cmp_w2_k[l], cmp_w2_v[l])
        o_a = _diff_attention(dq, dk, dv_t, (lam_q1[l], lam_k1[l], lam_q2[l], lam_k2[l]), g_subln[l], lambda_init)
        o_b = _nsa_attention(nq, kca, vc_t, ska, sv_t, wka, wv_t, gates)

        w_r = jnp.zeros((D, LANES), F32).at[:, :N_EXPERTS].set(w_router_exp[l]).at[
            :, N_EXPERTS:N_EXPERTS + N_GROUPS].set(w_router_grp[l])
        b_r = jnp.zeros((1, LANES), F32).at[0, :N_EXPERTS].set(b_router_exp[l]).at[
            0, N_EXPERTS:N_EXPERTS + N_GROUPS].set(b_router_grp[l])
        w_r_hi = w_r.astype(BF16)
        w_r = jnp.stack([w_r_hi, (w_r - w_r_hi.astype(F32)).astype(BF16)])
        x1, h2, route = _out_proj(o_a.reshape(T, -1), o_b.reshape(T, -1), x.reshape(T, D), w_out[l].astype(BF16),
                                  g_post_mix[l], gt_a, g_pre_ffn[l], sc_f, sh_f, w_r, b_r, B, S)
        slot1, slot2, blk_e, n_used = _slots(route, n_blocks)
        xbuf = _dispatch(h2, slot1, slot2, n_blocks * MOE_ROWS)
        ybuf = _experts(xbuf, blk_e, n_used, w_exp_gate[l], w_exp_up[l], w_exp_down[l])
        x = _combine(ybuf, slot1, slot2, route, x1, gt_f, g_post_ffn[l], B, S).reshape(B, S, D)
    return x
```

```python
import functools
import math

import numpy as np
import jax
import jax.numpy as jnp
from jax import lax
from jax.experimental import pallas as pl
from jax.experimental.pallas import tpu as pltpu

F32 = jnp.float32
BF16 = jnp.bfloat16
HIGHEST = lax.Precision.HIGHEST

D_MODEL = 1024
HEAD_DIM = 64
LANES = 128
DIFF_HEADS = 4
NSA_HEADS = 8
NSA_KV_HEADS = 2
NSA_GROUP = NSA_HEADS // NSA_KV_HEADS
CMP_BLOCK = 32
CMP_STRIDE = 16
CMP_HIDDEN = 2 * HEAD_DIM
SLC_BLOCK = 64
SLC_TOP = 16
WINDOW = 512
N_GROUPS = 8
EXPERTS_PER_GROUP = 8
N_EXPERTS = N_GROUPS * EXPERTS_PER_GROUP
TOP_K = 2
D_EXPERT = D_MODEL // 2
NEG = -1e30
FORCE_BONUS = 1e4
EPS = 1e-6

ROW_TILE = 512
KEY_TILE = 512
DIFF_TQ = 512
NSA_TQ = 256
SOFTMAX_CHUNK_VREGS = 32
MOE_ROWS = 256
DMA_UNROLL = 8
VMEM_LIMIT = 48 * 1024 * 1024
assert ROW_TILE == KEY_TILE == WINDOW

_OFF_DQ, _OFF_DK, _OFF_DV, _OFF_NQ = 0, 512, 1024, 1536
_OFF_KV = 2048
_OFF_NG = 2816
_N_GATES = 3 * NSA_HEADS
_W_PROJ = 2048 + 768 + LANES


def _alibi_slopes(n):
    return np.array([2.0 ** (-8.0 * (i + 1) / n) for i in range(n)], dtype=np.float32)


def _dot_nt(a, b):
    return lax.dot_general(a, b, (((1,), (1,)), ((), ())), preferred_element_type=F32)


def _rms(v):
    return v * lax.rsqrt(jnp.mean(v * v, axis=-1, keepdims=True) + EPS)


def _params(sem, vmem=VMEM_LIMIT):
    return pltpu.CompilerParams(dimension_semantics=sem, vmem_limit_bytes=vmem)


LOG2E = np.float32(1.0 / math.log(2.0))
N_SLOPE_PIECES = 3
MASK_LANE = 2 * N_SLOPE_PIECES


def _pos_lanes(pos, lane):
    half = lane % HEAD_DIM
    return jnp.where(half < MASK_LANE, jnp.where(half % 2 == 0, pos // LANES, pos % LANES), 0).astype(F32)


def _slope_cols(slopes, width, at):
    cols = np.zeros((len(slopes), width), np.float32)
    rest = (np.asarray(slopes, np.float32) * LOG2E).astype(np.float32)
    for i in range(N_SLOPE_PIECES):
        piece = rest.astype(BF16).astype(np.float32)
        cols[:, at + 2 * i] = piece * LANES
        cols[:, at + 2 * i + 1] = piece
        rest = rest - piece
    cols[:, at + MASK_LANE] = NEG
    return cols


def _adaln_kernel(c_ref, w_ref, b_ref, o_ref):
    c = c_ref[...]
    s = c * jax.nn.sigmoid(c)
    o_ref[...] = jnp.dot(s, w_ref[...], preferred_element_type=F32, precision=HIGHEST) + b_ref[...]


def _adaln(c, w_ada, b_ada):
    B, D = c.shape
    rows = 8
    cp = jnp.zeros((rows, D), F32).at[:B].set(c)
    n_out = w_ada.shape[1]
    out = pl.pallas_call(
        _adaln_kernel,
        grid=(n_out // D,),
        in_specs=[pl.BlockSpec((rows, D), lambda j: (0, 0)),
                  pl.BlockSpec((D, D), lambda j: (0, j)),
                  pl.BlockSpec((1, D), lambda j: (0, j))],
        out_specs=pl.BlockSpec((rows, D), lambda j: (0, j)),
        out_shape=jax.ShapeDtypeStruct((rows, n_out), F32),
        compiler_params=_params(("parallel",)),
        name="adaln",
    )(cp, w_ada, b_ada.reshape(1, n_out))
    return out[:B]


def _inproj_kernel(x_ref, g_ref, sc_ref, sh_ref, w_ref,
                   dq_ref, dk_ref, dvt_ref, nq_ref, ckv_ref, ska_ref, svt_ref, wka_ref, wvt_ref, gt_ref):
    step = pl.program_id(1)
    tm = x_ref.shape[1]
    dh = HEAD_DIM

    @pl.when(step == 0)
    def _():
        lane = lax.broadcasted_iota(jnp.int32, (tm, LANES), 1)
        for g in range(NSA_KV_HEADS):
            wka_ref[0, g] = jnp.where(lane == (1 - g) * dh + MASK_LANE, 1.0, 0.0).astype(BF16)
        wvt_ref[...] = jnp.zeros(wvt_ref.shape, wvt_ref.dtype)

    @pl.when(step > 0)
    def _():
        h = (_rms(x_ref[0]) * g_ref[...] * (1.0 + sc_ref[0]) + sh_ref[0]).astype(BF16)

        def proj(off, width):
            return jnp.dot(h, w_ref[:, off:off + width], preferred_element_type=F32)

        lane = lax.broadcasted_iota(jnp.int32, (tm, LANES), 1)
        pos = (step - 1) * tm + lax.broadcasted_iota(jnp.int32, (tm, 1), 0)
        low = lane < dh
        pcol = _pos_lanes(pos, lane)
        wcol = _pos_lanes(pos + WINDOW, lane)

        dq_ref[0] = proj(_OFF_DQ, 512).astype(BF16)
        k = proj(_OFF_DK, 512)
        v = proj(_OFF_DV, 512)
        for hh in range(DIFF_HEADS):
            dk_ref[0, :, hh * 256:hh * 256 + LANES] = k[:, hh * LANES:(hh + 1) * LANES].astype(BF16)
            dk_ref[0, :, hh * 256 + LANES:(hh + 1) * 256] = pcol.astype(BF16)
            dvt_ref[0, hh, 0] = v[:, hh * LANES:(hh + 1) * LANES].T.astype(BF16)
        nq_ref[0] = proj(_OFF_NQ, 512).astype(BF16)
        kv = proj(_OFF_KV, 768)
        ckv_ref[0] = kv[:, :2 * LANES].astype(BF16)
        sk, sv, wk, wv = (kv[:, i * LANES:(i + 1) * LANES] for i in range(2, 6))
        onehot = jnp.where(lane == pos // SLC_BLOCK, 1.0, 0.0).astype(BF16)
        sv_t = sv.T
        wv_t = wv.T
        for g, (keys_sel, keys_win) in enumerate(((jnp.where(low, sk, pcol), jnp.where(low, wk, wcol)),
                                                  (jnp.where(low, pcol, sk), jnp.where(low, wcol, wk)))):
            ska_ref[0, g, :, :LANES] = keys_sel.astype(BF16)
            ska_ref[0, g, :, LANES:] = onehot
            svt_ref[0, g, 0] = sv_t[g * dh:(g + 1) * dh].astype(BF16)
            wka_ref[0, g] = keys_win.astype(BF16)
            for c in range(tm // LANES):
                wvt_ref[0, g, c] = wv_t[g * dh:(g + 1) * dh, c * LANES:(c + 1) * LANES].astype(BF16)
        gt_ref[0] = jax.nn.sigmoid(proj(_OFF_NG, LANES))


def _build_w_proj(w_in):
    scale = LOG2E * HEAD_DIM ** -0.5
    nq = w_in[:, _OFF_NQ:_OFF_NQ + 512].reshape(D_MODEL, NSA_KV_HEADS, NSA_GROUP, HEAD_DIM)
    nq = nq.transpose(0, 2, 1, 3).reshape(D_MODEL, 512) * scale
    cols = [w_in[:, _OFF_DQ:_OFF_DQ + 512] * scale, w_in[:, _OFF_DK:_OFF_NQ], nq,
            w_in[:, _OFF_KV:_OFF_NG + _N_GATES], jnp.zeros((D_MODEL, LANES - _N_GATES), F32)]
    return jnp.concatenate(cols, axis=1).astype(BF16)


def _in_proj(x, g_pre, sc, sh, w_proj):
    B, S, D = x.shape
    tm = ROW_TILE
    n = S // tm
    KV, dh = NSA_KV_HEADS, HEAD_DIM
    tile = lambda b, s: (b, jnp.maximum(s - 1, 0), 0)
    out_shapes = [
        ((B, S, 512), (1, tm, 512), tile, BF16),
        ((B, S, DIFF_HEADS * 256), (1, tm, DIFF_HEADS * 256), tile, BF16),
        ((B, DIFF_HEADS, n, LANES, tm), (1, DIFF_HEADS, 1, LANES, tm),
         lambda b, s: (b, 0, jnp.maximum(s - 1, 0), 0, 0), BF16),
        ((B, S, 512), (1, tm, 512), tile, BF16),
        ((B, S, 2 * LANES), (1, tm, 2 * LANES), tile, BF16),
        ((B, KV, S, 2 * LANES), (1, KV, tm, 2 * LANES),
         lambda b, s: (b, 0, jnp.maximum(s - 1, 0), 0), BF16),
        ((B, KV, n, dh, tm), (1, KV, 1, dh, tm),
         lambda b, s: (b, 0, jnp.maximum(s - 1, 0), 0, 0), BF16),
        ((B, KV, S + WINDOW, LANES), (1, KV, tm, LANES), lambda b, s: (b, 0, s, 0), BF16),
        ((B, KV, (S + WINDOW) // LANES, dh, LANES), (1, KV, tm // LANES, dh, LANES),
         lambda b, s: (b, 0, s, 0, 0), BF16),
        ((B, S, LANES), (1, tm, LANES), tile, F32),
    ]
    return pl.pallas_call(
        _inproj_kernel,
        grid=(B, n + 1),
        in_specs=[pl.BlockSpec((1, tm, D), tile),
                  pl.BlockSpec((1, D), lambda b, s: (0, 0)),
                  pl.BlockSpec((1, 1, D), lambda b, s: (b, 0, 0)),
                  pl.BlockSpec((1, 1, D), lambda b, s: (b, 0, 0)),
                  pl.BlockSpec((D, _W_PROJ), lambda b, s: (0, 0))],
        out_specs=[pl.BlockSpec(blk, idx) for _, blk, idx, _ in out_shapes],
        out_shape=[jax.ShapeDtypeStruct(shape, dt) for shape, _, _, dt in out_shapes],
        compiler_params=_params(("parallel", "arbitrary")),
        name="in_proj",
    )(x, g_pre.reshape(1, D), sc.reshape(B, 1, D), sh.reshape(B, 1, D), w_proj)


def _compress_kernel(r_ref, pos_ref, w1_ref, w2_ref, kca_ref, vct_ref):
    half = CMP_STRIDE * HEAD_DIM
    dh = HEAD_DIM

    def mlp(t, h):
        r = r_ref[t, 0, h].astype(F32)
        pos = pos_ref[t]
        a = jnp.dot((r + pos[:, :half]).astype(BF16), w1_ref[t, :half], preferred_element_type=F32)
        b = jnp.dot((r + pos[:, half:]).astype(BF16), w1_ref[t, half:], preferred_element_type=F32)
        hid = a + pltpu.roll(b, a.shape[0] - 1, 0)
        return jnp.dot(jax.nn.gelu(hid).astype(BF16), w2_ref[t], preferred_element_type=F32)

    n_rows = r_ref.shape[3]
    lane = lax.broadcasted_iota(jnp.int32, (n_rows, LANES), 1)
    cend = lax.broadcasted_iota(jnp.int32, (n_rows, 1), 0) * CMP_STRIDE + (CMP_BLOCK - 1)
    pcol = _pos_lanes(cend, lane)
    kc = jnp.concatenate([mlp(0, 0), mlp(0, 1)], axis=-1)
    low = lane < dh
    kca_ref[0, 0] = jnp.where(low, kc, pcol).astype(BF16)
    kca_ref[0, 1] = jnp.where(low, pcol, kc).astype(BF16)
    vc_t = jnp.concatenate([mlp(1, 0), mlp(1, 1)], axis=-1).T
    for h in range(NSA_KV_HEADS):
        vct_ref[0, h] = vc_t[h * dh:(h + 1) * dh].astype(BF16)


def _compress(ckv, pos_k, pos_v, w1_k, w1_v, w2_k, w2_v):
    B, S, _ = ckv.shape
    KV, dh = NSA_KV_HEADS, HEAD_DIM
    n_rows = S // CMP_STRIDE
    wide = CMP_STRIDE * dh
    r = ckv.reshape(B, n_rows, CMP_STRIDE, 2, KV, dh).transpose(3, 0, 4, 1, 2, 5).reshape(2, B, KV, n_rows, wide)
    pos = jnp.stack([pos_k.reshape(1, -1), pos_v.reshape(1, -1)]).astype(F32)
    w1 = jnp.stack([w1_k, w1_v]).astype(BF16)
    w2 = jnp.stack([w2_k, w2_v]).astype(BF16)
    return pl.pallas_call(
        _compress_kernel,
        grid=(B,),
        in_specs=[pl.BlockSpec((2, 1, KV, n_rows, wide), lambda b: (0, b, 0, 0, 0)),
                  pl.BlockSpec((2, 1, 2 * wide), lambda b: (0, 0, 0)),
                  pl.BlockSpec((2, 2 * wide, CMP_HIDDEN), lambda b: (0, 0, 0)),
                  pl.BlockSpec((2, CMP_HIDDEN, dh), lambda b: (0, 0, 0))],
        out_specs=[pl.BlockSpec((1, KV, n_rows, LANES), lambda b: (b, 0, 0, 0)),
                   pl.BlockSpec((1, KV, dh, n_rows), lambda b: (b, 0, 0, 0))],
        out_shape=[jax.ShapeDtypeStruct((B, KV, n_rows, LANES), BF16),
                   jax.ShapeDtypeStruct((B, KV, dh, n_rows), BF16)],
        compiler_params=_params(("parallel",)),
        name="nsa_compress",
    )(r, pos, w1, w2)


def _flash_scratch(dv, n_cols):
    tk = KEY_TILE
    return [pltpu.VMEM((tk, n_cols), F32), pltpu.VMEM((tk, n_cols), F32),
            pltpu.VMEM((tk, n_cols), BF16), pltpu.VMEM((tk, n_cols), BF16),
            pltpu.VMEM((1, n_cols), F32), pltpu.VMEM((1, n_cols), F32),
            pltpu.VMEM((1, n_cols), F32), pltpu.VMEM((1, n_cols), F32), pltpu.VMEM((dv, n_cols), F32)]


def _flash_causal(q0, tq, qa_ref, k_tile, v_tile, scratch):
    s_a, s_b, p_a, p_b, a_a, a_b, m_ref, l_ref, acc_ref = scratch
    tk = KEY_TILE
    n_cols = qa_ref.shape[0]
    assert tk % tq == 0
    n_full = q0 // tk
    n = n_full + 1
    m_ref[...] = jnp.full(m_ref.shape, NEG, F32)
    l_ref[...] = jnp.zeros(l_ref.shape, F32)
    acc_ref[...] = jnp.zeros(acc_ref.shape, F32)
    p_b[...] = jnp.zeros(p_b.shape, BF16)
    a_b[...] = jnp.ones(a_b.shape, F32)

    def key_tile(t):
        t = jnp.clip(t, 0, n - 1)
        j = jnp.where(t == 0, n_full, t - 1)
        return j, pl.multiple_of(j * tk, tk)

    def scores(t, s_ref, masked):
        j, k0 = key_tile(t)
        s = _dot_nt(k_tile(j, k0), qa_ref[...])
        if masked:
            krow = lax.broadcasted_iota(jnp.int32, (tk, 1), 0)
            qcol = lax.broadcasted_iota(jnp.int32, (1, n_cols), 1) % tq
            s = jnp.where((krow + (k0 - q0)) <= qcol, s, NEG)
        s_ref[...] = s

    def softmax(s_ref, p_ref, a_ref):
        ch = SOFTMAX_CHUNK_VREGS * 8 * LANES // n_cols
        chunks = [slice(c, c + ch) for c in range(0, tk, ch)]
        mx = None
        for rows in chunks:
            part = jnp.max(s_ref[rows, :].reshape(ch // 8, 8, n_cols), axis=0)
            mx = part if mx is None else jnp.maximum(mx, part)
        m_old = m_ref[...]
        m_new = jnp.maximum(m_old, jnp.max(mx, axis=0, keepdims=True))
        sm = None
        for rows in chunks:
            p = jnp.exp2(s_ref[rows, :] - m_new)
            p_ref[rows, :] = p.astype(BF16)
            part = jnp.sum(p.reshape(ch // 8, 8, n_cols), axis=0)
            sm = part if sm is None else sm + part
        alpha = jnp.exp2(m_old - m_new)
        a_ref[...] = alpha
        l_ref[...] = alpha * l_ref[...] + jnp.sum(sm, axis=0, keepdims=True)
        m_ref[...] = m_new

    def values(t, p_ref, a_ref):
        j, k0 = key_tile(t)
        acc_ref[...] = a_ref[...] * acc_ref[...] + jnp.dot(v_tile(j, k0), p_ref[...], preferred_element_type=F32)

    scores(0, s_a, True)

    def pair(u, carry):
        i = 2 * u
        scores(i + 1, s_b, False)
        softmax(s_a, p_a, a_a)
        values(i - 1, p_b, a_b)

        @pl.when(i + 1 < n)
        def _():
            scores(i + 2, s_a, False)
            softmax(s_b, p_b, a_b)
            values(i, p_a, a_a)

        return carry

    lax.fori_loop(0, (n + 1) // 2, pair, 0)

    @pl.when(n % 2 == 1)
    def _():
        values(n - 1, p_a, a_a)

    @pl.when(n % 2 == 0)
    def _():
        values(n - 1, p_b, a_b)

    return acc_ref, l_ref


def _softmax_cols(s, bias):
    s = s + bias
    m = jnp.max(s, axis=0, keepdims=True)
    e = jnp.exp2(s - m)
    inv = jnp.where(m > 0.5 * NEG, 1.0 / jnp.sum(e, axis=0, keepdims=True), 0.0)
    return e * inv


def _diff_kernel(qx_ref, lam_ref, gsub_ref, q_ref, k_ref, v_ref, o_ref, qa_ref, *scratch, lambda_init):
    tq = DIFF_TQ
    q0 = pl.program_id(2) * tq
    q = q_ref[0].astype(F32)
    low = lax.broadcasted_iota(jnp.int32, (tq, LANES), 1) < HEAD_DIM
    ext = jnp.broadcast_to(qx_ref[0], (tq, LANES)).astype(BF16)
    qa_ref[0:tq, :LANES] = jnp.where(low, q, 0.0).astype(BF16)
    qa_ref[tq:, :LANES] = jnp.where(low, 0.0, q).astype(BF16)
    qa_ref[0:tq, LANES:] = ext
    qa_ref[tq:, LANES:] = ext

    acc_ref, l_ref = _flash_causal(q0, tq, qa_ref,
                                   lambda j, k0: k_ref[0, pl.ds(k0, KEY_TILE), :],
                                   lambda j, k0: v_ref[0, 0, j],
                                   scratch)

    lv = lam_ref[...]
    lam = (jnp.exp(jnp.sum(lv[0:1] * lv[1:2], axis=-1, keepdims=True))
           - jnp.exp(jnp.sum(lv[2:3] * lv[3:4], axis=-1, keepdims=True)) + lambda_init)
    o_t = acc_ref[...] / l_ref[...]
    o_t = o_t[:, :tq] - lam * o_t[:, tq:]
    o_t = o_t * lax.rsqrt(jnp.mean(o_t * o_t, axis=0, keepdims=True) + EPS)
    o_ref[0] = (o_t.T * gsub_ref[...] * (1.0 - lambda_init)).astype(o_ref.dtype)


def _diff_attention(dq, dk, dv_t, lam_vecs, g_subln, lambda_init):
    B, S, _ = dq.shape
    tq, tk = DIFF_TQ, KEY_TILE
    H = DIFF_HEADS
    qx = jnp.asarray(_slope_cols(_alibi_slopes(H), LANES, 0)).reshape(H, 1, LANES)
    lam = jnp.zeros((8, LANES), F32).at[:4, :HEAD_DIM].set(jnp.stack(lam_vecs))
    return pl.pallas_call(
        functools.partial(_diff_kernel, lambda_init=lambda_init),
        grid=(B, H, S // tq),
        in_specs=[pl.BlockSpec((1, 1, LANES), lambda b, h, i: (h, 0, 0)),
                  pl.BlockSpec((8, LANES), lambda b, h, i: (0, 0)),
                  pl.BlockSpec((1, LANES), lambda b, h, i: (0, 0)),
                  pl.BlockSpec((1, tq, LANES), lambda b, h, i: (b, i, h)),
                  pl.BlockSpec((1, S, 2 * LANES), lambda b, h, i: (b, 0, h)),
                  pl.BlockSpec((1, 1, S // tk, LANES, tk), lambda b, h, i: (b, h, 0, 0, 0))],
        out_specs=pl.BlockSpec((1, tq, LANES), lambda b, h, i: (b, i, h)),
        out_shape=jax.ShapeDtypeStruct((B, S, H * LANES), BF16),
        scratch_shapes=[pltpu.VMEM((2 * tq, 2 * LANES), BF16)] + _flash_scratch(LANES, 2 * tq),
        compiler_params=_params(("parallel", "parallel", "parallel")),
        name="diff_attention",
    )(qx, lam, g_subln.reshape(1, LANES).astype(F32), dq, dk, dv_t)


def _nsa_kernel(qx_ref, q_ref, kc_ref, vc_ref, ka_ref, sv_ref, wk_ref, wv_ref, g_ref, ov_ref, cb_ref, wb_ref,
                o_ref, qa_ref, out_ref, *scratch):
    tq, G, dh = NSA_TQ, NSA_GROUP, HEAD_DIM
    n_cmp = kc_ref.shape[2]
    n_win = WINDOW + tq
    grp = pl.program_id(1)
    qi = pl.program_id(2)
    q0 = pl.multiple_of(qi * tq, tq)
    cols = [slice(i * tq, (i + 1) * tq) for i in range(G)]
    g_t = g_ref[0].T

    def gate(i, branch):
        a = 3 * i + branch
        b = 3 * (G + i) + branch
        return jnp.where(grp == 0, g_t[a:a + 1, :], g_t[b:b + 1, :])

    lane_q = lax.broadcasted_iota(jnp.int32, (tq, LANES), 1)
    mine = (lane_q >= grp * dh) & (lane_q < (grp + 1) * dh)
    qx = qx_ref[0]
    for i in range(G):
        q = q_ref[0, :, i * LANES:(i + 1) * LANES].astype(F32)
        qa_ref[cols[i], :LANES] = jnp.where(mine, q, jnp.broadcast_to(qx[i:i + 1], (tq, LANES))).astype(BF16)
    qs = qa_ref[:, :LANES]

    kw = wk_ref[0, 0, pl.ds(q0, n_win), :]
    vw = jnp.concatenate([wv_ref[0, 0, (tq // LANES) * qi + c] for c in range(n_win // LANES)], axis=1)
    p = _softmax_cols(_dot_nt(kw, qs), jnp.tile(wb_ref[...], (1, G)))
    ow = jnp.dot(vw, p.astype(BF16), preferred_element_type=F32)

    cbias = cb_ref[pl.ds(pl.multiple_of(n_cmp - (tq // CMP_STRIDE) * qi, 8), n_cmp), :]
    p = _softmax_cols(_dot_nt(kc_ref[0, 0], qs), jnp.tile(cbias, (1, G)))
    oc = jnp.dot(vc_ref[0, 0], p.astype(BF16), preferred_element_type=F32)
    for i in range(G):
        out_ref[i * dh:(i + 1) * dh, :] = gate(i, 0) * oc[:, cols[i]] + gate(i, 2) * ow[:, cols[i]]

    psum = p[:, cols[0]]
    for i in range(1, G):
        psum = psum + p[:, cols[i]]
    hi = psum.astype(BF16)
    r1 = psum - hi.astype(F32)
    mid = r1.astype(BF16)
    lo = (r1 - mid.astype(F32)).astype(BF16)
    ov = ov_ref[...]
    imp = (jnp.dot(ov, hi, preferred_element_type=F32) + jnp.dot(ov, mid, preferred_element_type=F32)
           + jnp.dot(ov, lo, preferred_element_type=F32))
    n_slc = imp.shape[0]
    blk = lax.broadcasted_iota(jnp.int32, (n_slc, tq), 0)
    cur = (lax.broadcasted_iota(jnp.int32, (1, tq), 1) + q0) // SLC_BLOCK
    forced = (blk == 0) | (blk == cur) | (blk == cur - 1)
    score = jnp.where(forced, -jnp.inf, jnp.where(blk <= cur, imp, NEG))
    blkf = blk.astype(F32)
    sel = jnp.where(forced, 1.0, 0.0)
    for _ in range(SLC_TOP - 3):
        mx = jnp.max(score, axis=0, keepdims=True)
        idx = jnp.min(jnp.where(score == mx, blkf, float(n_slc)), axis=0, keepdims=True)
        hit = blkf == idx
        sel = jnp.where(hit, 1.0, sel)
        score = jnp.where(hit, -jnp.inf, score)
    blockmask = ((sel.T - 1.0) * 1e30).astype(BF16)
    for i in range(G):
        qa_ref[cols[i], LANES:] = blockmask

    acc_ref, l_ref = _flash_causal(q0, tq, qa_ref,
                                   lambda j, k0: ka_ref[0, 0, pl.ds(k0, KEY_TILE), :],
                                   lambda j, k0: sv_ref[0, 0, j],
                                   scratch)
    os_ = acc_ref[...] / l_ref[...]
    for i in range(G):
        out_ref[i * dh:(i + 1) * dh, :] += gate(i, 1) * os_[:, cols[i]]

    o_ref[0] = out_ref[...].T.astype(o_ref.dtype)


def _nsa_attention(nq, kca, vc_t, ska, sv_t, wka, wv_t, gates):
    B, S, _ = nq.shape
    tq, tk, G, dh = NSA_TQ, KEY_TILE, NSA_GROUP, HEAD_DIM
    KV = NSA_KV_HEADS
    n_cmp = kca.shape[2]
    n_slc = S // SLC_BLOCK
    Sp = S + WINDOW
    slopes = _alibi_slopes(NSA_HEADS)
    qx = jnp.asarray(np.stack([_slope_cols(slopes[:G], LANES, dh), _slope_cols(slopes[G:], LANES, 0)]))
    overlap_t = _overlap_matrix(S).T
    r = np.arange(tq)[None, :]
    j = np.arange(2 * n_cmp)[:, None]
    cmp_band = jnp.asarray(np.where(CMP_STRIDE * (j - n_cmp) + CMP_BLOCK - 1 <= r, 0.0, NEG).astype(np.float32))
    c = np.arange(WINDOW + tq)[:, None]
    d = r - c + WINDOW
    win_band = jnp.asarray(np.where((d >= 0) & (d < WINDOW), 0.0, NEG).astype(np.float32))
    const2 = lambda b, g, i: (0, 0)
    per_bg = lambda b, g, i: (b, g, 0, 0)
    per_bg5 = lambda b, g, i: (b, g, 0, 0, 0)
    return pl.pallas_call(
        _nsa_kernel,
        grid=(B, KV, S // tq),
        in_specs=[pl.BlockSpec((1, G, LANES), lambda b, g, i: (g, 0, 0)),
                  pl.BlockSpec((1, tq, G * LANES), lambda b, g, i: (b, i, 0)),
                  pl.BlockSpec((1, 1, n_cmp, LANES), per_bg),
                  pl.BlockSpec((1, 1, dh, n_cmp), per_bg),
                  pl.BlockSpec((1, 1, S, 2 * LANES), per_bg),
                  pl.BlockSpec((1, 1, S // tk, dh, tk), per_bg5),
                  pl.BlockSpec((1, 1, Sp, LANES), per_bg),
                  pl.BlockSpec((1, 1, Sp // LANES, dh, LANES), per_bg5),
                  pl.BlockSpec((1, tq, LANES), lambda b, g, i: (b, i, 0)),
                  pl.BlockSpec((n_slc, n_cmp), const2),
                  pl.BlockSpec(cmp_band.shape, const2), pl.BlockSpec(win_band.shape, const2)],
        out_specs=pl.BlockSpec((1, tq, G * dh), lambda b, g, i: (b, i, g)),
        out_shape=jax.ShapeDtypeStruct((B, S, NSA_HEADS * dh), BF16),
        scratch_shapes=[pltpu.VMEM((G * tq, 2 * LANES), BF16),
                        pltpu.VMEM((G * dh, tq), F32)] + _flash_scratch(dh, G * tq),
        compiler_params=_params(("parallel", "parallel", "parallel")),
        name="nsa_attention",
    )(qx, nq, kca, vc_t, ska, sv_t, wka, wv_t, gates, overlap_t, cmp_band, win_band)


def _outproj_kernel(oa_ref, ob_ref, x_ref, wo_ref, gpm_ref, gta_ref, gpf_ref, sc_ref, sh_ref, wr_ref, br_ref,
                    x1_ref, h2_ref, rt_ref):
    half = oa_ref.shape[1]
    mix = (jnp.dot(oa_ref[...], wo_ref[:half], preferred_element_type=F32)
           + jnp.dot(ob_ref[...], wo_ref[half:], preferred_element_type=F32))
    x1 = x_ref[...] + gta_ref[0] * (_rms(mix) * gpm_ref[...])
    x1_ref[...] = x1
    h2 = _rms(x1) * gpf_ref[...] * (1.0 + sc_ref[0]) + sh_ref[0]
    h2_ref[...] = h2
    h_hi = h2.astype(BF16)
    h_lo = (h2 - h_hi.astype(F32)).astype(BF16)
    logits = (jnp.dot(h_hi, wr_ref[0], preferred_element_type=F32)
              + jnp.dot(h_lo, wr_ref[0], preferred_element_type=F32)
              + jnp.dot(h_hi, wr_ref[1], preferred_element_type=F32)) + br_ref[...]
    tm = logits.shape[0]
    lane = lax.broadcasted_iota(jnp.int32, (tm, LANES), 1)
    lanef = lane.astype(F32)
    is_grp = (lane >= N_EXPERTS) & (lane < N_EXPERTS + N_GROUPS)
    lg = jnp.where(is_grp, logits, -jnp.inf)
    mg = jnp.max(lg, axis=-1, keepdims=True)
    g_lane = jnp.min(jnp.where(lg == mg, lanef, 1e9), axis=-1, keepdims=True)
    p_grp = 1.0 / jnp.sum(jnp.where(is_grp, jnp.exp(lg - mg), 0.0), axis=-1, keepdims=True)
    grp = g_lane.astype(jnp.int32) - N_EXPERTS
    in_grp = (lane < N_EXPERTS) & ((lane // EXPERTS_PER_GROUP) == grp)
    le = jnp.where(in_grp, logits, -jnp.inf)
    v1 = jnp.max(le, axis=-1, keepdims=True)
    i1 = jnp.min(jnp.where(le == v1, lanef, 1e9), axis=-1, keepdims=True)
    le2 = jnp.where(lanef == i1, -jnp.inf, le)
    v2 = jnp.max(le2, axis=-1, keepdims=True)
    i2 = jnp.min(jnp.where(le2 == v2, lanef, 1e9), axis=-1, keepdims=True)
    e2 = jnp.exp(v2 - v1)
    w1 = p_grp / (1.0 + e2)
    w2 = p_grp * e2 / (1.0 + e2)
    rt_ref[...] = jnp.where(lane == 0, i1, jnp.where(lane == 1, i2, jnp.where(lane == 2, w1,
                            jnp.where(lane == 3, w2, 0.0))))


def _out_proj(o_a, o_b, x2, w_out, g_post_mix, gt_a, g_pre_ffn, sc_f, sh_f, w_r, b_r, B, S):
    T, D = x2.shape
    tm = ROW_TILE
    per_b = S // tm
    row = lambda i: (i, 0)
    bat = lambda i: (i // per_b, 0, 0)
    const = lambda i: (0, 0)
    return pl.pallas_call(
        _outproj_kernel,
        grid=(T // tm,),
        in_specs=[pl.BlockSpec((tm, o_a.shape[1]), row), pl.BlockSpec((tm, o_b.shape[1]), row),
                  pl.BlockSpec((tm, D), row), pl.BlockSpec((D, D), const),
                  pl.BlockSpec((1, D), const), pl.BlockSpec((1, 1, D), bat), pl.BlockSpec((1, D), const),
                  pl.BlockSpec((1, 1, D), bat), pl.BlockSpec((1, 1, D), bat),
                  pl.BlockSpec((2, D, LANES), lambda i: (0, 0, 0)), pl.BlockSpec((1, LANES), const)],
        out_specs=[pl.BlockSpec((tm, D), row), pl.BlockSpec((tm, D), row), pl.BlockSpec((tm, LANES), row)],
        out_shape=[jax.ShapeDtypeStruct((T, D), F32), jax.ShapeDtypeStruct((T, D), F32),
                   jax.ShapeDtypeStruct((T, LANES), F32)],
        compiler_params=_params(("parallel",)),
        name="out_proj_router",
    )(o_a, o_b, x2, w_out, g_post_mix.reshape(1, D), gt_a.reshape(B, 1, D), g_pre_ffn.reshape(1, D),
      sc_f.reshape(B, 1, D), sh_f.reshape(B, 1, D), w_r, b_r)


def _slots_kernel(rt_ref, slot_ref, blk_ref, cnt_ref, base_ref, *, n_blk_rows):
    phase = pl.program_id(0)
    step = pl.program_id(1)
    tm = rt_ref.shape[0]
    rt = rt_ref[...]
    lane = lax.broadcasted_iota(jnp.int32, (tm, LANES), 1).astype(F32)
    i1 = rt[:, 0:1]
    i2 = rt[:, 1:2]
    hit1 = lane == i1
    hit2 = lane == i2
    onehot = jnp.where(hit1, 1.0, jnp.where(hit2, 1.0, 0.0))
    colsum = jnp.sum(onehot, axis=0, keepdims=True)

    @pl.when((phase == 0) & (step == 0))
    def _():
        cnt_ref[...] = jnp.zeros(cnt_ref.shape, F32)

    @pl.when(phase == 0)
    def _():
        cnt_ref[...] += colsum

    @pl.when((phase == 1) & (step == 0))
    def _():
        cnt = cnt_ref[...]
        padded = jnp.floor((cnt + (MOE_ROWS - 1)) / MOE_ROWS) * MOE_ROWS
        lane1 = lax.broadcasted_iota(jnp.int32, (1, LANES), 1)
        incl = padded
        sh = 1
        while sh < LANES:
            incl = incl + jnp.where(lane1 >= sh, pltpu.roll(incl, sh, 1), 0.0)
            sh *= 2
        base_ref[...] = incl - padded
        rows = lax.broadcasted_iota(jnp.int32, (n_blk_rows, LANES), 0).astype(F32) * MOE_ROWS
        lane2 = lax.broadcasted_iota(jnp.int32, (n_blk_rows, LANES), 1)
        ended = jnp.where(lane2 < N_EXPERTS, jnp.where(incl <= rows, 1.0, 0.0), 0.0)
        blk_e = jnp.minimum(jnp.sum(ended, axis=-1, keepdims=True), N_EXPERTS - 1.0)
        total = jnp.sum(jnp.where(lane1 == N_EXPERTS - 1, incl, 0.0), axis=-1, keepdims=True)
        n_used = total / MOE_ROWS
        blk_ref[...] = jnp.where(lane2 == 0, blk_e, jnp.where(lane2 == 1, n_used, 0.0)).astype(jnp.int32)

    @pl.when(phase == 1)
    def _():
        r = lax.broadcasted_iota(jnp.int32, (tm, tm), 0)
        c = lax.broadcasted_iota(jnp.int32, (tm, tm), 1)
        tri = jnp.where(c < r, 1.0, 0.0).astype(BF16)
        before = jnp.dot(tri, onehot.astype(BF16), preferred_element_type=F32) + base_ref[...]
        s1 = jnp.sum(jnp.where(hit1, before, 0.0), axis=-1, keepdims=True)
        s2 = jnp.sum(jnp.where(hit2, before, 0.0), axis=-1, keepdims=True)
        lane_i = lax.broadcasted_iota(jnp.int32, (tm, LANES), 1)
        slot_ref[...] = jnp.where(lane_i == 0, s1, jnp.where(lane_i == 1, s2, 0.0)).astype(jnp.int32)
        base_ref[...] += colsum


def _slots(route, n_blocks):
    T = route.shape[0]
    tm = ROW_TILE
    n_blk_rows = -(-n_blocks // 8) * 8
    slot, blk = pl.pallas_call(
        functools.partial(_slots_kernel, n_blk_rows=n_blk_rows),
        grid=(2, T // tm),
        in_specs=[pl.BlockSpec((tm, LANES), lambda p, i: (i, 0))],
        out_specs=[pl.BlockSpec((tm, LANES), lambda p, i: (i * p, 0)),
                   pl.BlockSpec((n_blk_rows, LANES), lambda p, i: (0, 0))],
        out_shape=[jax.ShapeDtypeStruct((T, LANES), jnp.int32),
                   jax.ShapeDtypeStruct((n_blk_rows, LANES), jnp.int32)],
        scratch_shapes=[pltpu.VMEM((1, LANES), F32), pltpu.VMEM((1, LANES), F32)],
        compiler_params=_params(("arbitrary", "arbitrary")),
        name="moe_slots",
    )(route)
    return slot[:, 0], slot[:, 1], blk[:n_blocks, 0], blk[0:1, 1]


def _dispatch_kernel(s1_ref, s2_ref, h_ref, xz_ref, xb_ref, sem):
    del xz_ref
    tm = h_ref.shape[0]
    base = pl.program_id(0) * tm

    def issue(r, carry):
        src = h_ref.at[pl.ds(r, 1), :]
        pltpu.make_async_copy(src, xb_ref.at[pl.ds(s1_ref[base + r], 1), :], sem).start()
        pltpu.make_async_copy(src, xb_ref.at[pl.ds(s2_ref[base + r], 1), :], sem).start()
        return carry

    lax.fori_loop(0, tm, issue, 0, unroll=DMA_UNROLL)

    def drain(r, carry):
        for _ in range(TOP_K):
            pltpu.make_async_copy(h_ref.at[pl.ds(0, 1), :], xb_ref.at[pl.ds(0, 1), :], sem).wait()
        return carry

    lax.fori_loop(0, tm, drain, 0, unroll=DMA_UNROLL)


def _dispatch(h2, slot1, slot2, n_rows):
    T, D = h2.shape
    tm = ROW_TILE
    xz = jnp.zeros((n_rows, D), h2.dtype)
    return pl.pallas_call(
        _dispatch_kernel,
        grid_spec=pltpu.PrefetchScalarGridSpec(
            num_scalar_prefetch=2,
            grid=(T // tm,),
            in_specs=[pl.BlockSpec((tm, D), lambda i, a, b: (i, 0)),
                      pl.BlockSpec(memory_space=pl.ANY)],
            out_specs=pl.BlockSpec(memory_space=pl.ANY),
            scratch_shapes=[pltpu.SemaphoreType.DMA(())]),
        out_shape=jax.ShapeDtypeStruct((n_rows, D), h2.dtype),
        input_output_aliases={3: 0},
        compiler_params=_params(("arbitrary",)),
        name="moe_dispatch",
    )(slot1, slot2, h2, xz)


def _expert_kernel(be_ref, nu_ref, x_ref, wg_ref, wu_ref, wd_ref, y_ref, wgb_ref, wub_ref, wdb_ref):
    i = pl.program_id(0)

    @pl.when(i >= nu_ref[0])
    def _():
        y_ref[...] = jnp.zeros(y_ref.shape, y_ref.dtype)

    @pl.when(i < nu_ref[0])
    def _():
        changed = be_ref[i] != be_ref[jnp.maximum(i - 1, 0)]

        @pl.when((i == 0) | changed)
        def _():
            wgb_ref[...] = wg_ref[0].astype(BF16)
            wub_ref[...] = wu_ref[0].astype(BF16)
            wdb_ref[...] = wd_ref[0].astype(BF16)

        x = x_ref[...].astype(BF16)
        g = jnp.dot(x, wgb_ref[...], preferred_element_type=F32)
        u = jnp.dot(x, wub_ref[...], preferred_element_type=F32)
        hid = (g * jax.nn.sigmoid(g) * u).astype(BF16)
        y_ref[...] = jnp.dot(hid, wdb_ref[...], preferred_element_type=F32).astype(y_ref.dtype)


def _experts(xbuf, blk_e, n_used, w_gate, w_up, w_down):
    R, D = xbuf.shape
    E, _, De = w_gate.shape
    n_blocks = R // MOE_ROWS
    blk = lambda i, be, nu: (jnp.minimum(i, nu[0] - 1), 0)
    wsel = lambda i, be, nu: (be[jnp.minimum(i, nu[0] - 1)], 0, 0)
    return pl.pallas_call(
        _expert_kernel,
        grid_spec=pltpu.PrefetchScalarGridSpec(
            num_scalar_prefetch=2,
            grid=(n_blocks,),
            in_specs=[pl.BlockSpec((MOE_ROWS, D), blk),
                      pl.BlockSpec((1, D, De), wsel), pl.BlockSpec((1, D, De), wsel),
                      pl.BlockSpec((1, De, D), wsel)],
            out_specs=pl.BlockSpec((MOE_ROWS, D), lambda i, be, nu: (i, 0)),
            scratch_shapes=[pltpu.VMEM((D, De), BF16), pltpu.VMEM((D, De), BF16), pltpu.VMEM((De, D), BF16)]),
        out_shape=jax.ShapeDtypeStruct((R, D), F32),
        compiler_params=_params(("arbitrary",)),
        name="moe_experts",
    )(blk_e, n_used, xbuf, w_gate, w_up, w_down)


def _combine_kernel(s1_ref, s2_ref, yb_ref, rt_ref, x1_ref, gtf_ref, gpo_ref, o_ref, ya_ref, yc_ref, sem):
    tm = x1_ref.shape[0]
    base = pl.program_id(0) * tm

    def issue(r, carry):
        pltpu.make_async_copy(yb_ref.at[pl.ds(s1_ref[base + r], 1), :], ya_ref.at[pl.ds(r, 1), :], sem).start()
        pltpu.make_async_copy(yb_ref.at[pl.ds(s2_ref[base + r], 1), :], yc_ref.at[pl.ds(r, 1), :], sem).start()
        return carry

    lax.fori_loop(0, tm, issue, 0, unroll=DMA_UNROLL)

    def drain(r, carry):
        for _ in range(TOP_K):
            pltpu.make_async_copy(yb_ref.at[pl.ds(0, 1), :], ya_ref.at[pl.ds(0, 1), :], sem).wait()
        return carry

    lax.fori_loop(0, tm, drain, 0, unroll=DMA_UNROLL)
    rt = rt_ref[...]
    f = rt[:, 2:3] * ya_ref[...] + rt[:, 3:4] * yc_ref[...]
    o_ref[...] = x1_ref[...] + gtf_ref[0] * (_rms(f) * gpo_ref[...])


def _combine(ybuf, slot1, slot2, route, x1, gt_f, g_post_ffn, B, S):
    T, D = x1.shape
    tm = ROW_TILE
    per_b = S // tm
    row = lambda i, a, b: (i, 0)
    return pl.pallas_call(
        _combine_kernel,
        grid_spec=pltpu.PrefetchScalarGridSpec(
            num_scalar_prefetch=2,
            grid=(T // tm,),
            in_specs=[pl.BlockSpec(memory_space=pl.ANY),
                      pl.BlockSpec((tm, LANES), row), pl.BlockSpec((tm, D), row),
                      pl.BlockSpec((1, 1, D), lambda i, a, b: (i // per_b, 0, 0)),
                      pl.BlockSpec((1, D), lambda i, a, b: (0, 0))],
            out_specs=pl.BlockSpec((tm, D), row),
            scratch_shapes=[pltpu.VMEM((tm, D), F32), pltpu.VMEM((tm, D), F32), pltpu.SemaphoreType.DMA(())]),
        out_shape=jax.ShapeDtypeStruct((T, D), F32),
        compiler_params=_params(("arbitrary",)),
        name="moe_combine",
    )(slot1, slot2, ybuf, route, x1, gt_f.reshape(B, 1, D), g_post_ffn.reshape(1, D))


def _overlap_matrix(S):
    n_rows = S // CMP_STRIDE
    n_cmp = (S - CMP_BLOCK) // CMP_STRIDE + 1
    n_slc = S // SLC_BLOCK
    cstart = np.arange(n_rows) * CMP_STRIDE
    sstart = np.arange(n_slc) * SLC_BLOCK
    ov = (cstart[:, None] < sstart[None, :] + SLC_BLOCK) & (cstart[:, None] + CMP_BLOCK > sstart[None, :])
    ov &= (np.arange(n_rows) < n_cmp)[:, None]
    return jnp.asarray(ov.astype(np.float32), dtype=BF16)


def kernel(x, c, w_ada, b_ada, g_pre_mix, g_post_mix, g_pre_ffn, g_post_ffn, w_in, lam_q1, lam_k1, lam_q2,
           lam_k2, g_subln, cmp_pos_k, cmp_pos_v, cmp_w1_k, cmp_w1_v, cmp_w2_k, cmp_w2_v, w_out,
           w_router_grp, b_router_grp, w_router_exp, b_router_exp, w_exp_gate, w_exp_up, w_exp_down):
    B, S, D = x.shape
    T = B * S
    depth = w_ada.shape[0]
    assert S // SLC_BLOCK == LANES and D == D_MODEL
    n_blocks = (T * TOP_K) // MOE_ROWS + N_EXPERTS
    for l in range(depth):
        lambda_init = 0.8 - 0.6 * math.exp(-0.3 * l)
        mod = _adaln(c, w_ada[l], b_ada[l])
        sh_a, sc_a, gt_a, sh_f, sc_f, gt_f = jnp.split(mod, 6, axis=-1)

        dq, dk, dv_t, nq, ckv, ska, sv_t, wka, wv_t, gates = _in_proj(x, g_pre_mix[l], sc_a, sh_a,
                                                                      _build_w_proj(w_in[l]))
        kca, vc_t = _compress(ckv, cmp_pos_k[l], cmp_pos_v[l], cmp_w1_k[l], cmp_w1_v[l], cmp_w2_k[l], cmp_w2_v[l])
        o_a = _diff_attention(dq, dk, dv_t, (lam_q1[l], lam_k1[l], lam_q2[l], lam_k2[l]), g_subln[l], lambda_init)
        o_b = _nsa_attention(nq, kca, vc_t, ska, sv_t, wka, wv_t, gates)

        w_r = jnp.zeros((D, LANES), F32).at[:, :N_EXPERTS].set(w_router_exp[l]).at[
            :, N_EXPERTS:N_EXPERTS + N_GROUPS].set(w_router_grp[l])
        b_r = jnp.zeros((1, LANES), F32).at[0, :N_EXPERTS].set(b_router_exp[l]).at[
            0, N_EXPERTS:N_EXPERTS + N_GROUPS].set(b_router_grp[l])
        w_r_hi = w_r.astype(BF16)
        w_r = jnp.stack([w_r_hi, (w_r - w_r_hi.astype(F32)).astype(BF16)])
        x1, h2, route = _out_proj(o_a.reshape(T, -1), o_b.reshape(T, -1), x.reshape(T, D), w_out[l].astype(BF16),
                                  g_post_mix[l], gt_a, g_pre_ffn[l], sc_f, sh_f, w_r, b_r, B, S)
        slot1, slot2, blk_e, n_used = _slots(route, n_blocks)
        xbuf = _dispatch(h2, slot1, slot2, n_blocks * MOE_ROWS)
        ybuf = _experts(xbuf, blk_e, n_used, w_exp_gate[l], w_exp_up[l], w_exp_down[l])
        x = _combine(ybuf, slot1, slot2, route, x1, gt_f, g_post_ffn[l], B, S).reshape(B, S, D)
    return x
```

```python
import functools
import math

import numpy as np
import jax
import jax.numpy as jnp
from jax import lax
from jax.experimental import pallas as pl
from jax.experimental.pallas import tpu as pltpu

F32 = jnp.float32
BF16 = jnp.bfloat16
HIGHEST = lax.Precision.HIGHEST

D_MODEL = 1024
HEAD_DIM = 64
LANES = 128
DIFF_HEADS = 4
NSA_HEADS = 8
NSA_KV_HEADS = 2
NSA_GROUP = NSA_HEADS // NSA_KV_HEADS
CMP_BLOCK = 32
CMP_STRIDE = 16
CMP_HIDDEN = 2 * HEAD_DIM
SLC_BLOCK = 64
SLC_TOP = 16
WINDOW = 512
N_GROUPS = 8
EXPERTS_PER_GROUP = 8
N_EXPERTS = N_GROUPS * EXPERTS_PER_GROUP
TOP_K = 2
D_EXPERT = D_MODEL // 2
NEG = -1e30
FORCE_BONUS = 1e4
EPS = 1e-6

ROW_TILE = 512
KEY_TILE = 512
DIFF_TQ = 512
NSA_TQ = 256
SOFTMAX_CHUNK_VREGS = 32
MOE_ROWS = 256
DMA_UNROLL = 8
VMEM_LIMIT = 48 * 1024 * 1024
assert ROW_TILE == KEY_TILE == WINDOW

_OFF_DQ, _OFF_DK, _OFF_DV, _OFF_NQ = 0, 512, 1024, 1536
_OFF_KV = 2048
_OFF_NG = 2816
_N_GATES = 3 * NSA_HEADS
_W_PROJ = 2048 + 768 + LANES


def _alibi_slopes(n):
    return np.array([2.0 ** (-8.0 * (i + 1) / n) for i in range(n)], dtype=np.float32)


def _dot_nt(a, b):
    return lax.dot_general(a, b, (((1,), (1,)), ((), ())), preferred_element_type=F32)


def _rms(v):
    return v * lax.rsqrt(jnp.mean(v * v, axis=-1, keepdims=True) + EPS)


def _params(sem, vmem=VMEM_LIMIT):
    return pltpu.CompilerParams(dimension_semantics=sem, vmem_limit_bytes=vmem)


LOG2E = np.float32(1.0 / math.log(2.0))
N_SLOPE_PIECES = 3
MASK_LANE = 2 * N_SLOPE_PIECES


def _pos_lanes(pos, lane):
    half = lane % HEAD_DIM
    return jnp.where(half < MASK_LANE, jnp.where(half % 2 == 0, pos // LANES, pos % LANES), 0).astype(F32)


def _slope_cols(slopes, width, at):
    cols = np.zeros((len(slopes), width), np.float32)
    rest = (np.asarray(slopes, np.float32) * LOG2E).astype(np.float32)
    for i in range(N_SLOPE_PIECES):
        piece = rest.astype(BF16).astype(np.float32)
        cols[:, at + 2 * i] = piece * LANES
        cols[:, at + 2 * i + 1] = piece
        rest = rest - piece
    cols[:, at + MASK_LANE] = NEG
    return cols


def _adaln_kernel(c_ref, w_ref, b_ref, o_ref):
    c = c_ref[...]
    s = c * jax.nn.sigmoid(c)
    o_ref[...] = jnp.dot(s, w_ref[...], preferred_element_type=F32, precision=HIGHEST) + b_ref[...]


def _adaln(c, w_ada, b_ada):
    B, D = c.shape
    rows = 8
    cp = jnp.zeros((rows, D), F32).at[:B].set(c)
    n_out = w_ada.shape[1]
    out = pl.pallas_call(
        _adaln_kernel,
        grid=(n_out // D,),
        in_specs=[pl.BlockSpec((rows, D), lambda j: (0, 0)),
                  pl.BlockSpec((D, D), lambda j: (0, j)),
                  pl.BlockSpec((1, D), lambda j: (0, j))],
        out_specs=pl.BlockSpec((rows, D), lambda j: (0, j)),
        out_shape=jax.ShapeDtypeStruct((rows, n_out), F32),
        compiler_params=_params(("parallel",)),
        name="adaln",
    )(cp, w_ada, b_ada.reshape(1, n_out))
    return out[:B]


def _inproj_kernel(x_ref, g_ref, sc_ref, sh_ref, w_ref,
                   dq_ref, dk_ref, dvt_ref, nq_ref, ckv_ref, ska_ref, svt_ref, wka_ref, wvt_ref, gt_ref):
    step = pl.program_id(1)
    tm = x_ref.shape[1]
    dh = HEAD_DIM

    @pl.when(step == 0)
    def _():
        lane = lax.broadcasted_iota(jnp.int32, (tm, LANES), 1)
        for g in range(NSA_KV_HEADS):
            wka_ref[0, g] = jnp.where(lane == (1 - g) * dh + MASK_LANE, 1.0, 0.0).astype(BF16)
        wvt_ref[...] = jnp.zeros(wvt_ref.shape, wvt_ref.dtype)

    @pl.when(step > 0)
    def _():
        h = (_rms(x_ref[0]) * g_ref[...] * (1.0 + sc_ref[0]) + sh_ref[0]).astype(BF16)

        def proj(off, width):
            return jnp.dot(h, w_ref[:, off:off + width], preferred_element_type=F32)

        lane = lax.broadcasted_iota(jnp.int32, (tm, LANES), 1)
        pos = (step - 1) * tm + lax.broadcasted_iota(jnp.int32, (tm, 1), 0)
        low = lane < dh
        pcol = _pos_lanes(pos, lane)
        wcol = _pos_lanes(pos + WINDOW, lane)

        dq_ref[0] = proj(_OFF_DQ, 512).astype(BF16)
        k = proj(_OFF_DK, 512)
        v = proj(_OFF_DV, 512)
        for hh in range(DIFF_HEADS):
            dk_ref[0, :, hh * 256:hh * 256 + LANES] = k[:, hh * LANES:(hh + 1) * LANES].astype(BF16)
            dk_ref[0, :, hh * 256 + LANES:(hh + 1) * 256] = pcol.astype(BF16)
            dvt_ref[0, hh, 0] = v[:, hh * LANES:(hh + 1) * LANES].T.astype(BF16)
        nq_ref[0] = proj(_OFF_NQ, 512).astype(BF16)
        kv = proj(_OFF_KV, 768)
        ckv_ref[0] = kv[:, :2 * LANES].astype(BF16)
        sk, sv, wk, wv = (kv[:, i * LANES:(i + 1) * LANES] for i in range(2, 6))
        onehot = jnp.where(lane == pos // SLC_BLOCK, 1.0, 0.0).astype(BF16)
        sv_t = sv.T
        wv_t = wv.T
        for g, (keys_sel, keys_win) in enumerate(((jnp.where(low, sk, pcol), jnp.where(low, wk, wcol)),
                                                  (jnp.where(low, pcol, sk), jnp.where(low, wcol, wk)))):
            ska_ref[0, g, :, :LANES] = keys_sel.astype(BF16)
            ska_ref[0, g, :, LANES:] = onehot
            svt_ref[0, g, 0] = sv_t[g * dh:(g + 1) * dh].astype(BF16)
            wka_ref[0, g] = keys_win.astype(BF16)
            for c in range(tm // LANES):
                wvt_ref[0, g, c] = wv_t[g * dh:(g + 1) * dh, c * LANES:(c + 1) * LANES].astype(BF16)
        gt_ref[0] = jax.nn.sigmoid(proj(_OFF_NG, LANES))


def _build_w_proj(w_in):
    scale = LOG2E * HEAD_DIM ** -0.5
    nq = w_in[:, _OFF_NQ:_OFF_NQ + 512].reshape(D_MODEL, NSA_KV_HEADS, NSA_GROUP, HEAD_DIM)
    nq = nq.transpose(0, 2, 1, 3).reshape(D_MODEL, 512) * scale
    cols = [w_in[:, _OFF_DQ:_OFF_DQ + 512] * scale, w_in[:, _OFF_DK:_OFF_NQ], nq,
            w_in[:, _OFF_KV:_OFF_NG + _N_GATES], jnp.zeros((D_MODEL, LANES - _N_GATES), F32)]
    return jnp.concatenate(cols, axis=1).astype(BF16)


def _in_proj(x, g_pre, sc, sh, w_proj):
    B, S, D = x.shape
    tm = ROW_TILE
    n = S // tm
    KV, dh = NSA_KV_HEADS, HEAD_DIM
    tile = lambda b, s: (b, jnp.maximum(s - 1, 0), 0)
    out_shapes = [
        ((B, S, 512), (1, tm, 512), tile, BF16),
        ((B, S, DIFF_HEADS * 256), (1, tm, DIFF_HEADS * 256), tile, BF16),
        ((B, DIFF_HEADS, n, LANES, tm), (1, DIFF_HEADS, 1, LANES, tm),
         lambda b, s: (b, 0, jnp.maximum(s - 1, 0), 0, 0), BF16),
        ((B, S, 512), (1, tm, 512), tile, BF16),
        ((B, S, 2 * LANES), (1, tm, 2 * LANES), tile, BF16),
        ((B, KV, S, 2 * LANES), (1, KV, tm, 2 * LANES),
         lambda b, s: (b, 0, jnp.maximum(s - 1, 0), 0), BF16),
        ((B, KV, n, dh, tm), (1, KV, 1, dh, tm),
         lambda b, s: (b, 0, jnp.maximum(s - 1, 0), 0, 0), BF16),
        ((B, KV, S + WINDOW, LANES), (1, KV, tm, LANES), lambda b, s: (b, 0, s, 0), BF16),
        ((B, KV, (S + WINDOW) // LANES, dh, LANES), (1, KV, tm // LANES, dh, LANES),
         lambda b, s: (b, 0, s, 0, 0), BF16),
        ((B, S, LANES), (1, tm, LANES), tile, F32),
    ]
    return pl.pallas_call(
        _inproj_kernel,
        grid=(B, n + 1),
        in_specs=[pl.BlockSpec((1, tm, D), tile),
                  pl.BlockSpec((1, D), lambda b, s: (0, 0)),
                  pl.BlockSpec((1, 1, D), lambda b, s: (b, 0, 0)),
                  pl.BlockSpec((1, 1, D), lambda b, s: (b, 0, 0)),
                  pl.BlockSpec((D, _W_PROJ), lambda b, s: (0, 0))],
        out_specs=[pl.BlockSpec(blk, idx) for _, blk, idx, _ in out_shapes],
        out_shape=[jax.ShapeDtypeStruct(shape, dt) for shape, _, _, dt in out_shapes],
        compiler_params=_params(("parallel", "arbitrary")),
        name="in_proj",
    )(x, g_pre.reshape(1, D), sc.reshape(B, 1, D), sh.reshape(B, 1, D), w_proj)


def _compress_kernel(r_ref, pos_ref, w1_ref, w2_ref, kca_ref, vct_ref):
    half = CMP_STRIDE * HEAD_DIM
    dh = HEAD_DIM

    def mlp(t, h):
        r = r_ref[t, 0, h].astype(F32)
        pos = pos_ref[t]
        a = jnp.dot((r + pos[:, :half]).astype(BF16), w1_ref[t, :half], preferred_element_type=F32)
        b = jnp.dot((r + pos[:, half:]).astype(BF16), w1_ref[t, half:], preferred_element_type=F32)
        hid = a + pltpu.roll(b, a.shape[0] - 1, 0)
        return jnp.dot(jax.nn.gelu(hid).astype(BF16), w2_ref[t], preferred_element_type=F32)

    n_rows = r_ref.shape[3]
    lane = lax.broadcasted_iota(jnp.int32, (n_rows, LANES), 1)
    cend = lax.broadcasted_iota(jnp.int32, (n_rows, 1), 0) * CMP_STRIDE + (CMP_BLOCK - 1)
    pcol = _pos_lanes(cend, lane)
    kc = jnp.concatenate([mlp(0, 0), mlp(0, 1)], axis=-1)
    low = lane < dh
    kca_ref[0, 0] = jnp.where(low, kc, pcol).astype(BF16)
    kca_ref[0, 1] = jnp.where(low, pcol, kc).astype(BF16)
    vc_t = jnp.concatenate([mlp(1, 0), mlp(1, 1)], axis=-1).T
    for h in range(NSA_KV_HEADS):
        vct_ref[0, h] = vc_t[h * dh:(h + 1) * dh].astype(BF16)


def _compress(ckv, pos_k, pos_v, w1_k, w1_v, w2_k, w2_v):
    B, S, _ = ckv.shape
    KV, dh = NSA_KV_HEADS, HEAD_DIM
    n_rows = S // CMP_STRIDE
    wide = CMP_STRIDE * dh
    r = ckv.reshape(B, n_rows, CMP_STRIDE, 2, KV, dh).transpose(3, 0, 4, 1, 2, 5).reshape(2, B, KV, n_rows, wide)
    pos = jnp.stack([pos_k.reshape(1, -1), pos_v.reshape(1, -1)]).astype(F32)
    w1 = jnp.stack([w1_k, w1_v]).astype(BF16)
    w2 = jnp.stack([w2_k, w2_v]).astype(BF16)
    return pl.pallas_call(
        _compress_kernel,
        grid=(B,),
        in_specs=[pl.BlockSpec((2, 1, KV, n_rows, wide), lambda b: (0, b, 0, 0, 0)),
                  pl.BlockSpec((2, 1, 2 * wide), lambda b: (0, 0, 0)),
                  pl.BlockSpec((2, 2 * wide, CMP_HIDDEN), lambda b: (0, 0, 0)),
                  pl.BlockSpec((2, CMP_HIDDEN, dh), lambda b: (0, 0, 0))],
        out_specs=[pl.BlockSpec((1, KV, n_rows, LANES), lambda b: (b, 0, 0, 0)),
                   pl.BlockSpec((1, KV, dh, n_rows), lambda b: (b, 0, 0, 0))],
        out_shape=[jax.ShapeDtypeStruct((B, KV, n_rows, LANES), BF16),
                   jax.ShapeDtypeStruct((B, KV, dh, n_rows), BF16)],
        compiler_params=_params(("parallel",)),
        name="nsa_compress",
    )(r, pos, w1, w2)


def _flash_scratch(dv, n_cols):
    tk = KEY_TILE
    return [pltpu.VMEM((tk, n_cols), F32), pltpu.VMEM((tk, n_cols), F32),
            pltpu.VMEM((tk, n_cols), BF16), pltpu.VMEM((tk, n_cols), BF16),
            pltpu.VMEM((1, n_cols), F32), pltpu.VMEM((1, n_cols), F32),
            pltpu.VMEM((1, n_cols), F32), pltpu.VMEM((1, n_cols), F32), pltpu.VMEM((dv, n_cols), F32)]


def _flash_causal(q0, tq, qa_ref, k_tile, v_tile, scratch, tiles=None):
    s_a, s_b, p_a, p_b, a_a, a_b, m_ref, l_ref, acc_ref = scratch
    tk = KEY_TILE
    n_cols = qa_ref.shape[0]
    assert tk % tq == 0
    n_full = q0 // tk
    n = n_full + 1 if tiles is None else tiles[1]
    m_ref[...] = jnp.full(m_ref.shape, NEG, F32)
    l_ref[...] = jnp.zeros(l_ref.shape, F32)
    acc_ref[...] = jnp.zeros(acc_ref.shape, F32)
    p_b[...] = jnp.zeros(p_b.shape, BF16)
    a_b[...] = jnp.ones(a_b.shape, F32)

    def key_tile(t):
        t = jnp.clip(t, 0, n - 1)
        j = jnp.where(t == 0, n_full, t - 1) if tiles is None else tiles[0][t]
        return j, pl.multiple_of(j * tk, tk)

    def scores(t, s_ref, masked):
        j, k0 = key_tile(t)
        s = _dot_nt(k_tile(j, k0), qa_ref[...])
        if masked:
            krow = lax.broadcasted_iota(jnp.int32, (tk, 1), 0)
            qcol = lax.broadcasted_iota(jnp.int32, (1, n_cols), 1) % tq
            s = jnp.where((krow + (k0 - q0)) <= qcol, s, NEG)
        s_ref[...] = s

    def softmax(s_ref, p_ref, a_ref):
        ch = SOFTMAX_CHUNK_VREGS * 8 * LANES // n_cols
        chunks = [slice(c, c + ch) for c in range(0, tk, ch)]
        mx = None
        for rows in chunks:
            part = jnp.max(s_ref[rows, :].reshape(ch // 8, 8, n_cols), axis=0)
            mx = part if mx is None else jnp.maximum(mx, part)
        m_old = m_ref[...]
        m_new = jnp.maximum(m_old, jnp.max(mx, axis=0, keepdims=True))
        sm = None
        for rows in chunks:
            p = jnp.exp2(s_ref[rows, :] - m_new)
            p_ref[rows, :] = p.astype(BF16)
            part = jnp.sum(p.reshape(ch // 8, 8, n_cols), axis=0)
            sm = part if sm is None else sm + part
        alpha = jnp.exp2(m_old - m_new)
        a_ref[...] = alpha
        l_ref[...] = alpha * l_ref[...] + jnp.sum(sm, axis=0, keepdims=True)
        m_ref[...] = m_new

    def values(t, p_ref, a_ref):
        j, k0 = key_tile(t)
        acc_ref[...] = a_ref[...] * acc_ref[...] + jnp.dot(v_tile(j, k0), p_ref[...], preferred_element_type=F32)

    scores(0, s_a, True)

    def pair(u, carry):
        i = 2 * u
        scores(i + 1, s_b, False)
        softmax(s_a, p_a, a_a)
        values(i - 1, p_b, a_b)

        @pl.when(i + 1 < n)
        def _():
            scores(i + 2, s_a, False)
            softmax(s_b, p_b, a_b)
            values(i, p_a, a_a)

        return carry

    lax.fori_loop(0, (n + 1) // 2, pair, 0)

    @pl.when(n % 2 == 1)
    def _():
        values(n - 1, p_a, a_a)

    @pl.when(n % 2 == 0)
    def _():
        values(n - 1, p_b, a_b)

    return acc_ref, l_ref


def _softmax_cols(s, bias):
    s = s + bias
    m = jnp.max(s, axis=0, keepdims=True)
    e = jnp.exp2(s - m)
    inv = jnp.where(m > 0.5 * NEG, 1.0 / jnp.sum(e, axis=0, keepdims=True), 0.0)
    return e * inv


def _diff_kernel(qx_ref, lam_ref, gsub_ref, q_ref, k_ref, v_ref, o_ref, qa_ref, *scratch, lambda_init):
    tq = DIFF_TQ
    q0 = pl.program_id(2) * tq
    q = q_ref[0].astype(F32)
    low = lax.broadcasted_iota(jnp.int32, (tq, LANES), 1) < HEAD_DIM
    ext = jnp.broadcast_to(qx_ref[0], (tq, LANES)).astype(BF16)
    qa_ref[0:tq, :LANES] = jnp.where(low, q, 0.0).astype(BF16)
    qa_ref[tq:, :LANES] = jnp.where(low, 0.0, q).astype(BF16)
    qa_ref[0:tq, LANES:] = ext
    qa_ref[tq:, LANES:] = ext

    acc_ref, l_ref = _flash_causal(q0, tq, qa_ref,
                                   lambda j, k0: k_ref[0, pl.ds(k0, KEY_TILE), :],
                                   lambda j, k0: v_ref[0, 0, j],
                                   scratch)

    lv = lam_ref[...]
    lam = (jnp.exp(jnp.sum(lv[0:1] * lv[1:2], axis=-1, keepdims=True))
           - jnp.exp(jnp.sum(lv[2:3] * lv[3:4], axis=-1, keepdims=True)) + lambda_init)
    o_t = acc_ref[...] / l_ref[...]
    o_t = o_t[:, :tq] - lam * o_t[:, tq:]
    o_t = o_t * lax.rsqrt(jnp.mean(o_t * o_t, axis=0, keepdims=True) + EPS)
    o_ref[0] = (o_t.T * gsub_ref[...] * (1.0 - lambda_init)).astype(o_ref.dtype)


def _diff_attention(dq, dk, dv_t, lam_vecs, g_subln, lambda_init):
    B, S, _ = dq.shape
    tq, tk = DIFF_TQ, KEY_TILE
    H = DIFF_HEADS
    qx = jnp.asarray(_slope_cols(_alibi_slopes(H), LANES, 0)).reshape(H, 1, LANES)
    lam = jnp.zeros((8, LANES), F32).at[:4, :HEAD_DIM].set(jnp.stack(lam_vecs))
    return pl.pallas_call(
        functools.partial(_diff_kernel, lambda_init=lambda_init),
        grid=(B, H, S // tq),
        in_specs=[pl.BlockSpec((1, 1, LANES), lambda b, h, i: (h, 0, 0)),
                  pl.BlockSpec((8, LANES), lambda b, h, i: (0, 0)),
                  pl.BlockSpec((1, LANES), lambda b, h, i: (0, 0)),
                  pl.BlockSpec((1, tq, LANES), lambda b, h, i: (b, i, h)),
                  pl.BlockSpec((1, S, 2 * LANES), lambda b, h, i: (b, 0, h)),
                  pl.BlockSpec((1, 1, S // tk, LANES, tk), lambda b, h, i: (b, h, 0, 0, 0))],
        out_specs=pl.BlockSpec((1, tq, LANES), lambda b, h, i: (b, i, h)),
        out_shape=jax.ShapeDtypeStruct((B, S, H * LANES), BF16),
        scratch_shapes=[pltpu.VMEM((2 * tq, 2 * LANES), BF16)] + _flash_scratch(LANES, 2 * tq),
        compiler_params=_params(("parallel", "parallel", "parallel")),
        name="diff_attention",
    )(qx, lam, g_subln.reshape(1, LANES).astype(F32), dq, dk, dv_t)


def _nsa_kernel(qx_ref, q_ref, kc_ref, vc_ref, ka_ref, sv_ref, wk_ref, wv_ref, g_ref, ov_ref, cb_ref, wb_ref,
                o_ref, qa_ref, out_ref, tiles_ref, *scratch):
    tq, G, dh = NSA_TQ, NSA_GROUP, HEAD_DIM
    n_cmp = kc_ref.shape[2]
    n_win = WINDOW + tq
    grp = pl.program_id(1)
    qi = pl.program_id(2)
    q0 = pl.multiple_of(qi * tq, tq)
    cols = [slice(i * tq, (i + 1) * tq) for i in range(G)]
    g_t = g_ref[0].T

    def gate(i, branch):
        a = 3 * i + branch
        b = 3 * (G + i) + branch
        return jnp.where(grp == 0, g_t[a:a + 1, :], g_t[b:b + 1, :])

    lane_q = lax.broadcasted_iota(jnp.int32, (tq, LANES), 1)
    mine = (lane_q >= grp * dh) & (lane_q < (grp + 1) * dh)
    qx = qx_ref[0]
    for i in range(G):
        q = q_ref[0, :, i * LANES:(i + 1) * LANES].astype(F32)
        qa_ref[cols[i], :LANES] = jnp.where(mine, q, jnp.broadcast_to(qx[i:i + 1], (tq, LANES))).astype(BF16)
    qs = qa_ref[:, :LANES]

    kw = wk_ref[0, 0, pl.ds(q0, n_win), :]
    vw = jnp.concatenate([wv_ref[0, 0, (tq // LANES) * qi + c] for c in range(n_win // LANES)], axis=1)
    p = _softmax_cols(_dot_nt(kw, qs), jnp.tile(wb_ref[...], (1, G)))
    ow = jnp.dot(vw, p.astype(BF16), preferred_element_type=F32)

    cbias = cb_ref[pl.ds(pl.multiple_of(n_cmp - (tq // CMP_STRIDE) * qi, 8), n_cmp), :]
    p = _softmax_cols(_dot_nt(kc_ref[0, 0], qs), jnp.tile(cbias, (1, G)))
    oc = jnp.dot(vc_ref[0, 0], p.astype(BF16), preferred_element_type=F32)
    for i in range(G):
        out_ref[i * dh:(i + 1) * dh, :] = gate(i, 0) * oc[:, cols[i]] + gate(i, 2) * ow[:, cols[i]]

    psum = p[:, cols[0]]
    for i in range(1, G):
        psum = psum + p[:, cols[i]]
    hi = psum.astype(BF16)
    r1 = psum - hi.astype(F32)
    mid = r1.astype(BF16)
    lo = (r1 - mid.astype(F32)).astype(BF16)
    ov = ov_ref[...]
    imp = (jnp.dot(ov, hi, preferred_element_type=F32) + jnp.dot(ov, mid, preferred_element_type=F32)
           + jnp.dot(ov, lo, preferred_element_type=F32))
    n_slc = imp.shape[0]
    blk = lax.broadcasted_iota(jnp.int32, (n_slc, tq), 0)
    cur = (lax.broadcasted_iota(jnp.int32, (1, tq), 1) + q0) // SLC_BLOCK
    forced = (blk == 0) | (blk == cur) | (blk == cur - 1)
    score = jnp.where(forced, -jnp.inf, jnp.where(blk <= cur, imp, NEG))
    blkf = blk.astype(F32)
    sel = jnp.where(forced, 1.0, 0.0)
    for _ in range(SLC_TOP - 3):
        mx = jnp.max(score, axis=0, keepdims=True)
        idx = jnp.min(jnp.where(score == mx, blkf, float(n_slc)), axis=0, keepdims=True)
        hit = blkf == idx
        sel = jnp.where(hit, 1.0, sel)
        score = jnp.where(hit, -jnp.inf, score)
    sel_t = sel.T
    blockmask = ((sel_t - 1.0) * 1e30).astype(BF16)
    for i in range(G):
        qa_ref[cols[i], LANES:] = blockmask

    per_tile = KEY_TILE // SLC_BLOCK
    n_tiles = n_slc // per_tile
    used = jnp.max(jnp.max(sel, axis=1, keepdims=True).reshape(n_tiles, per_tile, 1), axis=1)
    bit = jnp.where(used > 0.0, 1 << lax.broadcasted_iota(jnp.int32, (n_tiles, 1), 0), 0)
    bits = jnp.sum(bit, axis=0, keepdims=True)[0, 0]
    n_full = q0 // KEY_TILE
    tiles_ref[0] = n_full
    count = jnp.int32(1)
    for j in range(n_tiles):
        tiles_ref[count] = j
        count = count + jnp.where(j < n_full, (bits >> j) & 1, 0)

    acc_ref, l_ref = _flash_causal(q0, tq, qa_ref,
                                   lambda j, k0: ka_ref[0, 0, pl.ds(k0, KEY_TILE), :],
                                   lambda j, k0: sv_ref[0, 0, j],
                                   scratch, tiles=(tiles_ref, count))
    os_ = acc_ref[...] / l_ref[...]
    for i in range(G):
        out_ref[i * dh:(i + 1) * dh, :] += gate(i, 1) * os_[:, cols[i]]

    o_ref[0] = out_ref[...].T.astype(o_ref.dtype)


def _nsa_attention(nq, kca, vc_t, ska, sv_t, wka, wv_t, gates):
    B, S, _ = nq.shape
    tq, tk, G, dh = NSA_TQ, KEY_TILE, NSA_GROUP, HEAD_DIM
    KV = NSA_KV_HEADS
    n_cmp = kca.shape[2]
    n_slc = S // SLC_BLOCK
    Sp = S + WINDOW
    slopes = _alibi_slopes(NSA_HEADS)
    qx = jnp.asarray(np.stack([_slope_cols(slopes[:G], LANES, dh), _slope_cols(slopes[G:], LANES, 0)]))
    overlap_t = _overlap_matrix(S).T
    r = np.arange(tq)[None, :]
    j = np.arange(2 * n_cmp)[:, None]
    cmp_band = jnp.asarray(np.where(CMP_STRIDE * (j - n_cmp) + CMP_BLOCK - 1 <= r, 0.0, NEG).astype(np.float32))
    c = np.arange(WINDOW + tq)[:, None]
    d = r - c + WINDOW
    win_band = jnp.asarray(np.where((d >= 0) & (d < WINDOW), 0.0, NEG).astype(np.float32))
    const2 = lambda b, g, i: (0, 0)
    per_bg = lambda b, g, i: (b, g, 0, 0)
    per_bg5 = lambda b, g, i: (b, g, 0, 0, 0)
    return pl.pallas_call(
        _nsa_kernel,
        grid=(B, KV, S // tq),
        in_specs=[pl.BlockSpec((1, G, LANES), lambda b, g, i: (g, 0, 0)),
                  pl.BlockSpec((1, tq, G * LANES), lambda b, g, i: (b, i, 0)),
                  pl.BlockSpec((1, 1, n_cmp, LANES), per_bg),
                  pl.BlockSpec((1, 1, dh, n_cmp), per_bg),
                  pl.BlockSpec((1, 1, S, 2 * LANES), per_bg),
                  pl.BlockSpec((1, 1, S // tk, dh, tk), per_bg5),
                  pl.BlockSpec((1, 1, Sp, LANES), per_bg),
                  pl.BlockSpec((1, 1, Sp // LANES, dh, LANES), per_bg5),
                  pl.BlockSpec((1, tq, LANES), lambda b, g, i: (b, i, 0)),
                  pl.BlockSpec((n_slc, n_cmp), const2),
                  pl.BlockSpec(cmp_band.shape, const2), pl.BlockSpec(win_band.shape, const2)],
        out_specs=pl.BlockSpec((1, tq, G * dh), lambda b, g, i: (b, i, g)),
        out_shape=jax.ShapeDtypeStruct((B, S, NSA_HEADS * dh), BF16),
        scratch_shapes=[pltpu.VMEM((G * tq, 2 * LANES), BF16),
                        pltpu.VMEM((G * dh, tq), F32),
                        pltpu.SMEM((2 * (S // tk),), jnp.int32)]
        + _flash_scratch(dh, G * tq),
        compiler_params=_params(("parallel", "parallel", "parallel")),
        name="nsa_attention",
    )(qx, nq, kca, vc_t, ska, sv_t, wka, wv_t, gates, overlap_t, cmp_band, win_band)


def _outproj_kernel(oa_ref, ob_ref, x_ref, wo_ref, gpm_ref, gta_ref, gpf_ref, sc_ref, sh_ref, wr_ref, br_ref,
                    x1_ref, h2_ref, rt_ref):
    half = oa_ref.shape[1]
    mix = (jnp.dot(oa_ref[...], wo_ref[:half], preferred_element_type=F32)
           + jnp.dot(ob_ref[...], wo_ref[half:], preferred_element_type=F32))
    x1 = x_ref[...] + gta_ref[0] * (_rms(mix) * gpm_ref[...])
    x1_ref[...] = x1
    h2 = _rms(x1) * gpf_ref[...] * (1.0 + sc_ref[0]) + sh_ref[0]
    h2_ref[...] = h2
    h_hi = h2.astype(BF16)
    h_lo = (h2 - h_hi.astype(F32)).astype(BF16)
    logits = (jnp.dot(h_hi, wr_ref[0], preferred_element_type=F32)
              + jnp.dot(h_lo, wr_ref[0], preferred_element_type=F32)
              + jnp.dot(h_hi, wr_ref[1], preferred_element_type=F32)) + br_ref[...]
    tm = logits.shape[0]
    lane = lax.broadcasted_iota(jnp.int32, (tm, LANES), 1)
    lanef = lane.astype(F32)
    is_grp = (lane >= N_EXPERTS) & (lane < N_EXPERTS + N_GROUPS)
    lg = jnp.where(is_grp, logits, -jnp.inf)
    mg = jnp.max(lg, axis=-1, keepdims=True)
    g_lane = jnp.min(jnp.where(lg == mg, lanef, 1e9), axis=-1, keepdims=True)
    p_grp = 1.0 / jnp.sum(jnp.where(is_grp, jnp.exp(lg - mg), 0.0), axis=-1, keepdims=True)
    grp = g_lane.astype(jnp.int32) - N_EXPERTS
    in_grp = (lane < N_EXPERTS) & ((lane // EXPERTS_PER_GROUP) == grp)
    le = jnp.where(in_grp, logits, -jnp.inf)
    v1 = jnp.max(le, axis=-1, keepdims=True)
    i1 = jnp.min(jnp.where(le == v1, lanef, 1e9), axis=-1, keepdims=True)
    le2 = jnp.where(lanef == i1, -jnp.inf, le)
    v2 = jnp.max(le2, axis=-1, keepdims=True)
    i2 = jnp.min(jnp.where(le2 == v2, lanef, 1e9), axis=-1, keepdims=True)
    e2 = jnp.exp(v2 - v1)
    w1 = p_grp / (1.0 + e2)
    w2 = p_grp * e2 / (1.0 + e2)
    rt_ref[...] = jnp.where(lane == 0, i1, jnp.where(lane == 1, i2, jnp.where(lane == 2, w1,
                            jnp.where(lane == 3, w2, 0.0))))


def _out_proj(o_a, o_b, x2, w_out, g_post_mix, gt_a, g_pre_ffn, sc_f, sh_f, w_r, b_r, B, S):
    T, D = x2.shape
    tm = ROW_TILE
    per_b = S // tm
    row = lambda i: (i, 0)
    bat = lambda i: (i // per_b, 0, 0)
    const = lambda i: (0, 0)
    return pl.pallas_call(
        _outproj_kernel,
        grid=(T // tm,),
        in_specs=[pl.BlockSpec((tm, o_a.shape[1]), row), pl.BlockSpec((tm, o_b.shape[1]), row),
                  pl.BlockSpec((tm, D), row), pl.BlockSpec((D, D), const),
                  pl.BlockSpec((1, D), const), pl.BlockSpec((1, 1, D), bat), pl.BlockSpec((1, D), const),
                  pl.BlockSpec((1, 1, D), bat), pl.BlockSpec((1, 1, D), bat),
                  pl.BlockSpec((2, D, LANES), lambda i: (0, 0, 0)), pl.BlockSpec((1, LANES), const)],
        out_specs=[pl.BlockSpec((tm, D), row), pl.BlockSpec((tm, D), row), pl.BlockSpec((tm, LANES), row)],
        out_shape=[jax.ShapeDtypeStruct((T, D), F32), jax.ShapeDtypeStruct((T, D), F32),
                   jax.ShapeDtypeStruct((T, LANES), F32)],
        compiler_params=_params(("parallel",)),
        name="out_proj_router",
    )(o_a, o_b, x2, w_out, g_post_mix.reshape(1, D), gt_a.reshape(B, 1, D), g_pre_ffn.reshape(1, D),
      sc_f.reshape(B, 1, D), sh_f.reshape(B, 1, D), w_r, b_r)


def _slots_kernel(rt_ref, slot_ref, blk_ref, cnt_ref, base_ref, *, n_blk_rows):
    phase = pl.program_id(0)
    step = pl.program_id(1)
    tm = rt_ref.shape[0]
    rt = rt_ref[...]
    lane = lax.broadcasted_iota(jnp.int32, (tm, LANES), 1).astype(F32)
    i1 = rt[:, 0:1]
    i2 = rt[:, 1:2]
    hit1 = lane == i1
    hit2 = lane == i2
    onehot = jnp.where(hit1, 1.0, jnp.where(hit2, 1.0, 0.0))
    colsum = jnp.sum(onehot, axis=0, keepdims=True)

    @pl.when((phase == 0) & (step == 0))
    def _():
        cnt_ref[...] = jnp.zeros(cnt_ref.shape, F32)

    @pl.when(phase == 0)
    def _():
        cnt_ref[...] += colsum

    @pl.when((phase == 1) & (step == 0))
    def _():
        cnt = cnt_ref[...]
        padded = jnp.floor((cnt + (MOE_ROWS - 1)) / MOE_ROWS) * MOE_ROWS
        lane1 = lax.broadcasted_iota(jnp.int32, (1, LANES), 1)
        incl = padded
        sh = 1
        while sh < LANES:
            incl = incl + jnp.where(lane1 >= sh, pltpu.roll(incl, sh, 1), 0.0)
            sh *= 2
        base_ref[...] = incl - padded
        rows = lax.broadcasted_iota(jnp.int32, (n_blk_rows, LANES), 0).astype(F32) * MOE_ROWS
        lane2 = lax.broadcasted_iota(jnp.int32, (n_blk_rows, LANES), 1)
        ended = jnp.where(lane2 < N_EXPERTS, jnp.where(incl <= rows, 1.0, 0.0), 0.0)
        blk_e = jnp.minimum(jnp.sum(ended, axis=-1, keepdims=True), N_EXPERTS - 1.0)
        total = jnp.sum(jnp.where(lane1 == N_EXPERTS - 1, incl, 0.0), axis=-1, keepdims=True)
        n_used = total / MOE_ROWS
        blk_ref[...] = jnp.where(lane2 == 0, blk_e, jnp.where(lane2 == 1, n_used, 0.0)).astype(jnp.int32)

    @pl.when(phase == 1)
    def _():
        r = lax.broadcasted_iota(jnp.int32, (tm, tm), 0)
        c = lax.broadcasted_iota(jnp.int32, (tm, tm), 1)
        tri = jnp.where(c < r, 1.0, 0.0).astype(BF16)
        before = jnp.dot(tri, onehot.astype(BF16), preferred_element_type=F32) + base_ref[...]
        s1 = jnp.sum(jnp.where(hit1, before, 0.0), axis=-1, keepdims=True)
        s2 = jnp.sum(jnp.where(hit2, before, 0.0), axis=-1, keepdims=True)
        lane_i = lax.broadcasted_iota(jnp.int32, (tm, LANES), 1)
        slot_ref[...] = jnp.where(lane_i == 0, s1, jnp.where(lane_i == 1, s2, 0.0)).astype(jnp.int32)
        base_ref[...] += colsum


def _slots(route, n_blocks):
    T = route.shape[0]
    tm = ROW_TILE
    n_blk_rows = -(-n_blocks // 8) * 8
    slot, blk = pl.pallas_call(
        functools.partial(_slots_kernel, n_blk_rows=n_blk_rows),
        grid=(2, T // tm),
        in_specs=[pl.BlockSpec((tm, LANES), lambda p, i: (i, 0))],
        out_specs=[pl.BlockSpec((tm, LANES), lambda p, i: (i * p, 0)),
                   pl.BlockSpec((n_blk_rows, LANES), lambda p, i: (0, 0))],
        out_shape=[jax.ShapeDtypeStruct((T, LANES), jnp.int32),
                   jax.ShapeDtypeStruct((n_blk_rows, LANES), jnp.int32)],
        scratch_shapes=[pltpu.VMEM((1, LANES), F32), pltpu.VMEM((1, LANES), F32)],
        compiler_params=_params(("arbitrary", "arbitrary")),
        name="moe_slots",
    )(route)
    return slot[:, 0], slot[:, 1], blk[:n_blocks, 0], blk[0:1, 1]


def _dispatch_kernel(s1_ref, s2_ref, h_ref, xz_ref, xb_ref, sem):
    del xz_ref
    tm = h_ref.shape[0]
    base = pl.program_id(0) * tm

    def issue(r, carry):
        src = h_ref.at[pl.ds(r, 1), :]
        pltpu.make_async_copy(src, xb_ref.at[pl.ds(s1_ref[base + r], 1), :], sem).start()
        pltpu.make_async_copy(src, xb_ref.at[pl.ds(s2_ref[base + r], 1), :], sem).start()
        return carry

    lax.fori_loop(0, tm, issue, 0, unroll=DMA_UNROLL)

    def drain(r, carry):
        for _ in range(TOP_K):
            pltpu.make_async_copy(h_ref.at[pl.ds(0, 1), :], xb_ref.at[pl.ds(0, 1), :], sem).wait()
        return carry

    lax.fori_loop(0, tm, drain, 0, unroll=DMA_UNROLL)


def _dispatch(h2, slot1, slot2, n_rows):
    T, D = h2.shape
    tm = ROW_TILE
    xz = jnp.zeros((n_rows, D), h2.dtype)
    return pl.pallas_call(
        _dispatch_kernel,
        grid_spec=pltpu.PrefetchScalarGridSpec(
            num_scalar_prefetch=2,
            grid=(T // tm,),
            in_specs=[pl.BlockSpec((tm, D), lambda i, a, b: (i, 0)),
                      pl.BlockSpec(memory_space=pl.ANY)],
            out_specs=pl.BlockSpec(memory_space=pl.ANY),
            scratch_shapes=[pltpu.SemaphoreType.DMA(())]),
        out_shape=jax.ShapeDtypeStruct((n_rows, D), h2.dtype),
        input_output_aliases={3: 0},
        compiler_params=_params(("arbitrary",)),
        name="moe_dispatch",
    )(slot1, slot2, h2, xz)


def _expert_kernel(be_ref, nu_ref, x_ref, wg_ref, wu_ref, wd_ref, y_ref, wgb_ref, wub_ref, wdb_ref):
    i = pl.program_id(0)

    @pl.when(i >= nu_ref[0])
    def _():
        y_ref[...] = jnp.zeros(y_ref.shape, y_ref.dtype)

    @pl.when(i < nu_ref[0])
    def _():
        changed = be_ref[i] != be_ref[jnp.maximum(i - 1, 0)]

        @pl.when((i == 0) | changed)
        def _():
            wgb_ref[...] = wg_ref[0].astype(BF16)
            wub_ref[...] = wu_ref[0].astype(BF16)
            wdb_ref[...] = wd_ref[0].astype(BF16)

        x = x_ref[...].astype(BF16)
        g = jnp.dot(x, wgb_ref[...], preferred_element_type=F32)
        u = jnp.dot(x, wub_ref[...], preferred_element_type=F32)
        hid = (g * jax.nn.sigmoid(g) * u).astype(BF16)
        y_ref[...] = jnp.dot(hid, wdb_ref[...], preferred_element_type=F32).astype(y_ref.dtype)


def _experts(xbuf, blk_e, n_used, w_gate, w_up, w_down):
    R, D = xbuf.shape
    E, _, De = w_gate.shape
    n_blocks = R // MOE_ROWS
    blk = lambda i, be, nu: (jnp.minimum(i, nu[0] - 1), 0)
    wsel = lambda i, be, nu: (be[jnp.minimum(i, nu[0] - 1)], 0, 0)
    return pl.pallas_call(
        _expert_kernel,
        grid_spec=pltpu.PrefetchScalarGridSpec(
            num_scalar_prefetch=2,
            grid=(n_blocks,),
            in_specs=[pl.BlockSpec((MOE_ROWS, D), blk),
                      pl.BlockSpec((1, D, De), wsel), pl.BlockSpec((1, D, De), wsel),
                      pl.BlockSpec((1, De, D), wsel)],
            out_specs=pl.BlockSpec((MOE_ROWS, D), lambda i, be, nu: (i, 0)),
            scratch_shapes=[pltpu.VMEM((D, De), BF16), pltpu.VMEM((D, De), BF16), pltpu.VMEM((De, D), BF16)]),
        out_shape=jax.ShapeDtypeStruct((R, D), F32),
        compiler_params=_params(("arbitrary",)),
        name="moe_experts",
    )(blk_e, n_used, xbuf, w_gate, w_up, w_down)


def _combine_kernel(s1_ref, s2_ref, yb_ref, rt_ref, x1_ref, gtf_ref, gpo_ref, o_ref, ya_ref, yc_ref, sem):
    tm = x1_ref.shape[0]
    base = pl.program_id(0) * tm

    def issue(r, carry):
        pltpu.make_async_copy(yb_ref.at[pl.ds(s1_ref[base + r], 1), :], ya_ref.at[pl.ds(r, 1), :], sem).start()
        pltpu.make_async_copy(yb_ref.at[pl.ds(s2_ref[base + r], 1), :], yc_ref.at[pl.ds(r, 1), :], sem).start()
        return carry

    lax.fori_loop(0, tm, issue, 0, unroll=DMA_UNROLL)

    def drain(r, carry):
        for _ in range(TOP_K):
            pltpu.make_async_copy(yb_ref.at[pl.ds(0, 1), :], ya_ref.at[pl.ds(0, 1), :], sem).wait()
        return carry

    lax.fori_loop(0, tm, drain, 0, unroll=DMA_UNROLL)
    rt = rt_ref[...]
    f = rt[:, 2:3] * ya_ref[...] + rt[:, 3:4] * yc_ref[...]
    o_ref[...] = x1_ref[...] + gtf_ref[0] * (_rms(f) * gpo_ref[...])


def _combine(ybuf, slot1, slot2, route, x1, gt_f, g_post_ffn, B, S):
    T, D = x1.shape
    tm = ROW_TILE
    per_b = S // tm
    row = lambda i, a, b: (i, 0)
    return pl.pallas_call(
        _combine_kernel,
        grid_spec=pltpu.PrefetchScalarGridSpec(
            num_scalar_prefetch=2,
            grid=(T // tm,),
            in_specs=[pl.BlockSpec(memory_space=pl.ANY),
                      pl.BlockSpec((tm, LANES), row), pl.BlockSpec((tm, D), row),
                      pl.BlockSpec((1, 1, D), lambda i, a, b: (i // per_b, 0, 0)),
                      pl.BlockSpec((1, D), lambda i, a, b: (0, 0))],
            out_specs=pl.BlockSpec((tm, D), row),
            scratch_shapes=[pltpu.VMEM((tm, D), F32), pltpu.VMEM((tm, D), F32), pltpu.SemaphoreType.DMA(())]),
        out_shape=jax.ShapeDtypeStruct((T, D), F32),
        compiler_params=_params(("arbitrary",)),
        name="moe_combine",
    )(slot1, slot2, ybuf, route, x1, gt_f.reshape(B, 1, D), g_post_ffn.reshape(1, D))


def _overlap_matrix(S):
    n_rows = S // CMP_STRIDE
    n_cmp = (S - CMP_BLOCK) // CMP_STRIDE + 1
    n_slc = S // SLC_BLOCK
    cstart = np.arange(n_rows) * CMP_STRIDE
    sstart = np.arange(n_slc) * SLC_BLOCK
    ov = (cstart[:, None] < sstart[None, :] + SLC_BLOCK) & (cstart[:, None] + CMP_BLOCK > sstart[None, :])
    ov &= (np.arange(n_rows) < n_cmp)[:, None]
    return jnp.asarray(ov.astype(np.float32), dtype=BF16)


def kernel(x, c, w_ada, b_ada, g_pre_mix, g_post_mix, g_pre_ffn, g_post_ffn, w_in, lam_q1, lam_k1, lam_q2,
           lam_k2, g_subln, cmp_pos_k, cmp_pos_v, cmp_w1_k, cmp_w1_v, cmp_w2_k, cmp_w2_v, w_out,
           w_router_grp, b_router_grp, w_router_exp, b_router_exp, w_exp_gate, w_exp_up, w_exp_down):
    B, S, D = x.shape
    T = B * S
    depth = w_ada.shape[0]
    assert S // SLC_BLOCK == LANES and D == D_MODEL
    n_blocks = (T * TOP_K) // MOE_ROWS + N_EXPERTS
    for l in range(depth):
        lambda_init = 0.8 - 0.6 * math.exp(-0.3 * l)
        mod = _adaln(c, w_ada[l], b_ada[l])
        sh_a, sc_a, gt_a, sh_f, sc_f, gt_f = jnp.split(mod, 6, axis=-1)

        dq, dk, dv_t, nq, ckv, ska, sv_t, wka, wv_t, gates = _in_proj(x, g_pre_mix[l], sc_a, sh_a,
                                                                      _build_w_proj(w_in[l]))
        kca, vc_t = _compress(ckv, cmp_pos_k[l], cmp_pos_v[l], cmp_w1_k[l], cmp_w1_v[l], cmp_w2_k[l], cmp_w2_v[l])
        o_a = _diff_attention(dq, dk, dv_t, (lam_q1[l], lam_k1[l], lam_q2[l], lam_k2[l]), g_subln[l], lambda_init)
        o_b = _nsa_attention(nq, kca, vc_t, ska, sv_t, wka, wv_t, gates)

        w_r = jnp.zeros((D, LANES), F32).at[:, :N_EXPERTS].set(w_router_exp[l]).at[
            :, N_EXPERTS:N_EXPERTS + N_GROUPS].set(w_router_grp[l])
        b_r = jnp.zeros((1, LANES), F32).at[0, :N_EXPERTS].set(b_router_exp[l]).at[
            0, N_EXPERTS:N_EXPERTS + N_GROUPS].set(b_router_grp[l])
        w_r_hi = w_r.astype(BF16)
        w_r = jnp.stack([w_r_hi, (w_r - w_r_hi.astype(F32)).astype(BF16)])
        x1, h2, route = _out_proj(o_a.reshape(T, -1), o_b.reshape(T, -1), x.reshape(T, D), w_out[l].astype(BF16),
                                  g_post_mix[l], gt_a, g_pre_ffn[l], sc_f, sh_f, w_r, b_r, B, S)
        slot1, slot2, blk_e, n_used = _slots(route, n_blocks)
        xbuf = _dispatch(h2, slot1, slot2, n_blocks * MOE_ROWS)
        ybuf = _experts(xbuf, blk_e, n_used, w_exp_gate[l], w_exp_up[l], w_exp_down[l])
        x = _combine(ybuf, slot1, slot2, route, x1, gt_f, g_post_ffn[l], B, S).reshape(B, S, D)
    return x
```

```python
import functools
import math

import numpy as np
import jax
import jax.numpy as jnp
from jax import lax
from jax.experimental import pallas as pl
from jax.experimental.pallas import tpu as pltpu

F32 = jnp.float32
BF16 = jnp.bfloat16
HIGHEST = lax.Precision.HIGHEST

D_MODEL = 1024
HEAD_DIM = 64
LANES = 128
DIFF_HEADS = 4
NSA_HEADS = 8
NSA_KV_HEADS = 2
NSA_GROUP = NSA_HEADS // NSA_KV_HEADS
CMP_BLOCK = 32
CMP_STRIDE = 16
CMP_HIDDEN = 2 * HEAD_DIM
SLC_BLOCK = 64
SLC_TOP = 16
WINDOW = 512
N_GROUPS = 8
EXPERTS_PER_GROUP = 8
N_EXPERTS = N_GROUPS * EXPERTS_PER_GROUP
TOP_K = 2
D_EXPERT = D_MODEL // 2
NEG = -1e30
FORCE_BONUS = 1e4
EPS = 1e-6

ROW_TILE = 512
KEY_TILE = 512
DIFF_TQ = 512
NSA_TQ = 256
COL_SPLIT = 512
SOFTMAX_CHUNK_VREGS = 16
MOE_ROWS = 256
DMA_UNROLL = 8
VMEM_LIMIT = 48 * 1024 * 1024
assert ROW_TILE == KEY_TILE == WINDOW

_OFF_DQ, _OFF_DK, _OFF_DV, _OFF_NQ = 0, 512, 1024, 1536
_OFF_KV = 2048
_OFF_NG = 2816
_N_GATES = 3 * NSA_HEADS
_W_PROJ = 2048 + 768 + LANES


def _alibi_slopes(n):
    return np.array([2.0 ** (-8.0 * (i + 1) / n) for i in range(n)], dtype=np.float32)


def _dot_nt(a, b):
    return lax.dot_general(a, b, (((1,), (1,)), ((), ())), preferred_element_type=F32)


def _rms(v):
    return v * lax.rsqrt(jnp.mean(v * v, axis=-1, keepdims=True) + EPS)


def _params(sem, vmem=VMEM_LIMIT):
    return pltpu.CompilerParams(dimension_semantics=sem, vmem_limit_bytes=vmem)


LOG2E = np.float32(1.0 / math.log(2.0))
N_SLOPE_PIECES = 3
MASK_LANE = 2 * N_SLOPE_PIECES


def _pos_lanes(pos, lane):
    half = lane % HEAD_DIM
    return jnp.where(half < MASK_LANE, jnp.where(half % 2 == 0, pos // LANES, pos % LANES), 0).astype(F32)


def _slope_cols(slopes, width, at):
    cols = np.zeros((len(slopes), width), np.float32)
    rest = (np.asarray(slopes, np.float32) * LOG2E).astype(np.float32)
    for i in range(N_SLOPE_PIECES):
        piece = rest.astype(BF16).astype(np.float32)
        cols[:, at + 2 * i] = piece * LANES
        cols[:, at + 2 * i + 1] = piece
        rest = rest - piece
    cols[:, at + MASK_LANE] = NEG
    return cols


def _adaln_kernel(c_ref, w_ref, b_ref, o_ref):
    c = c_ref[...]
    s = c * jax.nn.sigmoid(c)
    o_ref[...] = jnp.dot(s, w_ref[...], preferred_element_type=F32, precision=HIGHEST) + b_ref[...]


def _adaln(c, w_ada, b_ada):
    B, D = c.shape
    rows = 8
    cp = jnp.zeros((rows, D), F32).at[:B].set(c)
    n_out = w_ada.shape[1]
    out = pl.pallas_call(
        _adaln_kernel,
        grid=(n_out // D,),
        in_specs=[pl.BlockSpec((rows, D), lambda j: (0, 0)),
                  pl.BlockSpec((D, D), lambda j: (0, j)),
                  pl.BlockSpec((1, D), lambda j: (0, j))],
        out_specs=pl.BlockSpec((rows, D), lambda j: (0, j)),
        out_shape=jax.ShapeDtypeStruct((rows, n_out), F32),
        compiler_params=_params(("parallel",)),
        name="adaln",
    )(cp, w_ada, b_ada.reshape(1, n_out))
    return out[:B]


def _inproj_kernel(x_ref, g_ref, sc_ref, sh_ref, w_ref,
                   dq_ref, dk_ref, dvt_ref, nq_ref, ckv_ref, ska_ref, svt_ref, wka_ref, wvt_ref, gt_ref):
    step = pl.program_id(1)
    tm = x_ref.shape[1]
    dh = HEAD_DIM

    @pl.when(step == 0)
    def _():
        lane = lax.broadcasted_iota(jnp.int32, (tm, LANES), 1)
        for g in range(NSA_KV_HEADS):
            wka_ref[0, g] = jnp.where(lane == (1 - g) * dh + MASK_LANE, 1.0, 0.0).astype(BF16)
        wvt_ref[...] = jnp.zeros(wvt_ref.shape, wvt_ref.dtype)

    @pl.when(step > 0)
    def _():
        h = (_rms(x_ref[0]) * g_ref[...] * (1.0 + sc_ref[0]) + sh_ref[0]).astype(BF16)

        def proj(off, width):
            return jnp.dot(h, w_ref[:, off:off + width], preferred_element_type=F32)

        lane = lax.broadcasted_iota(jnp.int32, (tm, LANES), 1)
        pos = (step - 1) * tm + lax.broadcasted_iota(jnp.int32, (tm, 1), 0)
        low = lane < dh
        pcol = _pos_lanes(pos, lane)
        wcol = _pos_lanes(pos + WINDOW, lane)

        dq_ref[0] = proj(_OFF_DQ, 512).astype(BF16)
        k = proj(_OFF_DK, 512)
        v = proj(_OFF_DV, 512)
        for hh in range(DIFF_HEADS):
            dk_ref[0, :, hh * 256:hh * 256 + LANES] = k[:, hh * LANES:(hh + 1) * LANES].astype(BF16)
            dk_ref[0, :, hh * 256 + LANES:(hh + 1) * 256] = pcol.astype(BF16)
            dvt_ref[0, hh, 0] = v[:, hh * LANES:(hh + 1) * LANES].T.astype(BF16)
        nq_ref[0] = proj(_OFF_NQ, 512).astype(BF16)
        kv = proj(_OFF_KV, 768)
        ckv_ref[0, 0] = kv[:, :LANES]
        ckv_ref[0, 1] = kv[:, LANES:2 * LANES]
        sk, sv, wk, wv = (kv[:, i * LANES:(i + 1) * LANES] for i in range(2, 6))
        onehot = jnp.where(lane == pos // SLC_BLOCK, 1.0, 0.0).astype(BF16)
        sv_t = sv.T
        wv_t = wv.T
        for g, (keys_sel, keys_win) in enumerate(((jnp.where(low, sk, pcol), jnp.where(low, wk, wcol)),
                                                  (jnp.where(low, pcol, sk), jnp.where(low, wcol, wk)))):
            ska_ref[0, g, :, :LANES] = keys_sel.astype(BF16)
            ska_ref[0, g, :, LANES:] = onehot
            svt_ref[0, g, 0] = sv_t[g * dh:(g + 1) * dh].astype(BF16)
            wka_ref[0, g] = keys_win.astype(BF16)
            for c in range(tm // LANES):
                wvt_ref[0, g, c] = wv_t[g * dh:(g + 1) * dh, c * LANES:(c + 1) * LANES].astype(BF16)
        gt_ref[0] = jax.nn.sigmoid(proj(_OFF_NG, LANES))


def _build_w_proj(w_in):
    scale = LOG2E * HEAD_DIM ** -0.5
    nq = w_in[:, _OFF_NQ:_OFF_NQ + 512].reshape(D_MODEL, NSA_KV_HEADS, NSA_GROUP, HEAD_DIM)
    nq = nq.transpose(0, 2, 1, 3).reshape(D_MODEL, 512) * scale
    cols = [w_in[:, _OFF_DQ:_OFF_DQ + 512] * scale, w_in[:, _OFF_DK:_OFF_NQ], nq,
            w_in[:, _OFF_KV:_OFF_NG + _N_GATES], jnp.zeros((D_MODEL, LANES - _N_GATES), F32)]
    return jnp.concatenate(cols, axis=1).astype(BF16)


def _in_proj(x, g_pre, sc, sh, w_proj):
    B, S, D = x.shape
    tm = ROW_TILE
    n = S // tm
    KV, dh = NSA_KV_HEADS, HEAD_DIM
    tile = lambda b, s: (b, jnp.maximum(s - 1, 0), 0)
    out_shapes = [
        ((B, S, 512), (1, tm, 512), tile, BF16),
        ((B, S, DIFF_HEADS * 256), (1, tm, DIFF_HEADS * 256), tile, BF16),
        ((B, DIFF_HEADS, n, LANES, tm), (1, DIFF_HEADS, 1, LANES, tm),
         lambda b, s: (b, 0, jnp.maximum(s - 1, 0), 0, 0), BF16),
        ((B, S, 512), (1, tm, 512), tile, BF16),
        ((B, 2, S, LANES), (1, 2, tm, LANES), lambda b, s: (b, 0, jnp.maximum(s - 1, 0), 0), F32),
        ((B, KV, S, 2 * LANES), (1, KV, tm, 2 * LANES),
         lambda b, s: (b, 0, jnp.maximum(s - 1, 0), 0), BF16),
        ((B, KV, n, dh, tm), (1, KV, 1, dh, tm),
         lambda b, s: (b, 0, jnp.maximum(s - 1, 0), 0, 0), BF16),
        ((B, KV, S + WINDOW, LANES), (1, KV, tm, LANES), lambda b, s: (b, 0, s, 0), BF16),
        ((B, KV, (S + WINDOW) // LANES, dh, LANES), (1, KV, tm // LANES, dh, LANES),
         lambda b, s: (b, 0, s, 0, 0), BF16),
        ((B, S, LANES), (1, tm, LANES), tile, F32),
    ]
    return pl.pallas_call(
        _inproj_kernel,
        grid=(B, n + 1),
        in_specs=[pl.BlockSpec((1, tm, D), tile),
                  pl.BlockSpec((1, D), lambda b, s: (0, 0)),
                  pl.BlockSpec((1, 1, D), lambda b, s: (b, 0, 0)),
                  pl.BlockSpec((1, 1, D), lambda b, s: (b, 0, 0)),
                  pl.BlockSpec((D, _W_PROJ), lambda b, s: (0, 0))],
        out_specs=[pl.BlockSpec(blk, idx) for _, blk, idx, _ in out_shapes],
        out_shape=[jax.ShapeDtypeStruct(shape, dt) for shape, _, _, dt in out_shapes],
        compiler_params=_params(("parallel", "arbitrary")),
        name="in_proj",
    )(x, g_pre.reshape(1, D), sc.reshape(B, 1, D), sh.reshape(B, 1, D), w_proj)


def _compress_kernel(x_ref, pos_ref, w1_ref, w2_ref, kca_ref, vct_ref, vc_ref):
    dh = HEAD_DIM
    n_rows = x_ref.shape[2] // CMP_STRIDE

    def mlp(t):
        first = jnp.zeros((n_rows, w1_ref.shape[3]), F32)
        second = jnp.zeros((n_rows, w1_ref.shape[3]), F32)
        for m in range(CMP_STRIDE):
            x = x_ref[0, t, pl.ds(m, n_rows, stride=CMP_STRIDE), :]
            first += jnp.dot((x + pos_ref[t, m:m + 1, :]).astype(BF16), w1_ref[t, m],
                             preferred_element_type=F32)
            second += jnp.dot((x + pos_ref[t, CMP_STRIDE + m:CMP_STRIDE + m + 1, :]).astype(BF16),
                              w1_ref[t, CMP_STRIDE + m], preferred_element_type=F32)
        hid = first + pltpu.roll(second, n_rows - 1, 0)
        return jnp.dot(jax.nn.gelu(hid).astype(BF16), w2_ref[t], preferred_element_type=F32)

    lane = lax.broadcasted_iota(jnp.int32, (n_rows, LANES), 1)
    cend = lax.broadcasted_iota(jnp.int32, (n_rows, 1), 0) * CMP_STRIDE + (CMP_BLOCK - 1)
    pcol = _pos_lanes(cend, lane)
    kc = mlp(0)
    low = lane < dh
    kca_ref[0, 0] = jnp.where(low, kc, pcol).astype(BF16)
    kca_ref[0, 1] = jnp.where(low, pcol, kc).astype(BF16)
    vc_ref[...] = mlp(1)
    vc_t = vc_ref[...].T
    for h in range(NSA_KV_HEADS):
        vct_ref[0, h] = vc_t[h * dh:(h + 1) * dh].astype(BF16)


def _block_diag(blocks):
    n, r, c = blocks.shape[-3:]
    out = jnp.einsum('...irc,ij->...irjc', blocks, jnp.eye(n, dtype=blocks.dtype))
    return out.reshape(blocks.shape[:-3] + (n * r, n * c))


def _compress(ckv, pos_k, pos_v, w1_k, w1_v, w2_k, w2_v):
    B, _, S, W = ckv.shape
    KV, dh = NSA_KV_HEADS, HEAD_DIM
    n_rows = S // CMP_STRIDE
    pos = jnp.stack([jnp.concatenate([p, p], axis=1) for p in (pos_k, pos_v)]).astype(F32)
    w1 = jnp.stack([w.reshape(CMP_BLOCK, dh, -1) for w in (w1_k, w1_v)])
    w1 = _block_diag(jnp.stack([w1, w1], axis=2)).astype(BF16)
    w2 = jnp.stack([w2_k, w2_v])
    w2 = _block_diag(jnp.stack([w2, w2], axis=1)).astype(BF16)
    return pl.pallas_call(
        _compress_kernel,
        grid=(B,),
        in_specs=[pl.BlockSpec((1, 2, S, W), lambda b: (b, 0, 0, 0)),
                  pl.BlockSpec(pos.shape, lambda b: (0, 0, 0)),
                  pl.BlockSpec(w1.shape, lambda b: (0, 0, 0, 0)),
                  pl.BlockSpec(w2.shape, lambda b: (0, 0, 0))],
        out_specs=[pl.BlockSpec((1, KV, n_rows, LANES), lambda b: (b, 0, 0, 0)),
                   pl.BlockSpec((1, KV, dh, n_rows), lambda b: (b, 0, 0, 0))],
        out_shape=[jax.ShapeDtypeStruct((B, KV, n_rows, LANES), BF16),
                   jax.ShapeDtypeStruct((B, KV, dh, n_rows), BF16)],
        scratch_shapes=[pltpu.VMEM((n_rows, W), F32)],
        compiler_params=_params(("parallel",)),
        name="nsa_compress",
    )(ckv, pos, w1, w2)


def _flash_scratch(dv, n_cols):
    tk = KEY_TILE
    return [pltpu.VMEM((tk, n_cols), F32), pltpu.VMEM((tk, n_cols), F32),
            pltpu.VMEM((tk, n_cols), BF16), pltpu.VMEM((tk, n_cols), BF16),
            pltpu.VMEM((1, n_cols), F32), pltpu.VMEM((1, n_cols), F32),
            pltpu.VMEM((1, n_cols), F32), pltpu.VMEM((1, n_cols), F32), pltpu.VMEM((dv, n_cols), F32)]


def _flash_causal(q0, tq, qa_ref, k_tile, v_tile, scratch, tiles=None):
    s_a, s_b, p_a, p_b, a_a, a_b, m_ref, l_ref, acc_ref = scratch
    tk = KEY_TILE
    n_cols = qa_ref.shape[0]
    assert tk % tq == 0
    n_full = q0 // tk
    n = n_full + 1 if tiles is None else tiles[1]
    m_ref[...] = jnp.full(m_ref.shape, NEG, F32)
    l_ref[...] = jnp.zeros(l_ref.shape, F32)
    acc_ref[...] = jnp.zeros(acc_ref.shape, F32)
    p_b[...] = jnp.zeros(p_b.shape, BF16)
    a_b[...] = jnp.ones(a_b.shape, F32)

    def key_tile(t):
        t = jnp.clip(t, 0, n - 1)
        j = jnp.where(t == 0, n_full, t - 1) if tiles is None else tiles[0][t]
        return j, pl.multiple_of(j * tk, tk)

    def scores(t, s_ref, masked):
        j, k0 = key_tile(t)
        kt = k_tile(j, k0)
        for c in range(0, n_cols, COL_SPLIT):
            s = _dot_nt(kt, qa_ref[c:c + COL_SPLIT, :])
            if masked:
                krow = lax.broadcasted_iota(jnp.int32, (tk, 1), 0)
                qcol = (lax.broadcasted_iota(jnp.int32, (1, COL_SPLIT), 1) + c) % tq
                s = jnp.where((krow + (k0 - q0)) <= qcol, s, NEG)
            s_ref[:, c:c + COL_SPLIT] = s

    def softmax(s_ref, p_ref, a_ref):
        ch = SOFTMAX_CHUNK_VREGS * 8 * LANES // n_cols
        chunks = [slice(c, c + ch) for c in range(0, tk, ch)]
        mx = None
        for rows in chunks:
            part = jnp.max(s_ref[rows, :].reshape(ch // 8, 8, n_cols), axis=0)
            mx = part if mx is None else jnp.maximum(mx, part)
        m_old = m_ref[...]
        m_new = jnp.maximum(m_old, jnp.max(mx, axis=0, keepdims=True))
        sm = None
        for rows in chunks:
            p = jnp.exp2(s_ref[rows, :] - m_new)
            p_ref[rows, :] = p.astype(BF16)
            part = jnp.sum(p.reshape(ch // 8, 8, n_cols), axis=0)
            sm = part if sm is None else sm + part
        alpha = jnp.exp2(m_old - m_new)
        a_ref[...] = alpha
        l_ref[...] = alpha * l_ref[...] + jnp.sum(sm, axis=0, keepdims=True)
        m_ref[...] = m_new

    def values(t, p_ref, a_ref):
        j, k0 = key_tile(t)
        vt = v_tile(j, k0)
        for c in range(0, n_cols, COL_SPLIT):
            cs = slice(c, c + COL_SPLIT)
            acc_ref[:, cs] = a_ref[:, cs] * acc_ref[:, cs] + jnp.dot(vt, p_ref[:, cs], preferred_element_type=F32)

    scores(0, s_a, True)

    def pair(u, carry):
        i = 2 * u
        scores(i + 1, s_b, False)
        softmax(s_a, p_a, a_a)
        values(i - 1, p_b, a_b)

        @pl.when(i + 1 < n)
        def _():
            scores(i + 2, s_a, False)
            softmax(s_b, p_b, a_b)
            values(i, p_a, a_a)

        return carry

    lax.fori_loop(0, (n + 1) // 2, pair, 0)

    @pl.when(n % 2 == 1)
    def _():
        values(n - 1, p_a, a_a)

    @pl.when(n % 2 == 0)
    def _():
        values(n - 1, p_b, a_b)

    return acc_ref, l_ref


def _softmax_cols(s, bias):
    e, inv = _softmax_cols_unnormalised(s, bias)
    return e * inv


def _softmax_cols_unnormalised(s, bias):
    s = s + bias
    m = jnp.max(s, axis=0, keepdims=True)
    e = jnp.exp2(s - m)
    inv = jnp.where(m > 0.5 * NEG, 1.0 / jnp.sum(e, axis=0, keepdims=True), 0.0)
    return e, inv


def _diff_kernel(qx_ref, lam_ref, gsub_ref, q_ref, k_ref, v_ref, o_ref, qa_ref, *scratch, lambda_init):
    tq = DIFF_TQ
    q0 = pl.program_id(2) * tq
    q = q_ref[0].astype(F32)
    low = lax.broadcasted_iota(jnp.int32, (tq, LANES), 1) < HEAD_DIM
    ext = jnp.broadcast_to(qx_ref[0], (tq, LANES)).astype(BF16)
    qa_ref[0:tq, :LANES] = jnp.where(low, q, 0.0).astype(BF16)
    qa_ref[tq:, :LANES] = jnp.where(low, 0.0, q).astype(BF16)
    qa_ref[0:tq, LANES:] = ext
    qa_ref[tq:, LANES:] = ext

    acc_ref, l_ref = _flash_causal(q0, tq, qa_ref,
                                   lambda j, k0: k_ref[0, pl.ds(k0, KEY_TILE), :],
                                   lambda j, k0: v_ref[0, 0, j],
                                   scratch)

    lv = lam_ref[...]
    lam = (jnp.exp(jnp.sum(lv[0:1] * lv[1:2], axis=-1, keepdims=True))
           - jnp.exp(jnp.sum(lv[2:3] * lv[3:4], axis=-1, keepdims=True)) + lambda_init)
    o_t = acc_ref[...] / l_ref[...]
    o_t = o_t[:, :tq] - lam * o_t[:, tq:]
    o_t = o_t * lax.rsqrt(jnp.mean(o_t * o_t, axis=0, keepdims=True) + EPS)
    o_ref[0] = (o_t.T * gsub_ref[...] * (1.0 - lambda_init)).astype(o_ref.dtype)


def _diff_attention(dq, dk, dv_t, lam_vecs, g_subln, lambda_init):
    B, S, _ = dq.shape
    tq, tk = DIFF_TQ, KEY_TILE
    H = DIFF_HEADS
    qx = jnp.asarray(_slope_cols(_alibi_slopes(H), LANES, 0)).reshape(H, 1, LANES)
    lam = jnp.zeros((8, LANES), F32).at[:4, :HEAD_DIM].set(jnp.stack(lam_vecs))
    return pl.pallas_call(
        functools.partial(_diff_kernel, lambda_init=lambda_init),
        grid=(B, H, S // tq),
        in_specs=[pl.BlockSpec((1, 1, LANES), lambda b, h, i: (h, 0, 0)),
                  pl.BlockSpec((8, LANES), lambda b, h, i: (0, 0)),
                  pl.BlockSpec((1, LANES), lambda b, h, i: (0, 0)),
                  pl.BlockSpec((1, tq, LANES), lambda b, h, i: (b, i, h)),
                  pl.BlockSpec((1, S, 2 * LANES), lambda b, h, i: (b, 0, h)),
                  pl.BlockSpec((1, 1, S // tk, LANES, tk), lambda b, h, i: (b, h, 0, 0, 0))],
        out_specs=pl.BlockSpec((1, tq, LANES), lambda b, h, i: (b, i, h)),
        out_shape=jax.ShapeDtypeStruct((B, S, H * LANES), BF16),
        scratch_shapes=[pltpu.VMEM((2 * tq, 2 * LANES), BF16)] + _flash_scratch(LANES, 2 * tq),
        compiler_params=_params(("parallel", "parallel", "parallel")),
        name="diff_attention",
    )(qx, lam, g_subln.reshape(1, LANES).astype(F32), dq, dk, dv_t)


def _nsa_kernel(qx_ref, q_ref, kc_ref, vc_ref, ka_ref, sv_ref, wk_ref, wv_ref, g_ref, ov_ref, cb_ref, wb_ref,
                o_ref, qa_ref, out_ref, tiles_ref, *scratch):
    tq, G, dh = NSA_TQ, NSA_GROUP, HEAD_DIM
    n_cmp = kc_ref.shape[2]
    n_win = WINDOW + tq
    grp = pl.program_id(1)
    qi = pl.program_id(2)
    q0 = pl.multiple_of(qi * tq, tq)
    cols = [slice(i * tq, (i + 1) * tq) for i in range(G)]
    g_t = g_ref[0].T

    def gate(i, branch):
        a = 3 * i + branch
        b = 3 * (G + i) + branch
        return jnp.where(grp == 0, g_t[a:a + 1, :], g_t[b:b + 1, :])

    lane_q = lax.broadcasted_iota(jnp.int32, (tq, LANES), 1)
    mine = (lane_q >= grp * dh) & (lane_q < (grp + 1) * dh)
    qx = qx_ref[0]
    for i in range(G):
        q = q_ref[0, :, i * LANES:(i + 1) * LANES].astype(F32)
        qa_ref[cols[i], :LANES] = jnp.where(mine, q, jnp.broadcast_to(qx[i:i + 1], (tq, LANES))).astype(BF16)
    qs = qa_ref[:, :LANES]

    kw = wk_ref[0, 0, pl.ds(q0, n_win), :]
    vw = jnp.concatenate([wv_ref[0, 0, (tq // LANES) * qi + c] for c in range(n_win // LANES)], axis=1)
    e, inv = _softmax_cols_unnormalised(_dot_nt(kw, qs), jnp.tile(wb_ref[...], (1, G)))
    ow = jnp.dot(vw, e.astype(BF16), preferred_element_type=F32) * inv

    cbias = cb_ref[pl.ds(pl.multiple_of(n_cmp - (tq // CMP_STRIDE) * qi, 8), n_cmp), :]
    p = _softmax_cols(_dot_nt(kc_ref[0, 0], qs), jnp.tile(cbias, (1, G)))
    oc = jnp.dot(vc_ref[0, 0], p.astype(BF16), preferred_element_type=F32)
    for i in range(G):
        out_ref[i * dh:(i + 1) * dh, :] = gate(i, 0) * oc[:, cols[i]] + gate(i, 2) * ow[:, cols[i]]

    psum = p[:, cols[0]]
    for i in range(1, G):
        psum = psum + p[:, cols[i]]
    hi = psum.astype(BF16)
    r1 = psum - hi.astype(F32)
    mid = r1.astype(BF16)
    lo = (r1 - mid.astype(F32)).astype(BF16)
    ov = ov_ref[...]
    imp = (jnp.dot(ov, hi, preferred_element_type=F32) + jnp.dot(ov, mid, preferred_element_type=F32)
           + jnp.dot(ov, lo, preferred_element_type=F32))
    n_slc = imp.shape[0]
    blk = lax.broadcasted_iota(jnp.int32, (n_slc, tq), 0)
    cur = (lax.broadcasted_iota(jnp.int32, (1, tq), 1) + q0) // SLC_BLOCK
    forced = (blk == 0) | (blk == cur) | (blk == cur - 1)
    score = jnp.where(forced, -jnp.inf, jnp.where(blk <= cur, imp, NEG))
    blkf = blk.astype(F32)
    sel = jnp.where(forced, 1.0, 0.0)
    for _ in range(SLC_TOP - 3):
        mx = jnp.max(score, axis=0, keepdims=True)
        idx = jnp.min(jnp.where(score == mx, blkf, float(n_slc)), axis=0, keepdims=True)
        hit = blkf == idx
        sel = jnp.where(hit, 1.0, sel)
        score = jnp.where(hit, -jnp.inf, score)
    sel_t = sel.T
    blockmask = ((sel_t - 1.0) * 1e30).astype(BF16)
    for i in range(G):
        qa_ref[cols[i], LANES:] = blockmask

    per_tile = KEY_TILE // SLC_BLOCK
    n_tiles = n_slc // per_tile
    used = jnp.max(jnp.max(sel, axis=1, keepdims=True).reshape(n_tiles, per_tile, 1), axis=1)
    bit = jnp.where(used > 0.0, 1 << lax.broadcasted_iota(jnp.int32, (n_tiles, 1), 0), 0)
    bits = jnp.sum(bit, axis=0, keepdims=True)[0, 0]
    n_full = q0 // KEY_TILE
    tiles_ref[0] = n_full
    count = jnp.int32(1)
    for j in range(n_tiles):
        tiles_ref[count] = j
        count = count + jnp.where(j < n_full, (bits >> j) & 1, 0)

    acc_ref, l_ref = _flash_causal(q0, tq, qa_ref,
                                   lambda j, k0: ka_ref[0, 0, pl.ds(k0, KEY_TILE), :],
                                   lambda j, k0: sv_ref[0, 0, j],
                                   scratch, tiles=(tiles_ref, count))
    os_ = acc_ref[...] / l_ref[...]
    for i in range(G):
        out_ref[i * dh:(i + 1) * dh, :] += gate(i, 1) * os_[:, cols[i]]

    o_ref[0] = out_ref[...].T.astype(o_ref.dtype)


def _nsa_attention(nq, kca, vc_t, ska, sv_t, wka, wv_t, gates):
    B, S, _ = nq.shape
    tq, tk, G, dh = NSA_TQ, KEY_TILE, NSA_GROUP, HEAD_DIM
    KV = NSA_KV_HEADS
    n_cmp = kca.shape[2]
    n_slc = S // SLC_BLOCK
    Sp = S + WINDOW
    slopes = _alibi_slopes(NSA_HEADS)
    qx = jnp.asarray(np.stack([_slope_cols(slopes[:G], LANES, dh), _slope_cols(slopes[G:], LANES, 0)]))
    overlap_t = _overlap_matrix(S).T
    r = np.arange(tq)[None, :]
    j = np.arange(2 * n_cmp)[:, None]
    cmp_band = jnp.asarray(np.where(CMP_STRIDE * (j - n_cmp) + CMP_BLOCK - 1 <= r, 0.0, NEG).astype(np.float32))
    c = np.arange(WINDOW + tq)[:, None]
    d = r - c + WINDOW
    win_band = jnp.asarray(np.where((d >= 0) & (d < WINDOW), 0.0, NEG).astype(np.float32))
    const2 = lambda b, g, i: (0, 0)
    per_bg = lambda b, g, i: (b, g, 0, 0)
    per_bg5 = lambda b, g, i: (b, g, 0, 0, 0)
    return pl.pallas_call(
        _nsa_kernel,
        grid=(B, KV, S // tq),
        in_specs=[pl.BlockSpec((1, G, LANES), lambda b, g, i: (g, 0, 0)),
                  pl.BlockSpec((1, tq, G * LANES), lambda b, g, i: (b, i, 0)),
                  pl.BlockSpec((1, 1, n_cmp, LANES), per_bg),
                  pl.BlockSpec((1, 1, dh, n_cmp), per_bg),
                  pl.BlockSpec((1, 1, S, 2 * LANES), per_bg),
                  pl.BlockSpec((1, 1, S // tk, dh, tk), per_bg5),
                  pl.BlockSpec((1, 1, Sp, LANES), per_bg),
                  pl.BlockSpec((1, 1, Sp // LANES, dh, LANES), per_bg5),
                  pl.BlockSpec((1, tq, LANES), lambda b, g, i: (b, i, 0)),
                  pl.BlockSpec((n_slc, n_cmp), const2),
                  pl.BlockSpec(cmp_band.shape, const2), pl.BlockSpec(win_band.shape, const2)],
        out_specs=pl.BlockSpec((1, tq, G * dh), lambda b, g, i: (b, i, g)),
        out_shape=jax.ShapeDtypeStruct((B, S, NSA_HEADS * dh), BF16),
        scratch_shapes=[pltpu.VMEM((G * tq, 2 * LANES), BF16),
                        pltpu.VMEM((G * dh, tq), F32),
                        pltpu.SMEM((2 * (S // tk),), jnp.int32)]
        + _flash_scratch(dh, G * tq),
        compiler_params=_params(("parallel", "parallel", "parallel")),
        name="nsa_attention",
    )(qx, nq, kca, vc_t, ska, sv_t, wka, wv_t, gates, overlap_t, cmp_band, win_band)


def _outproj_kernel(oa_ref, ob_ref, x_ref, wo_ref, gpm_ref, gta_ref, gpf_ref, sc_ref, sh_ref, wr_ref, br_ref,
                    x1_ref, h2_ref, rt_ref):
    half = oa_ref.shape[1]
    mix = (jnp.dot(oa_ref[...], wo_ref[:half], preferred_element_type=F32)
           + jnp.dot(ob_ref[...], wo_ref[half:], preferred_element_type=F32))
    x1 = x_ref[...] + gta_ref[0] * (_rms(mix) * gpm_ref[...])
    x1_ref[...] = x1
    h2 = _rms(x1) * gpf_ref[...] * (1.0 + sc_ref[0]) + sh_ref[0]
    h2_ref[...] = h2
    h_hi = h2.astype(BF16)
    h_lo = (h2 - h_hi.astype(F32)).astype(BF16)
    logits = (jnp.dot(h_hi, wr_ref[0], preferred_element_type=F32)
              + jnp.dot(h_lo, wr_ref[0], preferred_element_type=F32)
              + jnp.dot(h_hi, wr_ref[1], preferred_element_type=F32)) + br_ref[...]
    tm = logits.shape[0]
    lane = lax.broadcasted_iota(jnp.int32, (tm, LANES), 1)
    lanef = lane.astype(F32)
    is_grp = (lane >= N_EXPERTS) & (lane < N_EXPERTS + N_GROUPS)
    lg = jnp.where(is_grp, logits, -jnp.inf)
    mg = jnp.max(lg, axis=-1, keepdims=True)
    g_lane = jnp.min(jnp.where(lg == mg, lanef, 1e9), axis=-1, keepdims=True)
    p_grp = 1.0 / jnp.sum(jnp.where(is_grp, jnp.exp(lg - mg), 0.0), axis=-1, keepdims=True)
    grp = g_lane.astype(jnp.int32) - N_EXPERTS
    in_grp = (lane < N_EXPERTS) & ((lane // EXPERTS_PER_GROUP) == grp)
    le = jnp.where(in_grp, logits, -jnp.inf)
    v1 = jnp.max(le, axis=-1, keepdims=True)
    i1 = jnp.min(jnp.where(le == v1, lanef, 1e9), axis=-1, keepdims=True)
    le2 = jnp.where(lanef == i1, -jnp.inf, le)
    v2 = jnp.max(le2, axis=-1, keepdims=True)
    i2 = jnp.min(jnp.where(le2 == v2, lanef, 1e9), axis=-1, keepdims=True)
    e2 = jnp.exp(v2 - v1)
    w1 = p_grp / (1.0 + e2)
    w2 = p_grp * e2 / (1.0 + e2)
    rt_ref[...] = jnp.where(lane == 0, i1, jnp.where(lane == 1, i2, jnp.where(lane == 2, w1,
                            jnp.where(lane == 3, w2, 0.0))))


def _out_proj(o_a, o_b, x2, w_out, g_post_mix, gt_a, g_pre_ffn, sc_f, sh_f, w_r, b_r, B, S):
    T, D = x2.shape
    tm = ROW_TILE
    per_b = S // tm
    row = lambda i: (i, 0)
    bat = lambda i: (i // per_b, 0, 0)
    const = lambda i: (0, 0)
    return pl.pallas_call(
        _outproj_kernel,
        grid=(T // tm,),
        in_specs=[pl.BlockSpec((tm, o_a.shape[1]), row), pl.BlockSpec((tm, o_b.shape[1]), row),
                  pl.BlockSpec((tm, D), row), pl.BlockSpec((D, D), const),
                  pl.BlockSpec((1, D), const), pl.BlockSpec((1, 1, D), bat), pl.BlockSpec((1, D), const),
                  pl.BlockSpec((1, 1, D), bat), pl.BlockSpec((1, 1, D), bat),
                  pl.BlockSpec((2, D, LANES), lambda i: (0, 0, 0)), pl.BlockSpec((1, LANES), const)],
        out_specs=[pl.BlockSpec((tm, D), row), pl.BlockSpec((tm, D), row), pl.BlockSpec((tm, LANES), row)],
        out_shape=[jax.ShapeDtypeStruct((T, D), F32), jax.ShapeDtypeStruct((T, D), F32),
                   jax.ShapeDtypeStruct((T, LANES), F32)],
        compiler_params=_params(("parallel",)),
        name="out_proj_router",
    )(o_a, o_b, x2, w_out, g_post_mix.reshape(1, D), gt_a.reshape(B, 1, D), g_pre_ffn.reshape(1, D),
      sc_f.reshape(B, 1, D), sh_f.reshape(B, 1, D), w_r, b_r)


def _slots_kernel(rt_ref, slot_ref, blk_ref, cnt_ref, base_ref, *, n_blk_rows):
    phase = pl.program_id(0)
    step = pl.program_id(1)
    tm = rt_ref.shape[0]
    rt = rt_ref[...]
    lane = lax.broadcasted_iota(jnp.int32, (tm, LANES), 1).astype(F32)
    i1 = rt[:, 0:1]
    i2 = rt[:, 1:2]
    hit1 = lane == i1
    hit2 = lane == i2
    onehot = jnp.where(hit1, 1.0, jnp.where(hit2, 1.0, 0.0))
    colsum = jnp.sum(onehot, axis=0, keepdims=True)

    @pl.when((phase == 0) & (step == 0))
    def _():
        cnt_ref[...] = jnp.zeros(cnt_ref.shape, F32)

    @pl.when(phase == 0)
    def _():
        cnt_ref[...] += colsum

    @pl.when((phase == 1) & (step == 0))
    def _():
        cnt = cnt_ref[...]
        padded = jnp.floor((cnt + (MOE_ROWS - 1)) / MOE_ROWS) * MOE_ROWS
        lane1 = lax.broadcasted_iota(jnp.int32, (1, LANES), 1)
        incl = padded
        sh = 1
        while sh < LANES:
            incl = incl + jnp.where(lane1 >= sh, pltpu.roll(incl, sh, 1), 0.0)
            sh *= 2
        base_ref[...] = incl - padded
        rows = lax.broadcasted_iota(jnp.int32, (n_blk_rows, LANES), 0).astype(F32) * MOE_ROWS
        lane2 = lax.broadcasted_iota(jnp.int32, (n_blk_rows, LANES), 1)
        ended = jnp.where(lane2 < N_EXPERTS, jnp.where(incl <= rows, 1.0, 0.0), 0.0)
        blk_e = jnp.minimum(jnp.sum(ended, axis=-1, keepdims=True), N_EXPERTS - 1.0)
        total = jnp.sum(jnp.where(lane1 == N_EXPERTS - 1, incl, 0.0), axis=-1, keepdims=True)
        n_used = total / MOE_ROWS
        blk_ref[...] = jnp.where(lane2 == 0, blk_e, jnp.where(lane2 == 1, n_used, 0.0)).astype(jnp.int32)

    @pl.when(phase == 1)
    def _():
        r = lax.broadcasted_iota(jnp.int32, (tm, tm), 0)
        c = lax.broadcasted_iota(jnp.int32, (tm, tm), 1)
        tri = jnp.where(c < r, 1.0, 0.0).astype(BF16)
        before = jnp.dot(tri, onehot.astype(BF16), preferred_element_type=F32) + base_ref[...]
        s1 = jnp.sum(jnp.where(hit1, before, 0.0), axis=-1, keepdims=True)
        s2 = jnp.sum(jnp.where(hit2, before, 0.0), axis=-1, keepdims=True)
        lane_i = lax.broadcasted_iota(jnp.int32, (tm, LANES), 1)
        slot_ref[...] = jnp.where(lane_i == 0, s1, jnp.where(lane_i == 1, s2, 0.0)).astype(jnp.int32)
        base_ref[...] += colsum


def _slots(route, n_blocks):
    T = route.shape[0]
    tm = ROW_TILE
    n_blk_rows = -(-n_blocks // 8) * 8
    slot, blk = pl.pallas_call(
        functools.partial(_slots_kernel, n_blk_rows=n_blk_rows),
        grid=(2, T // tm),
        in_specs=[pl.BlockSpec((tm, LANES), lambda p, i: (i, 0))],
        out_specs=[pl.BlockSpec((tm, LANES), lambda p, i: (i * p, 0)),
                   pl.BlockSpec((n_blk_rows, LANES), lambda p, i: (0, 0))],
        out_shape=[jax.ShapeDtypeStruct((T, LANES), jnp.int32),
                   jax.ShapeDtypeStruct((n_blk_rows, LANES), jnp.int32)],
        scratch_shapes=[pltpu.VMEM((1, LANES), F32), pltpu.VMEM((1, LANES), F32)],
        compiler_params=_params(("arbitrary", "arbitrary")),
        name="moe_slots",
    )(route)
    return slot[:, 0], slot[:, 1], blk[:n_blocks, 0], blk[0:1, 1]


def _dispatch_kernel(s1_ref, s2_ref, h_ref, xz_ref, xb_ref, sem):
    del xz_ref
    tm = h_ref.shape[0]
    base = pl.program_id(0) * tm

    def issue(r, carry):
        src = h_ref.at[pl.ds(r, 1), :]
        pltpu.make_async_copy(src, xb_ref.at[pl.ds(s1_ref[base + r], 1), :], sem).start()
        pltpu.make_async_copy(src, xb_ref.at[pl.ds(s2_ref[base + r], 1), :], sem).start()
        return carry

    lax.fori_loop(0, tm, issue, 0, unroll=DMA_UNROLL)

    def drain(r, carry):
        for _ in range(TOP_K):
            pltpu.make_async_copy(h_ref.at[pl.ds(0, 1), :], xb_ref.at[pl.ds(0, 1), :], sem).wait()
        return carry

    lax.fori_loop(0, tm, drain, 0, unroll=DMA_UNROLL)


def _dispatch(h2, slot1, slot2, n_rows):
    T, D = h2.shape
    tm = ROW_TILE
    xz = jnp.zeros((n_rows, D), h2.dtype)
    return pl.pallas_call(
        _dispatch_kernel,
        grid_spec=pltpu.PrefetchScalarGridSpec(
            num_scalar_prefetch=2,
            grid=(T // tm,),
            in_specs=[pl.BlockSpec((tm, D), lambda i, a, b: (i, 0)),
                      pl.BlockSpec(memory_space=pl.ANY)],
            out_specs=pl.BlockSpec(memory_space=pl.ANY),
            scratch_shapes=[pltpu.SemaphoreType.DMA(())]),
        out_shape=jax.ShapeDtypeStruct((n_rows, D), h2.dtype),
        input_output_aliases={3: 0},
        compiler_params=_params(("arbitrary",)),
        name="moe_dispatch",
    )(slot1, slot2, h2, xz)


def _expert_kernel(be_ref, nu_ref, x_ref, wg_ref, wu_ref, wd_ref, y_ref, wgb_ref, wub_ref, wdb_ref):
    i = pl.program_id(0)

    @pl.when(i >= nu_ref[0])
    def _():
        y_ref[...] = jnp.zeros(y_ref.shape, y_ref.dtype)

    @pl.when(i < nu_ref[0])
    def _():
        changed = be_ref[i] != be_ref[jnp.maximum(i - 1, 0)]

        @pl.when((i == 0) | changed)
        def _():
            wgb_ref[...] = wg_ref[0].astype(BF16)
            wub_ref[...] = wu_ref[0].astype(BF16)
            wdb_ref[...] = wd_ref[0].astype(BF16)

        x = x_ref[...].astype(BF16)
        g = jnp.dot(x, wgb_ref[...], preferred_element_type=F32)
        u = jnp.dot(x, wub_ref[...], preferred_element_type=F32)
        hid = (g * jax.nn.sigmoid(g) * u).astype(BF16)
        y_ref[...] = jnp.dot(hid, wdb_ref[...], preferred_element_type=F32).astype(y_ref.dtype)


def _experts(xbuf, blk_e, n_used, w_gate, w_up, w_down):
    R, D = xbuf.shape
    E, _, De = w_gate.shape
    n_blocks = R // MOE_ROWS
    blk = lambda i, be, nu: (jnp.minimum(i, nu[0] - 1), 0)
    wsel = lambda i, be, nu: (be[jnp.minimum(i, nu[0] - 1)], 0, 0)
    return pl.pallas_call(
        _expert_kernel,
        grid_spec=pltpu.PrefetchScalarGridSpec(
            num_scalar_prefetch=2,
            grid=(n_blocks,),
            in_specs=[pl.BlockSpec((MOE_ROWS, D), blk),
                      pl.BlockSpec((1, D, De), wsel), pl.BlockSpec((1, D, De), wsel),
                      pl.BlockSpec((1, De, D), wsel)],
            out_specs=pl.BlockSpec((MOE_ROWS, D), lambda i, be, nu: (i, 0)),
            scratch_shapes=[pltpu.VMEM((D, De), BF16), pltpu.VMEM((D, De), BF16), pltpu.VMEM((De, D), BF16)]),
        out_shape=jax.ShapeDtypeStruct((R, D), F32),
        compiler_params=_params(("arbitrary",)),
        name="moe_experts",
    )(blk_e, n_used, xbuf, w_gate, w_up, w_down)


def _combine_kernel(s1_ref, s2_ref, yb_ref, rt_ref, x1_ref, gtf_ref, gpo_ref, o_ref, ya_ref, yc_ref, sem):
    tm = x1_ref.shape[0]
    base = pl.program_id(0) * tm

    def issue(r, carry):
        pltpu.make_async_copy(yb_ref.at[pl.ds(s1_ref[base + r], 1), :], ya_ref.at[pl.ds(r, 1), :], sem).start()
        pltpu.make_async_copy(yb_ref.at[pl.ds(s2_ref[base + r], 1), :], yc_ref.at[pl.ds(r, 1), :], sem).start()
        return carry

    lax.fori_loop(0, tm, issue, 0, unroll=DMA_UNROLL)

    def drain(r, carry):
        for _ in range(TOP_K):
            pltpu.make_async_copy(yb_ref.at[pl.ds(0, 1), :], ya_ref.at[pl.ds(0, 1), :], sem).wait()
        return carry

    lax.fori_loop(0, tm, drain, 0, unroll=DMA_UNROLL)
    rt = rt_ref[...]
    f = rt[:, 2:3] * ya_ref[...] + rt[:, 3:4] * yc_ref[...]
    o_ref[...] = x1_ref[...] + gtf_ref[0] * (_rms(f) * gpo_ref[...])


def _combine(ybuf, slot1, slot2, route, x1, gt_f, g_post_ffn, B, S):
    T, D = x1.shape
    tm = ROW_TILE
    per_b = S // tm
    row = lambda i, a, b: (i, 0)
    return pl.pallas_call(
        _combine_kernel,
        grid_spec=pltpu.PrefetchScalarGridSpec(
            num_scalar_prefetch=2,
            grid=(T // tm,),
            in_specs=[pl.BlockSpec(memory_space=pl.ANY),
                      pl.BlockSpec((tm, LANES), row), pl.BlockSpec((tm, D), row),
                      pl.BlockSpec((1, 1, D), lambda i, a, b: (i // per_b, 0, 0)),
                      pl.BlockSpec((1, D), lambda i, a, b: (0, 0))],
            out_specs=pl.BlockSpec((tm, D), row),
            scratch_shapes=[pltpu.VMEM((tm, D), F32), pltpu.VMEM((tm, D), F32), pltpu.SemaphoreType.DMA(())]),
        out_shape=jax.ShapeDtypeStruct((T, D), F32),
        compiler_params=_params(("arbitrary",)),
        name="moe_combine",
    )(slot1, slot2, ybuf, route, x1, gt_f.reshape(B, 1, D), g_post_ffn.reshape(1, D))


def _overlap_matrix(S):
    n_rows = S // CMP_STRIDE
    n_cmp = (S - CMP_BLOCK) // CMP_STRIDE + 1
    n_slc = S // SLC_BLOCK
    cstart = np.arange(n_rows) * CMP_STRIDE
    sstart = np.arange(n_slc) * SLC_BLOCK
    ov = (cstart[:, None] < sstart[None, :] + SLC_BLOCK) & (cstart[:, None] + CMP_BLOCK > sstart[None, :])
    ov &= (np.arange(n_rows) < n_cmp)[:, None]
    return jnp.asarray(ov.astype(np.float32), dtype=BF16)


def kernel(x, c, w_ada, b_ada, g_pre_mix, g_post_mix, g_pre_ffn, g_post_ffn, w_in, lam_q1, lam_k1, lam_q2,
           lam_k2, g_subln, cmp_pos_k, cmp_pos_v, cmp_w1_k, cmp_w1_v, cmp_w2_k, cmp_w2_v, w_out,
           w_router_grp, b_router_grp, w_router_exp, b_router_exp, w_exp_gate, w_exp_up, w_exp_down):
    B, S, D = x.shape
    T = B * S
    depth = w_ada.shape[0]
    assert S // SLC_BLOCK == LANES and D == D_MODEL
    n_blocks = (T * TOP_K) // MOE_ROWS + N_EXPERTS
    for l in range(depth):
        lambda_init = 0.8 - 0.6 * math.exp(-0.3 * l)
        mod = _adaln(c, w_ada[l], b_ada[l])
        sh_a, sc_a, gt_a, sh_f, sc_f, gt_f = jnp.split(mod, 6, axis=-1)

        dq, dk, dv_t, nq, ckv, ska, sv_t, wka, wv_t, gates = _in_proj(x, g_pre_mix[l], sc_a, sh_a,
                                                                      _build_w_proj(w_in[l]))
        kca, vc_t = _compress(ckv, cmp_pos_k[l], cmp_pos_v[l], cmp_w1_k[l], cmp_w1_v[l], cmp_w2_k[l], cmp_w2_v[l])
        o_a = _diff_attention(dq, dk, dv_t, (lam_q1[l], lam_k1[l], lam_q2[l], lam_k2[l]), g_subln[l], lambda_init)
        o_b = _nsa_attention(nq, kca, vc_t, ska, sv_t, wka, wv_t, gates)

        w_r = jnp.zeros((D, LANES), F32).at[:, :N_EXPERTS].set(w_router_exp[l]).at[
            :, N_EXPERTS:N_EXPERTS + N_GROUPS].set(w_router_grp[l])
        b_r = jnp.zeros((1, LANES), F32).at[0, :N_EXPERTS].set(b_router_exp[l]).at[
            0, N_EXPERTS:N_EXPERTS + N_GROUPS].set(b_router_grp[l])
        w_r_hi = w_r.astype(BF16)
        w_r = jnp.stack([w_r_hi, (w_r - w_r_hi.astype(F32)).astype(BF16)])
        x1, h2, route = _out_proj(o_a.reshape(T, -1), o_b.reshape(T, -1), x.reshape(T, D), w_out[l].astype(BF16),
                                  g_post_mix[l], gt_a, g_pre_ffn[l], sc_f, sh_f, w_r, b_r, B, S)
        slot1, slot2, blk_e, n_used = _slots(route, n_blocks)
        xbuf = _dispatch(h2, slot1, slot2, n_blocks * MOE_ROWS)
        ybuf = _experts(xbuf, blk_e, n_used, w_exp_gate[l], w_exp_up[l], w_exp_down[l])
        x = _combine(ybuf, slot1, slot2, route, x1, gt_f, g_post_ffn[l], B, S).reshape(B, S, D)
    return x
```

```python
import functools
import math

import numpy as np
import jax
import jax.numpy as jnp
from jax import lax
from jax.experimental import pallas as pl
from jax.experimental.pallas import tpu as pltpu

F32 = jnp.float32
BF16 = jnp.bfloat16
HIGHEST = lax.Precision.HIGHEST

D_MODEL = 1024
HEAD_DIM = 64
LANES = 128
DIFF_HEADS = 4
NSA_HEADS = 8
NSA_KV_HEADS = 2
NSA_GROUP = NSA_HEADS // NSA_KV_HEADS
CMP_BLOCK = 32
CMP_STRIDE = 16
CMP_HIDDEN = 2 * HEAD_DIM
SLC_BLOCK = 64
SLC_TOP = 16
WINDOW = 512
N_GROUPS = 8
EXPERTS_PER_GROUP = 8
N_EXPERTS = N_GROUPS * EXPERTS_PER_GROUP
TOP_K = 2
D_EXPERT = D_MODEL // 2
NEG = -1e30
FORCE_BONUS = 1e4
EPS = 1e-6

ROW_TILE = 512
KEY_TILE = 512
DIFF_TQ = 512
NSA_TQ = 256
COL_SPLIT = 512
SOFTMAX_CHUNK_VREGS = 16
MOE_ROWS = 256
DMA_UNROLL = 8
VMEM_LIMIT = 48 * 1024 * 1024
assert ROW_TILE == KEY_TILE == WINDOW

_OFF_DQ, _OFF_DK, _OFF_DV, _OFF_NQ = 0, 512, 1024, 1536
_OFF_KV = 2048
_OFF_NG = 2816
_N_GATES = 3 * NSA_HEADS
_W_PROJ = 2048 + 768 + LANES


def _alibi_slopes(n):
    return np.array([2.0 ** (-8.0 * (i + 1) / n) for i in range(n)], dtype=np.float32)


def _dot_nt(a, b):
    return lax.dot_general(a, b, (((1,), (1,)), ((), ())), preferred_element_type=F32)


def _rms(v):
    return v * lax.rsqrt(jnp.mean(v * v, axis=-1, keepdims=True) + EPS)


def _params(sem, vmem=VMEM_LIMIT):
    return pltpu.CompilerParams(dimension_semantics=sem, vmem_limit_bytes=vmem)


LOG2E = np.float32(1.0 / math.log(2.0))
N_SLOPE_PIECES = 3
MASK_LANE = 2 * N_SLOPE_PIECES


def _pos_lanes(pos, lane):
    half = lane % HEAD_DIM
    return jnp.where(half < MASK_LANE, jnp.where(half % 2 == 0, pos // LANES, pos % LANES), 0).astype(F32)


def _slope_cols(slopes, width, at):
    cols = np.zeros((len(slopes), width), np.float32)
    rest = (np.asarray(slopes, np.float32) * LOG2E).astype(np.float32)
    for i in range(N_SLOPE_PIECES):
        piece = rest.astype(BF16).astype(np.float32)
        cols[:, at + 2 * i] = piece * LANES
        cols[:, at + 2 * i + 1] = piece
        rest = rest - piece
    cols[:, at + MASK_LANE] = NEG
    return cols


def _adaln_kernel(c_ref, w_ref, b_ref, o_ref):
    c = c_ref[...]
    s = c * jax.nn.sigmoid(c)
    o_ref[...] = jnp.dot(s, w_ref[...], preferred_element_type=F32, precision=HIGHEST) + b_ref[...]


def _adaln(c, w_ada, b_ada):
    B, D = c.shape
    rows = 8
    cp = jnp.zeros((rows, D), F32).at[:B].set(c)
    n_out = w_ada.shape[1]
    out = pl.pallas_call(
        _adaln_kernel,
        grid=(n_out // D,),
        in_specs=[pl.BlockSpec((rows, D), lambda j: (0, 0)),
                  pl.BlockSpec((D, D), lambda j: (0, j)),
                  pl.BlockSpec((1, D), lambda j: (0, j))],
        out_specs=pl.BlockSpec((rows, D), lambda j: (0, j)),
        out_shape=jax.ShapeDtypeStruct((rows, n_out), F32),
        compiler_params=_params(("parallel",)),
        name="adaln",
    )(cp, w_ada, b_ada.reshape(1, n_out))
    return out[:B]


def _inproj_kernel(x_ref, g_ref, sc_ref, sh_ref, w_ref,
                   dq_ref, dk_ref, dvt_ref, nq_ref, ckv_ref, ska_ref, svt_ref, wka_ref, wvt_ref, gt_ref):
    step = pl.program_id(1)
    tm = x_ref.shape[1]
    dh = HEAD_DIM

    @pl.when(step == 0)
    def _():
        lane = lax.broadcasted_iota(jnp.int32, (tm, LANES), 1)
        for g in range(NSA_KV_HEADS):
            wka_ref[0, g] = jnp.where(lane == (1 - g) * dh + MASK_LANE, 1.0, 0.0).astype(BF16)
        wvt_ref[...] = jnp.zeros(wvt_ref.shape, wvt_ref.dtype)

    @pl.when(step > 0)
    def _():
        h = (_rms(x_ref[0]) * g_ref[...] * (1.0 + sc_ref[0]) + sh_ref[0]).astype(BF16)

        def proj(off, width):
            return jnp.dot(h, w_ref[:, off:off + width], preferred_element_type=F32)

        lane = lax.broadcasted_iota(jnp.int32, (tm, LANES), 1)
        pos = (step - 1) * tm + lax.broadcasted_iota(jnp.int32, (tm, 1), 0)
        low = lane < dh
        pcol = _pos_lanes(pos, lane)
        wcol = _pos_lanes(pos + WINDOW, lane)

        dq_ref[0] = proj(_OFF_DQ, 512).astype(BF16)
        k = proj(_OFF_DK, 512)
        v = proj(_OFF_DV, 512)
        for hh in range(DIFF_HEADS):
            dk_ref[0, :, hh * 256:hh * 256 + LANES] = k[:, hh * LANES:(hh + 1) * LANES].astype(BF16)
            dk_ref[0, :, hh * 256 + LANES:(hh + 1) * 256] = pcol.astype(BF16)
            dvt_ref[0, hh, 0] = v[:, hh * LANES:(hh + 1) * LANES].T.astype(BF16)
        nq_ref[0] = proj(_OFF_NQ, 512).astype(BF16)
        kv = proj(_OFF_KV, 768)
        ckv_ref[0, 0] = kv[:, :LANES]
        ckv_ref[0, 1] = kv[:, LANES:2 * LANES]
        sk, sv, wk, wv = (kv[:, i * LANES:(i + 1) * LANES] for i in range(2, 6))
        onehot = jnp.where(lane == pos // SLC_BLOCK, 1.0, 0.0).astype(BF16)
        sv_t = sv.T
        wv_t = wv.T
        for g, (keys_sel, keys_win) in enumerate(((jnp.where(low, sk, pcol), jnp.where(low, wk, wcol)),
                                                  (jnp.where(low, pcol, sk), jnp.where(low, wcol, wk)))):
            ska_ref[0, g, :, :LANES] = keys_sel.astype(BF16)
            ska_ref[0, g, :, LANES:] = onehot
            svt_ref[0, g, 0] = sv_t[g * dh:(g + 1) * dh].astype(BF16)
            wka_ref[0, g] = keys_win.astype(BF16)
            for c in range(tm // LANES):
                wvt_ref[0, g, c] = wv_t[g * dh:(g + 1) * dh, c * LANES:(c + 1) * LANES].astype(BF16)
        gt_ref[0] = jax.nn.sigmoid(proj(_OFF_NG, LANES))


def _build_w_proj(w_in):
    scale = LOG2E * HEAD_DIM ** -0.5
    nq = w_in[:, _OFF_NQ:_OFF_NQ + 512].reshape(D_MODEL, NSA_KV_HEADS, NSA_GROUP, HEAD_DIM)
    nq = nq.transpose(0, 2, 1, 3).reshape(D_MODEL, 512) * scale
    cols = [w_in[:, _OFF_DQ:_OFF_DQ + 512] * scale, w_in[:, _OFF_DK:_OFF_NQ], nq,
            w_in[:, _OFF_KV:_OFF_NG + _N_GATES], jnp.zeros((D_MODEL, LANES - _N_GATES), F32)]
    return jnp.concatenate(cols, axis=1).astype(BF16)


def _in_proj(x, g_pre, sc, sh, w_proj):
    B, S, D = x.shape
    tm = ROW_TILE
    n = S // tm
    KV, dh = NSA_KV_HEADS, HEAD_DIM
    tile = lambda b, s: (b, jnp.maximum(s - 1, 0), 0)
    out_shapes = [
        ((B, S, 512), (1, tm, 512), tile, BF16),
        ((B, S, DIFF_HEADS * 256), (1, tm, DIFF_HEADS * 256), tile, BF16),
        ((B, DIFF_HEADS, n, LANES, tm), (1, DIFF_HEADS, 1, LANES, tm),
         lambda b, s: (b, 0, jnp.maximum(s - 1, 0), 0, 0), BF16),
        ((B, S, 512), (1, tm, 512), tile, BF16),
        ((B, 2, S, LANES), (1, 2, tm, LANES), lambda b, s: (b, 0, jnp.maximum(s - 1, 0), 0), F32),
        ((B, KV, S, 2 * LANES), (1, KV, tm, 2 * LANES),
         lambda b, s: (b, 0, jnp.maximum(s - 1, 0), 0), BF16),
        ((B, KV, n, dh, tm), (1, KV, 1, dh, tm),
         lambda b, s: (b, 0, jnp.maximum(s - 1, 0), 0, 0), BF16),
        ((B, KV, S + WINDOW, LANES), (1, KV, tm, LANES), lambda b, s: (b, 0, s, 0), BF16),
        ((B, KV, (S + WINDOW) // LANES, dh, LANES), (1, KV, tm // LANES, dh, LANES),
         lambda b, s: (b, 0, s, 0, 0), BF16),
        ((B, S, LANES), (1, tm, LANES), tile, F32),
    ]
    return pl.pallas_call(
        _inproj_kernel,
        grid=(B, n + 1),
        in_specs=[pl.BlockSpec((1, tm, D), tile),
                  pl.BlockSpec((1, D), lambda b, s: (0, 0)),
                  pl.BlockSpec((1, 1, D), lambda b, s: (b, 0, 0)),
                  pl.BlockSpec((1, 1, D), lambda b, s: (b, 0, 0)),
                  pl.BlockSpec((D, _W_PROJ), lambda b, s: (0, 0))],
        out_specs=[pl.BlockSpec(blk, idx) for _, blk, idx, _ in out_shapes],
        out_shape=[jax.ShapeDtypeStruct(shape, dt) for shape, _, _, dt in out_shapes],
        compiler_params=_params(("parallel", "arbitrary")),
        name="in_proj",
    )(x, g_pre.reshape(1, D), sc.reshape(B, 1, D), sh.reshape(B, 1, D), w_proj)


def _compress_kernel(x_ref, pos_ref, w1_ref, w2_ref, kca_ref, vct_ref, vc_ref):
    dh = HEAD_DIM
    n_rows = x_ref.shape[2] // CMP_STRIDE

    def mlp(t):
        first = jnp.zeros((n_rows, w1_ref.shape[3]), F32)
        second = jnp.zeros((n_rows, w1_ref.shape[3]), F32)
        for m in range(CMP_STRIDE):
            x = x_ref[0, t, pl.ds(m, n_rows, stride=CMP_STRIDE), :]
            first += jnp.dot((x + pos_ref[t, m:m + 1, :]).astype(BF16), w1_ref[t, m],
                             preferred_element_type=F32)
            second += jnp.dot((x + pos_ref[t, CMP_STRIDE + m:CMP_STRIDE + m + 1, :]).astype(BF16),
                              w1_ref[t, CMP_STRIDE + m], preferred_element_type=F32)
        hid = first + pltpu.roll(second, n_rows - 1, 0)
        return jnp.dot(jax.nn.gelu(hid).astype(BF16), w2_ref[t], preferred_element_type=F32)

    lane = lax.broadcasted_iota(jnp.int32, (n_rows, LANES), 1)
    cend = lax.broadcasted_iota(jnp.int32, (n_rows, 1), 0) * CMP_STRIDE + (CMP_BLOCK - 1)
    pcol = _pos_lanes(cend, lane)
    kc = mlp(0)
    low = lane < dh
    kca_ref[0, 0] = jnp.where(low, kc, pcol).astype(BF16)
    kca_ref[0, 1] = jnp.where(low, pcol, kc).astype(BF16)
    vc_ref[...] = mlp(1)
    vc_t = vc_ref[...].T
    for h in range(NSA_KV_HEADS):
        vct_ref[0, h] = vc_t[h * dh:(h + 1) * dh].astype(BF16)


def _block_diag(blocks):
    n, r, c = blocks.shape[-3:]
    out = jnp.einsum('...irc,ij->...irjc', blocks, jnp.eye(n, dtype=blocks.dtype))
    return out.reshape(blocks.shape[:-3] + (n * r, n * c))


def _compress(ckv, pos_k, pos_v, w1_k, w1_v, w2_k, w2_v):
    B, _, S, W = ckv.shape
    KV, dh = NSA_KV_HEADS, HEAD_DIM
    n_rows = S // CMP_STRIDE
    pos = jnp.stack([jnp.concatenate([p, p], axis=1) for p in (pos_k, pos_v)]).astype(F32)
    w1 = jnp.stack([w.reshape(CMP_BLOCK, dh, -1) for w in (w1_k, w1_v)])
    w1 = _block_diag(jnp.stack([w1, w1], axis=2)).astype(BF16)
    w2 = jnp.stack([w2_k, w2_v])
    w2 = _block_diag(jnp.stack([w2, w2], axis=1)).astype(BF16)
    return pl.pallas_call(
        _compress_kernel,
        grid=(B,),
        in_specs=[pl.BlockSpec((1, 2, S, W), lambda b: (b, 0, 0, 0)),
                  pl.BlockSpec(pos.shape, lambda b: (0, 0, 0)),
                  pl.BlockSpec(w1.shape, lambda b: (0, 0, 0, 0)),
                  pl.BlockSpec(w2.shape, lambda b: (0, 0, 0))],
        out_specs=[pl.BlockSpec((1, KV, n_rows, LANES), lambda b: (b, 0, 0, 0)),
                   pl.BlockSpec((1, KV, dh, n_rows), lambda b: (b, 0, 0, 0))],
        out_shape=[jax.ShapeDtypeStruct((B, KV, n_rows, LANES), BF16),
                   jax.ShapeDtypeStruct((B, KV, dh, n_rows), BF16)],
        scratch_shapes=[pltpu.VMEM((n_rows, W), F32)],
        compiler_params=_params(("parallel",)),
        name="nsa_compress",
    )(ckv, pos, w1, w2)


def _flash_scratch(dv, n_cols):
    tk = KEY_TILE
    return [pltpu.VMEM((tk, n_cols), F32), pltpu.VMEM((tk, n_cols), F32),
            pltpu.VMEM((tk, n_cols), BF16), pltpu.VMEM((tk, n_cols), BF16),
            pltpu.VMEM((1, n_cols), F32), pltpu.VMEM((1, n_cols), F32),
            pltpu.VMEM((1, n_cols), F32), pltpu.VMEM((1, n_cols), F32), pltpu.VMEM((dv, n_cols), F32)]


def _flash_causal(q0, tq, qa_ref, k_tile, v_tile, scratch, tiles=None):
    s_a, s_b, p_a, p_b, a_a, a_b, m_ref, l_ref, acc_ref = scratch
    tk = KEY_TILE
    n_cols = qa_ref.shape[0]
    assert tk % tq == 0
    n_full = q0 // tk
    n = n_full + 1 if tiles is None else tiles[1]
    m_ref[...] = jnp.full(m_ref.shape, NEG, F32)
    l_ref[...] = jnp.zeros(l_ref.shape, F32)
    acc_ref[...] = jnp.zeros(acc_ref.shape, F32)
    p_b[...] = jnp.zeros(p_b.shape, BF16)
    a_b[...] = jnp.ones(a_b.shape, F32)

    def key_tile(t):
        t = jnp.clip(t, 0, n - 1)
        j = jnp.where(t == 0, n_full, t - 1) if tiles is None else tiles[0][t]
        return j, pl.multiple_of(j * tk, tk)

    def scores(t, s_ref, masked):
        j, k0 = key_tile(t)
        kt = k_tile(j, k0)
        for c in range(0, n_cols, COL_SPLIT):
            s = _dot_nt(kt, qa_ref[c:c + COL_SPLIT, :])
            if masked:
                krow = lax.broadcasted_iota(jnp.int32, (tk, 1), 0)
                qcol = (lax.broadcasted_iota(jnp.int32, (1, COL_SPLIT), 1) + c) % tq
                s = jnp.where((krow + (k0 - q0)) <= qcol, s, NEG)
            s_ref[:, c:c + COL_SPLIT] = s

    def softmax(s_ref, p_ref, a_ref):
        ch = SOFTMAX_CHUNK_VREGS * 8 * LANES // n_cols
        chunks = [slice(c, c + ch) for c in range(0, tk, ch)]
        mx = None
        for rows in chunks:
            part = jnp.max(s_ref[rows, :].reshape(ch // 8, 8, n_cols), axis=0)
            mx = part if mx is None else jnp.maximum(mx, part)
        m_old = m_ref[...]
        m_new = jnp.maximum(m_old, jnp.max(mx, axis=0, keepdims=True))
        sm = None
        for rows in chunks:
            p = jnp.exp2(s_ref[rows, :] - m_new)
            p_ref[rows, :] = p.astype(BF16)
            part = jnp.sum(p.reshape(ch // 8, 8, n_cols), axis=0)
            sm = part if sm is None else sm + part
        alpha = jnp.exp2(m_old - m_new)
        a_ref[...] = alpha
        l_ref[...] = alpha * l_ref[...] + jnp.sum(sm, axis=0, keepdims=True)
        m_ref[...] = m_new

    def values(t, p_ref, a_ref):
        j, k0 = key_tile(t)
        vt = v_tile(j, k0)
        for c in range(0, n_cols, COL_SPLIT):
            cs = slice(c, c + COL_SPLIT)
            acc_ref[:, cs] = a_ref[:, cs] * acc_ref[:, cs] + jnp.dot(vt, p_ref[:, cs], preferred_element_type=F32)

    scores(0, s_a, True)

    def pair(u, carry):
        i = 2 * u
        scores(i + 1, s_b, False)
        softmax(s_a, p_a, a_a)
        values(i - 1, p_b, a_b)

        @pl.when(i + 1 < n)
        def _():
            scores(i + 2, s_a, False)
            softmax(s_b, p_b, a_b)
            values(i, p_a, a_a)

        return carry

    lax.fori_loop(0, (n + 1) // 2, pair, 0)

    @pl.when(n % 2 == 1)
    def _():
        values(n - 1, p_a, a_a)

    @pl.when(n % 2 == 0)
    def _():
        values(n - 1, p_b, a_b)

    return acc_ref, l_ref


def _softmax_cols(s, bias):
    e, inv = _softmax_cols_unnormalised(s, bias)
    return e * inv


def _softmax_cols_unnormalised(s, bias):
    s = s + bias
    m = jnp.max(s, axis=0, keepdims=True)
    e = jnp.exp2(s - m)
    inv = jnp.where(m > 0.5 * NEG, 1.0 / jnp.sum(e, axis=0, keepdims=True), 0.0)
    return e, inv


def _diff_kernel(qx_ref, lam_ref, gsub_ref, q_ref, k_ref, v_ref, o_ref, qa_ref, *scratch, lambda_init):
    tq = DIFF_TQ
    q0 = pl.program_id(2) * tq
    q = q_ref[0].astype(F32)
    low = lax.broadcasted_iota(jnp.int32, (tq, LANES), 1) < HEAD_DIM
    ext = jnp.broadcast_to(qx_ref[0], (tq, LANES)).astype(BF16)
    qa_ref[0:tq, :LANES] = jnp.where(low, q, 0.0).astype(BF16)
    qa_ref[tq:, :LANES] = jnp.where(low, 0.0, q).astype(BF16)
    qa_ref[0:tq, LANES:] = ext
    qa_ref[tq:, LANES:] = ext

    acc_ref, l_ref = _flash_causal(q0, tq, qa_ref,
                                   lambda j, k0: k_ref[0, pl.ds(k0, KEY_TILE), :],
                                   lambda j, k0: v_ref[0, 0, j],
                                   scratch)

    lv = lam_ref[...]
    lam = (jnp.exp(jnp.sum(lv[0:1] * lv[1:2], axis=-1, keepdims=True))
           - jnp.exp(jnp.sum(lv[2:3] * lv[3:4], axis=-1, keepdims=True)) + lambda_init)
    o_t = acc_ref[...] / l_ref[...]
    o_t = o_t[:, :tq] - lam * o_t[:, tq:]
    o_t = o_t * lax.rsqrt(jnp.mean(o_t * o_t, axis=0, keepdims=True) + EPS)
    o_ref[0] = (o_t.T * gsub_ref[...] * (1.0 - lambda_init)).astype(o_ref.dtype)


def _diff_attention(dq, dk, dv_t, lam_vecs, g_subln, lambda_init):
    B, S, _ = dq.shape
    tq, tk = DIFF_TQ, KEY_TILE
    H = DIFF_HEADS
    qx = jnp.asarray(_slope_cols(_alibi_slopes(H), LANES, 0)).reshape(H, 1, LANES)
    lam = jnp.zeros((8, LANES), F32).at[:4, :HEAD_DIM].set(jnp.stack(lam_vecs))
    return pl.pallas_call(
        functools.partial(_diff_kernel, lambda_init=lambda_init),
        grid=(B, H, S // tq),
        in_specs=[pl.BlockSpec((1, 1, LANES), lambda b, h, i: (h, 0, 0)),
                  pl.BlockSpec((8, LANES), lambda b, h, i: (0, 0)),
                  pl.BlockSpec((1, LANES), lambda b, h, i: (0, 0)),
                  pl.BlockSpec((1, tq, LANES), lambda b, h, i: (b, i, h)),
                  pl.BlockSpec((1, S, 2 * LANES), lambda b, h, i: (b, 0, h)),
                  pl.BlockSpec((1, 1, S // tk, LANES, tk), lambda b, h, i: (b, h, 0, 0, 0))],
        out_specs=pl.BlockSpec((1, tq, LANES), lambda b, h, i: (b, i, h)),
        out_shape=jax.ShapeDtypeStruct((B, S, H * LANES), BF16),
        scratch_shapes=[pltpu.VMEM((2 * tq, 2 * LANES), BF16)] + _flash_scratch(LANES, 2 * tq),
        compiler_params=_params(("parallel", "parallel", "parallel")),
        name="diff_attention",
    )(qx, lam, g_subln.reshape(1, LANES).astype(F32), dq, dk, dv_t)


def _nsa_kernel(qx_ref, q_ref, kc_ref, vc_ref, ka_ref, sv_ref, wk_ref, wv_ref, g_ref, ov_ref, cb_ref, wb_ref,
                o_ref, qa_ref, out_ref, tiles_ref, *scratch):
    tq, G, dh = NSA_TQ, NSA_GROUP, HEAD_DIM
    n_cmp = kc_ref.shape[2]
    n_win = WINDOW + tq
    grp = pl.program_id(1)
    qi = pl.program_id(2)
    q0 = pl.multiple_of(qi * tq, tq)
    cols = [slice(i * tq, (i + 1) * tq) for i in range(G)]
    g_t = g_ref[0].T

    def gate(i, branch):
        a = 3 * i + branch
        b = 3 * (G + i) + branch
        return jnp.where(grp == 0, g_t[a:a + 1, :], g_t[b:b + 1, :])

    lane_q = lax.broadcasted_iota(jnp.int32, (tq, LANES), 1)
    mine = (lane_q >= grp * dh) & (lane_q < (grp + 1) * dh)
    qx = qx_ref[0]
    for i in range(G):
        q = q_ref[0, :, i * LANES:(i + 1) * LANES].astype(F32)
        qa_ref[cols[i], :LANES] = jnp.where(mine, q, jnp.broadcast_to(qx[i:i + 1], (tq, LANES))).astype(BF16)
    qs = qa_ref[:, :LANES]

    kw = wk_ref[0, 0, pl.ds(q0, n_win), :]
    vw = jnp.concatenate([wv_ref[0, 0, (tq // LANES) * qi + c] for c in range(n_win // LANES)], axis=1)
    e, inv = _softmax_cols_unnormalised(_dot_nt(kw, qs), jnp.tile(wb_ref[...], (1, G)))
    ow = jnp.dot(vw, e.astype(BF16), preferred_element_type=F32) * inv

    cbias = cb_ref[pl.ds(pl.multiple_of(n_cmp - (tq // CMP_STRIDE) * qi, 8), n_cmp), :]
    p = _softmax_cols(_dot_nt(kc_ref[0, 0], qs), jnp.tile(cbias, (1, G)))
    oc = jnp.dot(vc_ref[0, 0], p.astype(BF16), preferred_element_type=F32)
    for i in range(G):
        out_ref[i * dh:(i + 1) * dh, :] = gate(i, 0) * oc[:, cols[i]] + gate(i, 2) * ow[:, cols[i]]

    psum = p[:, cols[0]]
    for i in range(1, G):
        psum = psum + p[:, cols[i]]
    hi = psum.astype(BF16)
    r1 = psum - hi.astype(F32)
    mid = r1.astype(BF16)
    lo = (r1 - mid.astype(F32)).astype(BF16)
    ov = ov_ref[...]
    imp = (jnp.dot(ov, hi, preferred_element_type=F32) + jnp.dot(ov, mid, preferred_element_type=F32)
           + jnp.dot(ov, lo, preferred_element_type=F32))
    n_slc = imp.shape[0]
    blk = lax.broadcasted_iota(jnp.int32, (n_slc, tq), 0)
    cur = (lax.broadcasted_iota(jnp.int32, (1, tq), 1) + q0) // SLC_BLOCK
    forced = (blk == 0) | (blk == cur) | (blk == cur - 1)
    score = jnp.where(forced, -jnp.inf, jnp.where(blk <= cur, imp, NEG))
    blkf = blk.astype(F32)
    sel = jnp.where(forced, 1.0, 0.0)
    for _ in range(SLC_TOP - 3):
        mx = jnp.max(score, axis=0, keepdims=True)
        idx = jnp.min(jnp.where(score == mx, blkf, float(n_slc)), axis=0, keepdims=True)
        hit = blkf == idx
        sel = jnp.where(hit, 1.0, sel)
        score = jnp.where(hit, -jnp.inf, score)
    sel_t = sel.T
    blockmask = ((sel_t - 1.0) * 1e30).astype(BF16)
    for i in range(G):
        qa_ref[cols[i], LANES:] = blockmask

    per_tile = KEY_TILE // SLC_BLOCK
    n_tiles = n_slc // per_tile
    used = jnp.max(jnp.max(sel, axis=1, keepdims=True).reshape(n_tiles, per_tile, 1), axis=1)
    bit = jnp.where(used > 0.0, 1 << lax.broadcasted_iota(jnp.int32, (n_tiles, 1), 0), 0)
    bits = jnp.sum(bit, axis=0, keepdims=True)[0, 0]
    n_full = q0 // KEY_TILE
    tiles_ref[0] = n_full
    count = jnp.int32(1)
    for j in range(n_tiles):
        tiles_ref[count] = j
        count = count + jnp.where(j < n_full, (bits >> j) & 1, 0)

    acc_ref, l_ref = _flash_causal(q0, tq, qa_ref,
                                   lambda j, k0: ka_ref[0, 0, pl.ds(k0, KEY_TILE), :],
                                   lambda j, k0: sv_ref[0, 0, j],
                                   scratch, tiles=(tiles_ref, count))
    os_ = acc_ref[...] / l_ref[...]
    for i in range(G):
        out_ref[i * dh:(i + 1) * dh, :] += gate(i, 1) * os_[:, cols[i]]

    o_ref[0] = out_ref[...].T.astype(o_ref.dtype)


def _nsa_attention(nq, kca, vc_t, ska, sv_t, wka, wv_t, gates):
    B, S, _ = nq.shape
    tq, tk, G, dh = NSA_TQ, KEY_TILE, NSA_GROUP, HEAD_DIM
    KV = NSA_KV_HEADS
    n_cmp = kca.shape[2]
    n_slc = S // SLC_BLOCK
    Sp = S + WINDOW
    slopes = _alibi_slopes(NSA_HEADS)
    qx = jnp.asarray(np.stack([_slope_cols(slopes[:G], LANES, dh), _slope_cols(slopes[G:], LANES, 0)]))
    overlap_t = _overlap_matrix(S).T
    r = np.arange(tq)[None, :]
    j = np.arange(2 * n_cmp)[:, None]
    cmp_band = jnp.asarray(np.where(CMP_STRIDE * (j - n_cmp) + CMP_BLOCK - 1 <= r, 0.0, NEG).astype(np.float32))
    c = np.arange(WINDOW + tq)[:, None]
    d = r - c + WINDOW
    win_band = jnp.asarray(np.where((d >= 0) & (d < WINDOW), 0.0, NEG).astype(np.float32))
    const2 = lambda b, g, i: (0, 0)
    per_bg = lambda b, g, i: (b, g, 0, 0)
    per_bg5 = lambda b, g, i: (b, g, 0, 0, 0)
    return pl.pallas_call(
        _nsa_kernel,
        grid=(B, KV, S // tq),
        in_specs=[pl.BlockSpec((1, G, LANES), lambda b, g, i: (g, 0, 0)),
                  pl.BlockSpec((1, tq, G * LANES), lambda b, g, i: (b, i, 0)),
                  pl.BlockSpec((1, 1, n_cmp, LANES), per_bg),
                  pl.BlockSpec((1, 1, dh, n_cmp), per_bg),
                  pl.BlockSpec((1, 1, S, 2 * LANES), per_bg),
                  pl.BlockSpec((1, 1, S // tk, dh, tk), per_bg5),
                  pl.BlockSpec((1, 1, Sp, LANES), per_bg),
                  pl.BlockSpec((1, 1, Sp // LANES, dh, LANES), per_bg5),
                  pl.BlockSpec((1, tq, LANES), lambda b, g, i: (b, i, 0)),
                  pl.BlockSpec((n_slc, n_cmp), const2),
                  pl.BlockSpec(cmp_band.shape, const2), pl.BlockSpec(win_band.shape, const2)],
        out_specs=pl.BlockSpec((1, tq, G * dh), lambda b, g, i: (b, i, g)),
        out_shape=jax.ShapeDtypeStruct((B, S, NSA_HEADS * dh), BF16),
        scratch_shapes=[pltpu.VMEM((G * tq, 2 * LANES), BF16),
                        pltpu.VMEM((G * dh, tq), F32),
                        pltpu.SMEM((2 * (S // tk),), jnp.int32)]
        + _flash_scratch(dh, G * tq),
        compiler_params=_params(("parallel", "parallel", "parallel")),
        name="nsa_attention",
    )(qx, nq, kca, vc_t, ska, sv_t, wka, wv_t, gates, overlap_t, cmp_band, win_band)


def _outproj_kernel(oa_ref, ob_ref, x_ref, wo_ref, gpm_ref, gta_ref, gpf_ref, sc_ref, sh_ref, wr_ref, br_ref,
                    x1_ref, h2_ref, rt_ref):
    half = oa_ref.shape[1]
    mix = (jnp.dot(oa_ref[...], wo_ref[:half], preferred_element_type=F32)
           + jnp.dot(ob_ref[...], wo_ref[half:], preferred_element_type=F32))
    x1 = x_ref[...] + gta_ref[0] * (_rms(mix) * gpm_ref[...])
    x1_ref[...] = x1
    h2 = _rms(x1) * gpf_ref[...] * (1.0 + sc_ref[0]) + sh_ref[0]
    h2_ref[...] = h2
    h_hi = h2.astype(BF16)
    h_lo = (h2 - h_hi.astype(F32)).astype(BF16)
    logits = (jnp.dot(h_hi, wr_ref[0], preferred_element_type=F32)
              + jnp.dot(h_lo, wr_ref[0], preferred_element_type=F32)
              + jnp.dot(h_hi, wr_ref[1], preferred_element_type=F32)) + br_ref[...]
    tm = logits.shape[0]
    lane = lax.broadcasted_iota(jnp.int32, (tm, LANES), 1)
    lanef = lane.astype(F32)
    is_grp = (lane >= N_EXPERTS) & (lane < N_EXPERTS + N_GROUPS)
    lg = jnp.where(is_grp, logits, -jnp.inf)
    mg = jnp.max(lg, axis=-1, keepdims=True)
    g_lane = jnp.min(jnp.where(lg == mg, lanef, 1e9), axis=-1, keepdims=True)
    p_grp = 1.0 / jnp.sum(jnp.where(is_grp, jnp.exp(lg - mg), 0.0), axis=-1, keepdims=True)
    grp = g_lane.astype(jnp.int32) - N_EXPERTS
    in_grp = (lane < N_EXPERTS) & ((lane // EXPERTS_PER_GROUP) == grp)
    le = jnp.where(in_grp, logits, -jnp.inf)
    v1 = jnp.max(le, axis=-1, keepdims=True)
    i1 = jnp.min(jnp.where(le == v1, lanef, 1e9), axis=-1, keepdims=True)
    le2 = jnp.where(lanef == i1, -jnp.inf, le)
    v2 = jnp.max(le2, axis=-1, keepdims=True)
    i2 = jnp.min(jnp.where(le2 == v2, lanef, 1e9), axis=-1, keepdims=True)
    e2 = jnp.exp(v2 - v1)
    w1 = p_grp / (1.0 + e2)
    w2 = p_grp * e2 / (1.0 + e2)
    rt_ref[...] = jnp.where(lane == 0, i1, jnp.where(lane == 1, i2, jnp.where(lane == 2, w1,
                            jnp.where(lane == 3, w2, 0.0))))


def _out_proj(o_a, o_b, x2, w_out, g_post_mix, gt_a, g_pre_ffn, sc_f, sh_f, w_r, b_r, B, S):
    T, D = x2.shape
    tm = ROW_TILE
    per_b = S // tm
    row = lambda i: (i, 0)
    bat = lambda i: (i // per_b, 0, 0)
    const = lambda i: (0, 0)
    return pl.pallas_call(
        _outproj_kernel,
        grid=(T // tm,),
        in_specs=[pl.BlockSpec((tm, o_a.shape[1]), row), pl.BlockSpec((tm, o_b.shape[1]), row),
                  pl.BlockSpec((tm, D), row), pl.BlockSpec((D, D), const),
                  pl.BlockSpec((1, D), const), pl.BlockSpec((1, 1, D), bat), pl.BlockSpec((1, D), const),
                  pl.BlockSpec((1, 1, D), bat), pl.BlockSpec((1, 1, D), bat),
                  pl.BlockSpec((2, D, LANES), lambda i: (0, 0, 0)), pl.BlockSpec((1, LANES), const)],
        out_specs=[pl.BlockSpec((tm, D), row), pl.BlockSpec((tm, D), row), pl.BlockSpec((tm, LANES), row)],
        out_shape=[jax.ShapeDtypeStruct((T, D), F32), jax.ShapeDtypeStruct((T, D), F32),
                   jax.ShapeDtypeStruct((T, LANES), F32)],
        compiler_params=_params(("parallel",)),
        name="out_proj_router",
    )(o_a, o_b, x2, w_out, g_post_mix.reshape(1, D), gt_a.reshape(B, 1, D), g_pre_ffn.reshape(1, D),
      sc_f.reshape(B, 1, D), sh_f.reshape(B, 1, D), w_r, b_r)


def _slots_kernel(rt_ref, slot_ref, blk_ref, cnt_ref, base_ref, *, n_blk_rows):
    phase = pl.program_id(0)
    step = pl.program_id(1)
    tm = rt_ref.shape[0]
    rt = rt_ref[...]
    lane = lax.broadcasted_iota(jnp.int32, (tm, LANES), 1).astype(F32)
    i1 = rt[:, 0:1]
    i2 = rt[:, 1:2]
    hit1 = lane == i1
    hit2 = lane == i2
    onehot = jnp.where(hit1, 1.0, jnp.where(hit2, 1.0, 0.0))
    colsum = jnp.sum(onehot, axis=0, keepdims=True)

    @pl.when((phase == 0) & (step == 0))
    def _():
        cnt_ref[...] = jnp.zeros(cnt_ref.shape, F32)

    @pl.when(phase == 0)
    def _():
        cnt_ref[...] += colsum

    @pl.when((phase == 1) & (step == 0))
    def _():
        cnt = cnt_ref[...]
        padded = jnp.floor((cnt + (MOE_ROWS - 1)) / MOE_ROWS) * MOE_ROWS
        lane1 = lax.broadcasted_iota(jnp.int32, (1, LANES), 1)
        incl = padded
        sh = 1
        while sh < LANES:
            incl = incl + jnp.where(lane1 >= sh, pltpu.roll(incl, sh, 1), 0.0)
            sh *= 2
        base_ref[...] = incl - padded
        rows = lax.broadcasted_iota(jnp.int32, (n_blk_rows, LANES), 0).astype(F32) * MOE_ROWS
        lane2 = lax.broadcasted_iota(jnp.int32, (n_blk_rows, LANES), 1)
        ended = jnp.where(lane2 < N_EXPERTS, jnp.where(incl <= rows, 1.0, 0.0), 0.0)
        blk_e = jnp.minimum(jnp.sum(ended, axis=-1, keepdims=True), N_EXPERTS - 1.0)
        total = jnp.sum(jnp.where(lane1 == N_EXPERTS - 1, incl, 0.0), axis=-1, keepdims=True)
        n_used = total / MOE_ROWS
        blk_ref[...] = jnp.where(lane2 == 0, blk_e, jnp.where(lane2 == 1, n_used, 0.0)).astype(jnp.int32)

    @pl.when(phase == 1)
    def _():
        r = lax.broadcasted_iota(jnp.int32, (tm, tm), 0)
        c = lax.broadcasted_iota(jnp.int32, (tm, tm), 1)
        tri = jnp.where(c < r, 1.0, 0.0).astype(BF16)
        before = jnp.dot(tri, onehot.astype(BF16), preferred_element_type=F32) + base_ref[...]
        s1 = jnp.sum(jnp.where(hit1, before, 0.0), axis=-1, keepdims=True)
        s2 = jnp.sum(jnp.where(hit2, before, 0.0), axis=-1, keepdims=True)
        lane_i = lax.broadcasted_iota(jnp.int32, (tm, LANES), 1)
        slot_ref[...] = jnp.where(lane_i == 0, s1, jnp.where(lane_i == 1, s2, 0.0)).astype(jnp.int32)
        base_ref[...] += colsum


def _slots(route, n_blocks):
    T = route.shape[0]
    tm = ROW_TILE
    n_blk_rows = -(-n_blocks // 8) * 8
    slot, blk = pl.pallas_call(
        functools.partial(_slots_kernel, n_blk_rows=n_blk_rows),
        grid=(2, T // tm),
        in_specs=[pl.BlockSpec((tm, LANES), lambda p, i: (i, 0))],
        out_specs=[pl.BlockSpec((tm, LANES), lambda p, i: (i * p, 0)),
                   pl.BlockSpec((n_blk_rows, LANES), lambda p, i: (0, 0))],
        out_shape=[jax.ShapeDtypeStruct((T, LANES), jnp.int32),
                   jax.ShapeDtypeStruct((n_blk_rows, LANES), jnp.int32)],
        scratch_shapes=[pltpu.VMEM((1, LANES), F32), pltpu.VMEM((1, LANES), F32)],
        compiler_params=_params(("arbitrary", "arbitrary")),
        name="moe_slots",
    )(route)
    return slot[:, 0], slot[:, 1], blk[:n_blocks, 0], blk[0:1, 1]


def _inverse_kernel(s1_ref, s2_ref, inv_ref):
    def clear(r, carry):
        inv_ref[r] = 0
        return carry

    lax.fori_loop(0, inv_ref.shape[0], clear, 0, unroll=DMA_UNROLL)

    def mark(t, carry):
        inv_ref[s1_ref[t]] = t
        inv_ref[s2_ref[t]] = t
        return carry

    lax.fori_loop(0, s1_ref.shape[0], mark, 0, unroll=DMA_UNROLL)


def _inverse(slot1, slot2, n_rows):
    return pl.pallas_call(
        _inverse_kernel,
        grid_spec=pltpu.PrefetchScalarGridSpec(
            num_scalar_prefetch=2, grid=(1,), in_specs=[],
            out_specs=pl.BlockSpec(memory_space=pltpu.SMEM)),
        out_shape=jax.ShapeDtypeStruct((n_rows,), jnp.int32),
        compiler_params=_params(("arbitrary",)),
        name="moe_inverse",
    )(slot1, slot2)


def _expert_kernel(be_ref, nu_ref, inv_ref, h_ref, wg_ref, wu_ref, wd_ref, y_ref,
                   x_ref, wgb_ref, wub_ref, wdb_ref, sem):
    i = pl.program_id(0)
    n_used = nu_ref[0]
    rows = x_ref.shape[1]

    def row_copy(src_row, slot, r):
        return pltpu.make_async_copy(h_ref.at[pl.ds(src_row, 1), :], x_ref.at[slot, pl.ds(r, 1), :], sem.at[slot])

    def gather(blk, slot):
        def issue(r, carry):
            row_copy(inv_ref[blk * rows + r], slot, r).start()
            return carry

        lax.fori_loop(0, rows, issue, 0, unroll=DMA_UNROLL)

    def gather_wait(slot):
        def drain(r, carry):
            row_copy(0, slot, 0).wait()
            return carry

        lax.fori_loop(0, rows, drain, 0, unroll=DMA_UNROLL)

    @pl.when(i == 0)
    def _():
        gather(0, 0)

    @pl.when(i + 1 < n_used)
    def _():
        gather(i + 1, (i + 1) % 2)

    @pl.when(i >= n_used)
    def _():
        y_ref[...] = jnp.zeros(y_ref.shape, y_ref.dtype)

    @pl.when(i < n_used)
    def _():
        changed = be_ref[i] != be_ref[jnp.maximum(i - 1, 0)]

        @pl.when((i == 0) | changed)
        def _():
            wgb_ref[...] = wg_ref[0].astype(BF16)
            wub_ref[...] = wu_ref[0].astype(BF16)
            wdb_ref[...] = wd_ref[0].astype(BF16)

        gather_wait(i % 2)
        x = x_ref[i % 2].astype(BF16)
        g = jnp.dot(x, wgb_ref[...], preferred_element_type=F32)
        u = jnp.dot(x, wub_ref[...], preferred_element_type=F32)
        hid = (g * jax.nn.sigmoid(g) * u).astype(BF16)
        y_ref[...] = jnp.dot(hid, wdb_ref[...], preferred_element_type=F32).astype(y_ref.dtype)


def _experts(h2, inv, blk_e, n_used, w_gate, w_up, w_down):
    T, D = h2.shape
    R = inv.shape[0]
    E, _, De = w_gate.shape
    wsel = lambda i, be, nu, inv: (be[jnp.minimum(i, nu[0] - 1)], 0, 0)
    return pl.pallas_call(
        _expert_kernel,
        grid_spec=pltpu.PrefetchScalarGridSpec(
            num_scalar_prefetch=3,
            grid=(R // MOE_ROWS,),
            in_specs=[pl.BlockSpec(memory_space=pl.ANY),
                      pl.BlockSpec((1, D, De), wsel), pl.BlockSpec((1, D, De), wsel),
                      pl.BlockSpec((1, De, D), wsel)],
            out_specs=pl.BlockSpec((MOE_ROWS, D), lambda i, be, nu, inv: (i, 0)),
            scratch_shapes=[pltpu.VMEM((2, MOE_ROWS, D), F32),
                            pltpu.VMEM((D, De), BF16), pltpu.VMEM((D, De), BF16), pltpu.VMEM((De, D), BF16),
                            pltpu.SemaphoreType.DMA((2,))]),
        out_shape=jax.ShapeDtypeStruct((R, D), F32),
        compiler_params=_params(("arbitrary",)),
        name="moe_experts",
    )(blk_e, n_used, inv, h2, w_gate, w_up, w_down)


def _combine_kernel(s1_ref, s2_ref, yb_ref, rt_ref, x1_ref, gtf_ref, gpo_ref, o_ref, ya_ref, yc_ref, sem):
    tm = x1_ref.shape[0]
    base = pl.program_id(0) * tm

    def issue(r, carry):
        pltpu.make_async_copy(yb_ref.at[pl.ds(s1_ref[base + r], 1), :], ya_ref.at[pl.ds(r, 1), :], sem).start()
        pltpu.make_async_copy(yb_ref.at[pl.ds(s2_ref[base + r], 1), :], yc_ref.at[pl.ds(r, 1), :], sem).start()
        return carry

    lax.fori_loop(0, tm, issue, 0, unroll=DMA_UNROLL)

    def drain(r, carry):
        for _ in range(TOP_K):
            pltpu.make_async_copy(yb_ref.at[pl.ds(0, 1), :], ya_ref.at[pl.ds(0, 1), :], sem).wait()
        return carry

    lax.fori_loop(0, tm, drain, 0, unroll=DMA_UNROLL)
    rt = rt_ref[...]
    f = rt[:, 2:3] * ya_ref[...] + rt[:, 3:4] * yc_ref[...]
    o_ref[...] = x1_ref[...] + gtf_ref[0] * (_rms(f) * gpo_ref[...])


def _combine(ybuf, slot1, slot2, route, x1, gt_f, g_post_ffn, B, S):
    T, D = x1.shape
    tm = ROW_TILE
    per_b = S // tm
    row = lambda i, a, b: (i, 0)
    return pl.pallas_call(
        _combine_kernel,
        grid_spec=pltpu.PrefetchScalarGridSpec(
            num_scalar_prefetch=2,
            grid=(T // tm,),
            in_specs=[pl.BlockSpec(memory_space=pl.ANY),
                      pl.BlockSpec((tm, LANES), row), pl.BlockSpec((tm, D), row),
                      pl.BlockSpec((1, 1, D), lambda i, a, b: (i // per_b, 0, 0)),
                      pl.BlockSpec((1, D), lambda i, a, b: (0, 0))],
            out_specs=pl.BlockSpec((tm, D), row),
            scratch_shapes=[pltpu.VMEM((tm, D), F32), pltpu.VMEM((tm, D), F32), pltpu.SemaphoreType.DMA(())]),
        out_shape=jax.ShapeDtypeStruct((T, D), F32),
        compiler_params=_params(("arbitrary",)),
        name="moe_combine",
    )(slot1, slot2, ybuf, route, x1, gt_f.reshape(B, 1, D), g_post_ffn.reshape(1, D))


def _overlap_matrix(S):
    n_rows = S // CMP_STRIDE
    n_cmp = (S - CMP_BLOCK) // CMP_STRIDE + 1
    n_slc = S // SLC_BLOCK
    cstart = np.arange(n_rows) * CMP_STRIDE
    sstart = np.arange(n_slc) * SLC_BLOCK
    ov = (cstart[:, None] < sstart[None, :] + SLC_BLOCK) & (cstart[:, None] + CMP_BLOCK > sstart[None, :])
    ov &= (np.arange(n_rows) < n_cmp)[:, None]
    return jnp.asarray(ov.astype(np.float32), dtype=BF16)


def kernel(x, c, w_ada, b_ada, g_pre_mix, g_post_mix, g_pre_ffn, g_post_ffn, w_in, lam_q1, lam_k1, lam_q2,
           lam_k2, g_subln, cmp_pos_k, cmp_pos_v, cmp_w1_k, cmp_w1_v, cmp_w2_k, cmp_w2_v, w_out,
           w_router_grp, b_router_grp, w_router_exp, b_router_exp, w_exp_gate, w_exp_up, w_exp_down):
    B, S, D = x.shape
    T = B * S
    depth = w_ada.shape[0]
    assert S // SLC_BLOCK == LANES and D == D_MODEL
    n_blocks = (T * TOP_K) // MOE_ROWS + N_EXPERTS
    for l in range(depth):
        lambda_init = 0.8 - 0.6 * math.exp(-0.3 * l)
        mod = _adaln(c, w_ada[l], b_ada[l])
        sh_a, sc_a, gt_a, sh_f, sc_f, gt_f = jnp.split(mod, 6, axis=-1)

        dq, dk, dv_t, nq, ckv, ska, sv_t, wka, wv_t, gates = _in_proj(x, g_pre_mix[l], sc_a, sh_a,
                                                                      _build_w_proj(w_in[l]))
        kca, vc_t = _compress(ckv, cmp_pos_k[l], cmp_pos_v[l], cmp_w1_k[l], cmp_w1_v[l], cmp_w2_k[l], cmp_w2_v[l])
        o_a = _diff_attention(dq, dk, dv_t, (lam_q1[l], lam_k1[l], lam_q2[l], lam_k2[l]), g_subln[l], lambda_init)
        o_b = _nsa_attention(nq, kca, vc_t, ska, sv_t, wka, wv_t, gates)

        w_r = jnp.zeros((D, LANES), F32).at[:, :N_EXPERTS].set(w_router_exp[l]).at[
            :, N_EXPERTS:N_EXPERTS + N_GROUPS].set(w_router_grp[l])
        b_r = jnp.zeros((1, LANES), F32).at[0, :N_EXPERTS].set(b_router_exp[l]).at[
            0, N_EXPERTS:N_EXPERTS + N_GROUPS].set(b_router_grp[l])
        w_r_hi = w_r.astype(BF16)
        w_r = jnp.stack([w_r_hi, (w_r - w_r_hi.astype(F32)).astype(BF16)])
        x1, h2, route = _out_proj(o_a.reshape(T, -1), o_b.reshape(T, -1), x.reshape(T, D), w_out[l].astype(BF16),
                                  g_post_mix[l], gt_a, g_pre_ffn[l], sc_f, sh_f, w_r, b_r, B, S)
        slot1, slot2, blk_e, n_used = _slots(route, n_blocks)
        inv = _inverse(slot1, slot2, n_blocks * MOE_ROWS)
        ybuf = _experts(h2, inv, blk_e, n_used, w_exp_gate[l], w_exp_up[l], w_exp_down[l])
        x = _combine(ybuf, slot1, slot2, route, x1, gt_f, g_post_ffn[l], B, S).reshape(B, S, D)
    return x
```

```python
import functools
import math

import numpy as np
import jax
import jax.numpy as jnp
from jax import lax
from jax.experimental import pallas as pl
from jax.experimental.pallas import tpu as pltpu

F32 = jnp.float32
BF16 = jnp.bfloat16
HIGHEST = lax.Precision.HIGHEST

D_MODEL = 1024
HEAD_DIM = 64
LANES = 128
DIFF_HEADS = 4
NSA_HEADS = 8
NSA_KV_HEADS = 2
NSA_GROUP = NSA_HEADS // NSA_KV_HEADS
CMP_BLOCK = 32
CMP_STRIDE = 16
CMP_HIDDEN = 2 * HEAD_DIM
SLC_BLOCK = 64
SLC_TOP = 16
WINDOW = 512
N_GROUPS = 8
EXPERTS_PER_GROUP = 8
N_EXPERTS = N_GROUPS * EXPERTS_PER_GROUP
TOP_K = 2
D_EXPERT = D_MODEL // 2
NEG = -1e30
FORCE_BONUS = 1e4
EPS = 1e-6

ROW_TILE = 512
KEY_TILE = 512
DIFF_TQ = 512
NSA_TQ = 256
COL_SPLIT = 512
SOFTMAX_CHUNK_VREGS = 16
MOE_ROWS = 256
DMA_UNROLL = 8
VMEM_LIMIT = 48 * 1024 * 1024
assert ROW_TILE == KEY_TILE == WINDOW

_OFF_DQ, _OFF_DK, _OFF_DV, _OFF_NQ = 0, 512, 1024, 1536
_OFF_KV = 2048
_OFF_NG = 2816
_N_GATES = 3 * NSA_HEADS
_W_PROJ = 2048 + 768 + LANES


def _alibi_slopes(n):
    return np.array([2.0 ** (-8.0 * (i + 1) / n) for i in range(n)], dtype=np.float32)


def _dot_nt(a, b):
    return lax.dot_general(a, b, (((1,), (1,)), ((), ())), preferred_element_type=F32)


def _rms(v):
    return v * lax.rsqrt(jnp.mean(v * v, axis=-1, keepdims=True) + EPS)


def _params(sem, vmem=VMEM_LIMIT):
    return pltpu.CompilerParams(dimension_semantics=sem, vmem_limit_bytes=vmem)


LOG2E = np.float32(1.0 / math.log(2.0))
N_SLOPE_PIECES = 3
MASK_LANE = 2 * N_SLOPE_PIECES


def _pos_lanes(pos, lane):
    half = lane % HEAD_DIM
    return jnp.where(half < MASK_LANE, jnp.where(half % 2 == 0, pos // LANES, pos % LANES), 0).astype(F32)


def _slope_cols(slopes, width, at):
    cols = np.zeros((len(slopes), width), np.float32)
    rest = (np.asarray(slopes, np.float32) * LOG2E).astype(np.float32)
    for i in range(N_SLOPE_PIECES):
        piece = rest.astype(BF16).astype(np.float32)
        cols[:, at + 2 * i] = piece * LANES
        cols[:, at + 2 * i + 1] = piece
        rest = rest - piece
    cols[:, at + MASK_LANE] = NEG
    return cols


def _adaln_kernel(c_ref, w_ref, b_ref, o_ref):
    c = c_ref[...]
    s = c * jax.nn.sigmoid(c)
    o_ref[...] = jnp.dot(s, w_ref[...], preferred_element_type=F32, precision=HIGHEST) + b_ref[...]


def _adaln(c, w_ada, b_ada):
    B, D = c.shape
    rows = 8
    cp = jnp.zeros((rows, D), F32).at[:B].set(c)
    n_out = w_ada.shape[1]
    out = pl.pallas_call(
        _adaln_kernel,
        grid=(n_out // D,),
        in_specs=[pl.BlockSpec((rows, D), lambda j: (0, 0)),
                  pl.BlockSpec((D, D), lambda j: (0, j)),
                  pl.BlockSpec((1, D), lambda j: (0, j))],
        out_specs=pl.BlockSpec((rows, D), lambda j: (0, j)),
        out_shape=jax.ShapeDtypeStruct((rows, n_out), F32),
        compiler_params=_params(("parallel",)),
        name="adaln",
    )(cp, w_ada, b_ada.reshape(1, n_out))
    return out[:B]


def _inproj_kernel(x_ref, g_ref, sc_ref, sh_ref, w_ref,
                   dq_ref, dk_ref, dvt_ref, nq_ref, ckv_ref, ska_ref, svt_ref, wka_ref, wvt_ref, gt_ref):
    step = pl.program_id(1)
    tm = x_ref.shape[1]
    dh = HEAD_DIM

    @pl.when(step == 0)
    def _():
        lane = lax.broadcasted_iota(jnp.int32, (tm, LANES), 1)
        for g in range(NSA_KV_HEADS):
            wka_ref[0, g] = jnp.where(lane == (1 - g) * dh + MASK_LANE, 1.0, 0.0).astype(BF16)
        wvt_ref[...] = jnp.zeros(wvt_ref.shape, wvt_ref.dtype)

    @pl.when(step > 0)
    def _():
        h = (_rms(x_ref[0]) * g_ref[...] * (1.0 + sc_ref[0]) + sh_ref[0]).astype(BF16)

        def proj(off, width):
            return jnp.dot(h, w_ref[:, off:off + width], preferred_element_type=F32)

        lane = lax.broadcasted_iota(jnp.int32, (tm, LANES), 1)
        pos = (step - 1) * tm + lax.broadcasted_iota(jnp.int32, (tm, 1), 0)
        low = lane < dh
        pcol = _pos_lanes(pos, lane)
        wcol = _pos_lanes(pos + WINDOW, lane)

        dq_ref[0] = proj(_OFF_DQ, 512).astype(BF16)
        k = proj(_OFF_DK, 512)
        v = proj(_OFF_DV, 512)
        for hh in range(DIFF_HEADS):
            dk_ref[0, :, hh * 256:hh * 256 + LANES] = k[:, hh * LANES:(hh + 1) * LANES].astype(BF16)
            dk_ref[0, :, hh * 256 + LANES:(hh + 1) * 256] = pcol.astype(BF16)
            dvt_ref[0, hh, 0] = v[:, hh * LANES:(hh + 1) * LANES].T.astype(BF16)
        nq_ref[0] = proj(_OFF_NQ, 512).astype(BF16)
        kv = proj(_OFF_KV, 768)
        ckv_ref[0, 0] = kv[:, :LANES]
        ckv_ref[0, 1] = kv[:, LANES:2 * LANES]
        sk, sv, wk, wv = (kv[:, i * LANES:(i + 1) * LANES] for i in range(2, 6))
        onehot = jnp.where(lane == pos // SLC_BLOCK, 1.0, 0.0).astype(BF16)
        sv_t = sv.T
        wv_t = wv.T
        for g, (keys_sel, keys_win) in enumerate(((jnp.where(low, sk, pcol), jnp.where(low, wk, wcol)),
                                                  (jnp.where(low, pcol, sk), jnp.where(low, wcol, wk)))):
            ska_ref[0, g, :, :LANES] = keys_sel.astype(BF16)
            ska_ref[0, g, :, LANES:] = onehot
            svt_ref[0, g, 0] = sv_t[g * dh:(g + 1) * dh].astype(BF16)
            wka_ref[0, g] = keys_win.astype(BF16)
            for c in range(tm // LANES):
                wvt_ref[0, g, c] = wv_t[g * dh:(g + 1) * dh, c * LANES:(c + 1) * LANES].astype(BF16)
        gt_ref[0] = jax.nn.sigmoid(proj(_OFF_NG, LANES))


def _build_w_proj(w_in):
    scale = LOG2E * HEAD_DIM ** -0.5
    nq = w_in[:, _OFF_NQ:_OFF_NQ + 512].reshape(D_MODEL, NSA_KV_HEADS, NSA_GROUP, HEAD_DIM)
    nq = nq.transpose(0, 2, 1, 3).reshape(D_MODEL, 512) * scale
    cols = [w_in[:, _OFF_DQ:_OFF_DQ + 512] * scale, w_in[:, _OFF_DK:_OFF_NQ], nq,
            w_in[:, _OFF_KV:_OFF_NG + _N_GATES], jnp.zeros((D_MODEL, LANES - _N_GATES), F32)]
    return jnp.concatenate(cols, axis=1).astype(BF16)


def _in_proj(x, g_pre, sc, sh, w_proj):
    B, S, D = x.shape
    tm = ROW_TILE
    n = S // tm
    KV, dh = NSA_KV_HEADS, HEAD_DIM
    tile = lambda b, s: (b, jnp.maximum(s - 1, 0), 0)
    out_shapes = [
        ((B, S, 512), (1, tm, 512), tile, BF16),
        ((B, S, DIFF_HEADS * 256), (1, tm, DIFF_HEADS * 256), tile, BF16),
        ((B, DIFF_HEADS, n, LANES, tm), (1, DIFF_HEADS, 1, LANES, tm),
         lambda b, s: (b, 0, jnp.maximum(s - 1, 0), 0, 0), BF16),
        ((B, S, 512), (1, tm, 512), tile, BF16),
        ((B, 2, S, LANES), (1, 2, tm, LANES), lambda b, s: (b, 0, jnp.maximum(s - 1, 0), 0), F32),
        ((B, KV, S, 2 * LANES), (1, KV, tm, 2 * LANES),
         lambda b, s: (b, 0, jnp.maximum(s - 1, 0), 0), BF16),
        ((B, KV, n, dh, tm), (1, KV, 1, dh, tm),
         lambda b, s: (b, 0, jnp.maximum(s - 1, 0), 0, 0), BF16),
        ((B, KV, S + WINDOW, LANES), (1, KV, tm, LANES), lambda b, s: (b, 0, s, 0), BF16),
        ((B, KV, (S + WINDOW) // LANES, dh, LANES), (1, KV, tm // LANES, dh, LANES),
         lambda b, s: (b, 0, s, 0, 0), BF16),
        ((B, S, LANES), (1, tm, LANES), tile, F32),
    ]
    return pl.pallas_call(
        _inproj_kernel,
        grid=(B, n + 1),
        in_specs=[pl.BlockSpec((1, tm, D), tile),
                  pl.BlockSpec((1, D), lambda b, s: (0, 0)),
                  pl.BlockSpec((1, 1, D), lambda b, s: (b, 0, 0)),
                  pl.BlockSpec((1, 1, D), lambda b, s: (b, 0, 0)),
                  pl.BlockSpec((D, _W_PROJ), lambda b, s: (0, 0))],
        out_specs=[pl.BlockSpec(blk, idx) for _, blk, idx, _ in out_shapes],
        out_shape=[jax.ShapeDtypeStruct(shape, dt) for shape, _, _, dt in out_shapes],
        compiler_params=_params(("parallel", "arbitrary")),
        name="in_proj",
    )(x, g_pre.reshape(1, D), sc.reshape(B, 1, D), sh.reshape(B, 1, D), w_proj)


def _compress_kernel(x_ref, pos_ref, w1_ref, w2_ref, kca_ref, vct_ref, vc_ref):
    dh = HEAD_DIM
    n_rows = x_ref.shape[2] // CMP_STRIDE

    def mlp(t):
        first = jnp.zeros((n_rows, w1_ref.shape[3]), F32)
        second = jnp.zeros((n_rows, w1_ref.shape[3]), F32)
        for m in range(CMP_STRIDE):
            x = x_ref[0, t, pl.ds(m, n_rows, stride=CMP_STRIDE), :]
            first += jnp.dot((x + pos_ref[t, m:m + 1, :]).astype(BF16), w1_ref[t, m],
                             preferred_element_type=F32)
            second += jnp.dot((x + pos_ref[t, CMP_STRIDE + m:CMP_STRIDE + m + 1, :]).astype(BF16),
                              w1_ref[t, CMP_STRIDE + m], preferred_element_type=F32)
        hid = first + pltpu.roll(second, n_rows - 1, 0)
        return jnp.dot(jax.nn.gelu(hid).astype(BF16), w2_ref[t], preferred_element_type=F32)

    lane = lax.broadcasted_iota(jnp.int32, (n_rows, LANES), 1)
    cend = lax.broadcasted_iota(jnp.int32, (n_rows, 1), 0) * CMP_STRIDE + (CMP_BLOCK - 1)
    pcol = _pos_lanes(cend, lane)
    kc = mlp(0)
    low = lane < dh
    kca_ref[0, 0] = jnp.where(low, kc, pcol).astype(BF16)
    kca_ref[0, 1] = jnp.where(low, pcol, kc).astype(BF16)
    vc_ref[...] = mlp(1)
    vc_t = vc_ref[...].T
    for h in range(NSA_KV_HEADS):
        vct_ref[0, h] = vc_t[h * dh:(h + 1) * dh].astype(BF16)


def _block_diag(blocks):
    n, r, c = blocks.shape[-3:]
    out = jnp.einsum('...irc,ij->...irjc', blocks, jnp.eye(n, dtype=blocks.dtype))
    return out.reshape(blocks.shape[:-3] + (n * r, n * c))


def _compress(ckv, pos_k, pos_v, w1_k, w1_v, w2_k, w2_v):
    B, _, S, W = ckv.shape
    KV, dh = NSA_KV_HEADS, HEAD_DIM
    n_rows = S // CMP_STRIDE
    pos = jnp.stack([jnp.concatenate([p, p], axis=1) for p in (pos_k, pos_v)]).astype(F32)
    w1 = jnp.stack([w.reshape(CMP_BLOCK, dh, -1) for w in (w1_k, w1_v)])
    w1 = _block_diag(jnp.stack([w1, w1], axis=2)).astype(BF16)
    w2 = jnp.stack([w2_k, w2_v])
    w2 = _block_diag(jnp.stack([w2, w2], axis=1)).astype(BF16)
    return pl.pallas_call(
        _compress_kernel,
        grid=(B,),
        in_specs=[pl.BlockSpec((1, 2, S, W), lambda b: (b, 0, 0, 0)),
                  pl.BlockSpec(pos.shape, lambda b: (0, 0, 0)),
                  pl.BlockSpec(w1.shape, lambda b: (0, 0, 0, 0)),
                  pl.BlockSpec(w2.shape, lambda b: (0, 0, 0))],
        out_specs=[pl.BlockSpec((1, KV, n_rows, LANES), lambda b: (b, 0, 0, 0)),
                   pl.BlockSpec((1, KV, dh, n_rows), lambda b: (b, 0, 0, 0))],
        out_shape=[jax.ShapeDtypeStruct((B, KV, n_rows, LANES), BF16),
                   jax.ShapeDtypeStruct((B, KV, dh, n_rows), BF16)],
        scratch_shapes=[pltpu.VMEM((n_rows, W), F32)],
        compiler_params=_params(("parallel",)),
        name="nsa_compress",
    )(ckv, pos, w1, w2)


def _flash_scratch(dv, n_cols):
    tk = KEY_TILE
    return [pltpu.VMEM((tk, n_cols), F32), pltpu.VMEM((tk, n_cols), F32),
            pltpu.VMEM((tk, n_cols), BF16), pltpu.VMEM((tk, n_cols), BF16),
            pltpu.VMEM((1, n_cols), F32), pltpu.VMEM((1, n_cols), F32),
            pltpu.VMEM((1, n_cols), F32), pltpu.VMEM((1, n_cols), F32), pltpu.VMEM((dv, n_cols), F32)]


def _flash_causal(q0, tq, qa_ref, k_tile, v_tile, scratch, tiles=None):
    s_a, s_b, p_a, p_b, a_a, a_b, m_ref, l_ref, acc_ref = scratch
    tk = KEY_TILE
    n_cols = qa_ref.shape[0]
    assert tk % tq == 0
    n_full = q0 // tk
    n = n_full + 1 if tiles is None else tiles[1]
    m_ref[...] = jnp.full(m_ref.shape, NEG, F32)
    l_ref[...] = jnp.zeros(l_ref.shape, F32)
    acc_ref[...] = jnp.zeros(acc_ref.shape, F32)
    p_b[...] = jnp.zeros(p_b.shape, BF16)
    a_b[...] = jnp.ones(a_b.shape, F32)

    def key_tile(t):
        t = jnp.clip(t, 0, n - 1)
        j = jnp.where(t == 0, n_full, t - 1) if tiles is None else tiles[0][t]
        return j, pl.multiple_of(j * tk, tk)

    def scores(t, s_ref, masked):
        j, k0 = key_tile(t)
        kt = k_tile(j, k0)
        for c in range(0, n_cols, COL_SPLIT):
            s = _dot_nt(kt, qa_ref[c:c + COL_SPLIT, :])
            if masked:
                krow = lax.broadcasted_iota(jnp.int32, (tk, 1), 0)
                qcol = (lax.broadcasted_iota(jnp.int32, (1, COL_SPLIT), 1) + c) % tq
                s = jnp.where((krow + (k0 - q0)) <= qcol, s, NEG)
            s_ref[:, c:c + COL_SPLIT] = s

    def softmax(s_ref, p_ref, a_ref):
        ch = SOFTMAX_CHUNK_VREGS * 8 * LANES // n_cols
        chunks = [slice(c, c + ch) for c in range(0, tk, ch)]
        mx = None
        for rows in chunks:
            part = jnp.max(s_ref[rows, :].reshape(ch // 8, 8, n_cols), axis=0)
            mx = part if mx is None else jnp.maximum(mx, part)
        m_old = m_ref[...]
        m_new = jnp.maximum(m_old, jnp.max(mx, axis=0, keepdims=True))
        sm = None
        for rows in chunks:
            p = jnp.exp2(s_ref[rows, :] - m_new)
            p_ref[rows, :] = p.astype(BF16)
            part = jnp.sum(p.reshape(ch // 8, 8, n_cols), axis=0)
            sm = part if sm is None else sm + part
        alpha = jnp.exp2(m_old - m_new)
        a_ref[...] = alpha
        l_ref[...] = alpha * l_ref[...] + jnp.sum(sm, axis=0, keepdims=True)
        m_ref[...] = m_new

    def values(t, p_ref, a_ref):
        j, k0 = key_tile(t)
        vt = v_tile(j, k0)
        for c in range(0, n_cols, COL_SPLIT):
            cs = slice(c, c + COL_SPLIT)
            acc_ref[:, cs] = a_ref[:, cs] * acc_ref[:, cs] + jnp.dot(vt, p_ref[:, cs], preferred_element_type=F32)

    scores(0, s_a, True)

    def pair(u, carry):
        i = 2 * u
        scores(i + 1, s_b, False)
        softmax(s_a, p_a, a_a)
        values(i - 1, p_b, a_b)

        @pl.when(i + 1 < n)
        def _():
            scores(i + 2, s_a, False)
            softmax(s_b, p_b, a_b)
            values(i, p_a, a_a)

        return carry

    lax.fori_loop(0, (n + 1) // 2, pair, 0)

    @pl.when(n % 2 == 1)
    def _():
        values(n - 1, p_a, a_a)

    @pl.when(n % 2 == 0)
    def _():
        values(n - 1, p_b, a_b)

    return acc_ref, l_ref


def _softmax_cols(s, bias):
    e, inv = _softmax_cols_unnormalised(s, bias)
    return e * inv


def _softmax_cols_unnormalised(s, bias):
    s = s + bias
    m = jnp.max(s, axis=0, keepdims=True)
    e = jnp.exp2(s - m)
    inv = jnp.where(m > 0.5 * NEG, 1.0 / jnp.sum(e, axis=0, keepdims=True), 0.0)
    return e, inv


def _diff_kernel(qx_ref, lam_ref, gsub_ref, q_ref, k_ref, v_ref, o_ref, qa_ref, *scratch, lambda_init):
    tq = DIFF_TQ
    q0 = pl.program_id(2) * tq
    q = q_ref[0].astype(F32)
    low = lax.broadcasted_iota(jnp.int32, (tq, LANES), 1) < HEAD_DIM
    ext = jnp.broadcast_to(qx_ref[0], (tq, LANES)).astype(BF16)
    qa_ref[0:tq, :LANES] = jnp.where(low, q, 0.0).astype(BF16)
    qa_ref[tq:, :LANES] = jnp.where(low, 0.0, q).astype(BF16)
    qa_ref[0:tq, LANES:] = ext
    qa_ref[tq:, LANES:] = ext

    acc_ref, l_ref = _flash_causal(q0, tq, qa_ref,
                                   lambda j, k0: k_ref[0, pl.ds(k0, KEY_TILE), :],
                                   lambda j, k0: v_ref[0, 0, j],
                                   scratch)

    lv = lam_ref[...]
    lam = (jnp.exp(jnp.sum(lv[0:1] * lv[1:2], axis=-1, keepdims=True))
           - jnp.exp(jnp.sum(lv[2:3] * lv[3:4], axis=-1, keepdims=True)) + lambda_init)
    o_t = acc_ref[...] / l_ref[...]
    o_t = o_t[:, :tq] - lam * o_t[:, tq:]
    o_t = o_t * lax.rsqrt(jnp.mean(o_t * o_t, axis=0, keepdims=True) + EPS)
    o_ref[0] = (o_t.T * gsub_ref[...] * (1.0 - lambda_init)).astype(o_ref.dtype)


def _diff_attention(dq, dk, dv_t, lam_vecs, g_subln, lambda_init):
    B, S, _ = dq.shape
    tq, tk = DIFF_TQ, KEY_TILE
    H = DIFF_HEADS
    qx = jnp.asarray(_slope_cols(_alibi_slopes(H), LANES, 0)).reshape(H, 1, LANES)
    lam = jnp.zeros((8, LANES), F32).at[:4, :HEAD_DIM].set(jnp.stack(lam_vecs))
    return pl.pallas_call(
        functools.partial(_diff_kernel, lambda_init=lambda_init),
        grid=(B, H, S // tq),
        in_specs=[pl.BlockSpec((1, 1, LANES), lambda b, h, i: (h, 0, 0)),
                  pl.BlockSpec((8, LANES), lambda b, h, i: (0, 0)),
                  pl.BlockSpec((1, LANES), lambda b, h, i: (0, 0)),
                  pl.BlockSpec((1, tq, LANES), lambda b, h, i: (b, i, h)),
                  pl.BlockSpec((1, S, 2 * LANES), lambda b, h, i: (b, 0, h)),
                  pl.BlockSpec((1, 1, S // tk, LANES, tk), lambda b, h, i: (b, h, 0, 0, 0))],
        out_specs=pl.BlockSpec((1, tq, LANES), lambda b, h, i: (b, i, h)),
        out_shape=jax.ShapeDtypeStruct((B, S, H * LANES), BF16),
        scratch_shapes=[pltpu.VMEM((2 * tq, 2 * LANES), BF16)] + _flash_scratch(LANES, 2 * tq),
        compiler_params=_params(("parallel", "parallel", "parallel")),
        name="diff_attention",
    )(qx, lam, g_subln.reshape(1, LANES).astype(F32), dq, dk, dv_t)


def _nsa_kernel(qx_ref, q_ref, kc_ref, vc_ref, ka_ref, sv_ref, wk_ref, wv_ref, g_ref, ov_ref, cb_ref, wb_ref,
                o_ref, qa_ref, out_ref, tiles_ref, *scratch):
    tq, G, dh = NSA_TQ, NSA_GROUP, HEAD_DIM
    n_cmp = kc_ref.shape[2]
    n_win = WINDOW + tq
    grp = pl.program_id(1)
    qi = pl.program_id(2)
    q0 = pl.multiple_of(qi * tq, tq)
    cols = [slice(i * tq, (i + 1) * tq) for i in range(G)]
    g_t = g_ref[0].T

    def gate(i, branch):
        a = 3 * i + branch
        b = 3 * (G + i) + branch
        return jnp.where(grp == 0, g_t[a:a + 1, :], g_t[b:b + 1, :])

    lane_q = lax.broadcasted_iota(jnp.int32, (tq, LANES), 1)
    mine = (lane_q >= grp * dh) & (lane_q < (grp + 1) * dh)
    qx = qx_ref[0]
    for i in range(G):
        q = q_ref[0, :, i * LANES:(i + 1) * LANES].astype(F32)
        qa_ref[cols[i], :LANES] = jnp.where(mine, q, jnp.broadcast_to(qx[i:i + 1], (tq, LANES))).astype(BF16)
    qs = qa_ref[:, :LANES]

    kw = wk_ref[0, 0, pl.ds(q0, n_win), :]
    vw = jnp.concatenate([wv_ref[0, 0, (tq // LANES) * qi + c] for c in range(n_win // LANES)], axis=1)
    e, inv = _softmax_cols_unnormalised(_dot_nt(kw, qs), jnp.tile(wb_ref[...], (1, G)))
    ow = jnp.dot(vw, e.astype(BF16), preferred_element_type=F32) * inv

    cbias = cb_ref[pl.ds(pl.multiple_of(n_cmp - (tq // CMP_STRIDE) * qi, 8), n_cmp), :]
    p = _softmax_cols(_dot_nt(kc_ref[0, 0], qs), jnp.tile(cbias, (1, G)))
    oc = jnp.dot(vc_ref[0, 0], p.astype(BF16), preferred_element_type=F32)
    for i in range(G):
        out_ref[i * dh:(i + 1) * dh, :] = gate(i, 0) * oc[:, cols[i]] + gate(i, 2) * ow[:, cols[i]]

    psum = p[:, cols[0]]
    for i in range(1, G):
        psum = psum + p[:, cols[i]]
    hi = psum.astype(BF16)
    r1 = psum - hi.astype(F32)
    mid = r1.astype(BF16)
    lo = (r1 - mid.astype(F32)).astype(BF16)
    ov = ov_ref[...]
    imp = (jnp.dot(ov, hi, preferred_element_type=F32) + jnp.dot(ov, mid, preferred_element_type=F32)
           + jnp.dot(ov, lo, preferred_element_type=F32))
    n_slc = imp.shape[0]
    blk = lax.broadcasted_iota(jnp.int32, (n_slc, tq), 0)
    cur = (lax.broadcasted_iota(jnp.int32, (1, tq), 1) + q0) // SLC_BLOCK
    forced = (blk == 0) | (blk == cur) | (blk == cur - 1)
    score = jnp.where(forced, -jnp.inf, jnp.where(blk <= cur, imp, NEG))
    blkf = blk.astype(F32)
    for _ in range(SLC_TOP - 3):
        mx = jnp.max(score, axis=0, keepdims=True)
        idx = jnp.min(jnp.where(score == mx, blkf, float(n_slc)), axis=0, keepdims=True)
        score = jnp.where(blkf == idx, -jnp.inf, score)
    sel = jnp.where(score == -jnp.inf, 1.0, 0.0)
    sel_t = sel.T
    blockmask = ((sel_t - 1.0) * 1e30).astype(BF16)
    for i in range(G):
        qa_ref[cols[i], LANES:] = blockmask

    per_tile = KEY_TILE // SLC_BLOCK
    n_tiles = n_slc // per_tile
    used = jnp.max(jnp.max(sel, axis=1, keepdims=True).reshape(n_tiles, per_tile, 1), axis=1)
    bit = jnp.where(used > 0.0, 1 << lax.broadcasted_iota(jnp.int32, (n_tiles, 1), 0), 0)
    bits = jnp.sum(bit, axis=0, keepdims=True)[0, 0]
    n_full = q0 // KEY_TILE
    tiles_ref[0] = n_full
    count = jnp.int32(1)
    for j in range(n_tiles):
        tiles_ref[count] = j
        count = count + jnp.where(j < n_full, (bits >> j) & 1, 0)

    acc_ref, l_ref = _flash_causal(q0, tq, qa_ref,
                                   lambda j, k0: ka_ref[0, 0, pl.ds(k0, KEY_TILE), :],
                                   lambda j, k0: sv_ref[0, 0, j],
                                   scratch, tiles=(tiles_ref, count))
    os_ = acc_ref[...] / l_ref[...]
    for i in range(G):
        out_ref[i * dh:(i + 1) * dh, :] += gate(i, 1) * os_[:, cols[i]]

    o_ref[0] = out_ref[...].T.astype(o_ref.dtype)


def _nsa_attention(nq, kca, vc_t, ska, sv_t, wka, wv_t, gates):
    B, S, _ = nq.shape
    tq, tk, G, dh = NSA_TQ, KEY_TILE, NSA_GROUP, HEAD_DIM
    KV = NSA_KV_HEADS
    n_cmp = kca.shape[2]
    n_slc = S // SLC_BLOCK
    Sp = S + WINDOW
    slopes = _alibi_slopes(NSA_HEADS)
    qx = jnp.asarray(np.stack([_slope_cols(slopes[:G], LANES, dh), _slope_cols(slopes[G:], LANES, 0)]))
    overlap_t = _overlap_matrix(S).T
    r = np.arange(tq)[None, :]
    j = np.arange(2 * n_cmp)[:, None]
    cmp_band = jnp.asarray(np.where(CMP_STRIDE * (j - n_cmp) + CMP_BLOCK - 1 <= r, 0.0, NEG).astype(np.float32))
    c = np.arange(WINDOW + tq)[:, None]
    d = r - c + WINDOW
    win_band = jnp.asarray(np.where((d >= 0) & (d < WINDOW), 0.0, NEG).astype(np.float32))
    const2 = lambda b, g, i: (0, 0)
    per_bg = lambda b, g, i: (b, g, 0, 0)
    per_bg5 = lambda b, g, i: (b, g, 0, 0, 0)
    return pl.pallas_call(
        _nsa_kernel,
        grid=(B, KV, S // tq),
        in_specs=[pl.BlockSpec((1, G, LANES), lambda b, g, i: (g, 0, 0)),
                  pl.BlockSpec((1, tq, G * LANES), lambda b, g, i: (b, i, 0)),
                  pl.BlockSpec((1, 1, n_cmp, LANES), per_bg),
                  pl.BlockSpec((1, 1, dh, n_cmp), per_bg),
                  pl.BlockSpec((1, 1, S, 2 * LANES), per_bg),
                  pl.BlockSpec((1, 1, S // tk, dh, tk), per_bg5),
                  pl.BlockSpec((1, 1, Sp, LANES), per_bg),
                  pl.BlockSpec((1, 1, Sp // LANES, dh, LANES), per_bg5),
                  pl.BlockSpec((1, tq, LANES), lambda b, g, i: (b, i, 0)),
                  pl.BlockSpec((n_slc, n_cmp), const2),
                  pl.BlockSpec(cmp_band.shape, const2), pl.BlockSpec(win_band.shape, const2)],
        out_specs=pl.BlockSpec((1, tq, G * dh), lambda b, g, i: (b, i, g)),
        out_shape=jax.ShapeDtypeStruct((B, S, NSA_HEADS * dh), BF16),
        scratch_shapes=[pltpu.VMEM((G * tq, 2 * LANES), BF16),
                        pltpu.VMEM((G * dh, tq), F32),
                        pltpu.SMEM((2 * (S // tk),), jnp.int32)]
        + _flash_scratch(dh, G * tq),
        compiler_params=_params(("parallel", "parallel", "parallel")),
        name="nsa_attention",
    )(qx, nq, kca, vc_t, ska, sv_t, wka, wv_t, gates, overlap_t, cmp_band, win_band)


def _outproj_kernel(oa_ref, ob_ref, x_ref, wo_ref, gpm_ref, gta_ref, gpf_ref, sc_ref, sh_ref, wr_ref, br_ref,
                    x1_ref, h2_ref, rt_ref):
    half = oa_ref.shape[1]
    mix = (jnp.dot(oa_ref[...], wo_ref[:half], preferred_element_type=F32)
           + jnp.dot(ob_ref[...], wo_ref[half:], preferred_element_type=F32))
    x1 = x_ref[...] + gta_ref[0] * (_rms(mix) * gpm_ref[...])
    x1_ref[...] = x1
    h2 = _rms(x1) * gpf_ref[...] * (1.0 + sc_ref[0]) + sh_ref[0]
    h2_ref[...] = h2
    h_hi = h2.astype(BF16)
    h_lo = (h2 - h_hi.astype(F32)).astype(BF16)
    logits = (jnp.dot(h_hi, wr_ref[0], preferred_element_type=F32)
              + jnp.dot(h_lo, wr_ref[0], preferred_element_type=F32)
              + jnp.dot(h_hi, wr_ref[1], preferred_element_type=F32)) + br_ref[...]
    tm = logits.shape[0]
    lane = lax.broadcasted_iota(jnp.int32, (tm, LANES), 1)
    lanef = lane.astype(F32)
    is_grp = (lane >= N_EXPERTS) & (lane < N_EXPERTS + N_GROUPS)
    lg = jnp.where(is_grp, logits, -jnp.inf)
    mg = jnp.max(lg, axis=-1, keepdims=True)
    g_lane = jnp.min(jnp.where(lg == mg, lanef, 1e9), axis=-1, keepdims=True)
    p_grp = 1.0 / jnp.sum(jnp.where(is_grp, jnp.exp(lg - mg), 0.0), axis=-1, keepdims=True)
    grp = g_lane.astype(jnp.int32) - N_EXPERTS
    in_grp = (lane < N_EXPERTS) & ((lane // EXPERTS_PER_GROUP) == grp)
    le = jnp.where(in_grp, logits, -jnp.inf)
    v1 = jnp.max(le, axis=-1, keepdims=True)
    i1 = jnp.min(jnp.where(le == v1, lanef, 1e9), axis=-1, keepdims=True)
    le2 = jnp.where(lanef == i1, -jnp.inf, le)
    v2 = jnp.max(le2, axis=-1, keepdims=True)
    i2 = jnp.min(jnp.where(le2 == v2, lanef, 1e9), axis=-1, keepdims=True)
    e2 = jnp.exp(v2 - v1)
    w1 = p_grp / (1.0 + e2)
    w2 = p_grp * e2 / (1.0 + e2)
    rt_ref[...] = jnp.where(lane == 0, i1, jnp.where(lane == 1, i2, jnp.where(lane == 2, w1,
                            jnp.where(lane == 3, w2, 0.0))))


def _out_proj(o_a, o_b, x2, w_out, g_post_mix, gt_a, g_pre_ffn, sc_f, sh_f, w_r, b_r, B, S):
    T, D = x2.shape
    tm = ROW_TILE
    per_b = S // tm
    row = lambda i: (i, 0)
    bat = lambda i: (i // per_b, 0, 0)
    const = lambda i: (0, 0)
    return pl.pallas_call(
        _outproj_kernel,
        grid=(T // tm,),
        in_specs=[pl.BlockSpec((tm, o_a.shape[1]), row), pl.BlockSpec((tm, o_b.shape[1]), row),
                  pl.BlockSpec((tm, D), row), pl.BlockSpec((D, D), const),
                  pl.BlockSpec((1, D), const), pl.BlockSpec((1, 1, D), bat), pl.BlockSpec((1, D), const),
                  pl.BlockSpec((1, 1, D), bat), pl.BlockSpec((1, 1, D), bat),
                  pl.BlockSpec((2, D, LANES), lambda i: (0, 0, 0)), pl.BlockSpec((1, LANES), const)],
        out_specs=[pl.BlockSpec((tm, D), row), pl.BlockSpec((tm, D), row), pl.BlockSpec((tm, LANES), row)],
        out_shape=[jax.ShapeDtypeStruct((T, D), F32), jax.ShapeDtypeStruct((T, D), F32),
                   jax.ShapeDtypeStruct((T, LANES), F32)],
        compiler_params=_params(("parallel",)),
        name="out_proj_router",
    )(o_a, o_b, x2, w_out, g_post_mix.reshape(1, D), gt_a.reshape(B, 1, D), g_pre_ffn.reshape(1, D),
      sc_f.reshape(B, 1, D), sh_f.reshape(B, 1, D), w_r, b_r)


def _slots_kernel(rt_ref, slot_ref, blk_ref, cnt_ref, base_ref, *, n_blk_rows):
    phase = pl.program_id(0)
    step = pl.program_id(1)
    tm = rt_ref.shape[0]
    rt = rt_ref[...]
    lane = lax.broadcasted_iota(jnp.int32, (tm, LANES), 1).astype(F32)
    i1 = rt[:, 0:1]
    i2 = rt[:, 1:2]
    hit1 = lane == i1
    hit2 = lane == i2
    onehot = jnp.where(hit1, 1.0, jnp.where(hit2, 1.0, 0.0))
    colsum = jnp.sum(onehot, axis=0, keepdims=True)

    @pl.when((phase == 0) & (step == 0))
    def _():
        cnt_ref[...] = jnp.zeros(cnt_ref.shape, F32)

    @pl.when(phase == 0)
    def _():
        cnt_ref[...] += colsum

    @pl.when((phase == 1) & (step == 0))
    def _():
        cnt = cnt_ref[...]
        padded = jnp.floor((cnt + (MOE_ROWS - 1)) / MOE_ROWS) * MOE_ROWS
        lane1 = lax.broadcasted_iota(jnp.int32, (1, LANES), 1)
        incl = padded
        sh = 1
        while sh < LANES:
            incl = incl + jnp.where(lane1 >= sh, pltpu.roll(incl, sh, 1), 0.0)
            sh *= 2
        base_ref[...] = incl - padded
        rows = lax.broadcasted_iota(jnp.int32, (n_blk_rows, LANES), 0).astype(F32) * MOE_ROWS
        lane2 = lax.broadcasted_iota(jnp.int32, (n_blk_rows, LANES), 1)
        ended = jnp.where(lane2 < N_EXPERTS, jnp.where(incl <= rows, 1.0, 0.0), 0.0)
        blk_e = jnp.minimum(jnp.sum(ended, axis=-1, keepdims=True), N_EXPERTS - 1.0)
        total = jnp.sum(jnp.where(lane1 == N_EXPERTS - 1, incl, 0.0), axis=-1, keepdims=True)
        n_used = total / MOE_ROWS
        blk_ref[...] = jnp.where(lane2 == 0, blk_e, jnp.where(lane2 == 1, n_used, 0.0)).astype(jnp.int32)

    @pl.when(phase == 1)
    def _():
        r = lax.broadcasted_iota(jnp.int32, (tm, tm), 0)
        c = lax.broadcasted_iota(jnp.int32, (tm, tm), 1)
        tri = jnp.where(c < r, 1.0, 0.0).astype(BF16)
        before = jnp.dot(tri, onehot.astype(BF16), preferred_element_type=F32) + base_ref[...]
        s1 = jnp.sum(jnp.where(hit1, before, 0.0), axis=-1, keepdims=True)
        s2 = jnp.sum(jnp.where(hit2, before, 0.0), axis=-1, keepdims=True)
        lane_i = lax.broadcasted_iota(jnp.int32, (tm, LANES), 1)
        slot_ref[...] = jnp.where(lane_i == 0, s1, jnp.where(lane_i == 1, s2, 0.0)).astype(jnp.int32)
        base_ref[...] += colsum


def _slots(route, n_blocks):
    T = route.shape[0]
    tm = ROW_TILE
    n_blk_rows = -(-n_blocks // 8) * 8
    slot, blk = pl.pallas_call(
        functools.partial(_slots_kernel, n_blk_rows=n_blk_rows),
        grid=(2, T // tm),
        in_specs=[pl.BlockSpec((tm, LANES), lambda p, i: (i, 0))],
        out_specs=[pl.BlockSpec((tm, LANES), lambda p, i: (i * p, 0)),
                   pl.BlockSpec((n_blk_rows, LANES), lambda p, i: (0, 0))],
        out_shape=[jax.ShapeDtypeStruct((T, LANES), jnp.int32),
                   jax.ShapeDtypeStruct((n_blk_rows, LANES), jnp.int32)],
        scratch_shapes=[pltpu.VMEM((1, LANES), F32), pltpu.VMEM((1, LANES), F32)],
        compiler_params=_params(("arbitrary", "arbitrary")),
        name="moe_slots",
    )(route)
    return slot[:, 0], slot[:, 1], blk[:n_blocks, 0], blk[0:1, 1]


def _dispatch_kernel(s1_ref, s2_ref, h_ref, xz_ref, xb_ref, sem):
    del xz_ref
    tm = h_ref.shape[0]
    base = pl.program_id(0) * tm

    def issue(r, carry):
        src = h_ref.at[pl.ds(r, 1), :]
        pltpu.make_async_copy(src, xb_ref.at[pl.ds(s1_ref[base + r], 1), :], sem).start()
        pltpu.make_async_copy(src, xb_ref.at[pl.ds(s2_ref[base + r], 1), :], sem).start()
        return carry

    lax.fori_loop(0, tm, issue, 0, unroll=DMA_UNROLL)

    def drain(r, carry):
        for _ in range(TOP_K):
            pltpu.make_async_copy(h_ref.at[pl.ds(0, 1), :], xb_ref.at[pl.ds(0, 1), :], sem).wait()
        return carry

    lax.fori_loop(0, tm, drain, 0, unroll=DMA_UNROLL)


def _dispatch(h2, slot1, slot2, n_rows):
    T, D = h2.shape
    tm = ROW_TILE
    xz = jnp.zeros((n_rows, D), h2.dtype)
    return pl.pallas_call(
        _dispatch_kernel,
        grid_spec=pltpu.PrefetchScalarGridSpec(
            num_scalar_prefetch=2,
            grid=(T // tm,),
            in_specs=[pl.BlockSpec((tm, D), lambda i, a, b: (i, 0)),
                      pl.BlockSpec(memory_space=pl.ANY)],
            out_specs=pl.BlockSpec(memory_space=pl.ANY),
            scratch_shapes=[pltpu.SemaphoreType.DMA(())]),
        out_shape=jax.ShapeDtypeStruct((n_rows, D), h2.dtype),
        input_output_aliases={3: 0},
        compiler_params=_params(("arbitrary",)),
        name="moe_dispatch",
    )(slot1, slot2, h2, xz)


def _expert_kernel(be_ref, nu_ref, x_ref, wg_ref, wu_ref, wd_ref, y_ref):
    del be_ref
    i = pl.program_id(0)

    @pl.when(i >= nu_ref[0])
    def _():
        y_ref[...] = jnp.zeros(y_ref.shape, y_ref.dtype)

    @pl.when(i < nu_ref[0])
    def _():
        x = x_ref[...].astype(BF16)
        g = jnp.dot(x, wg_ref[0].astype(BF16), preferred_element_type=F32)
        u = jnp.dot(x, wu_ref[0].astype(BF16), preferred_element_type=F32)
        hid = (g * jax.nn.sigmoid(g) * u).astype(BF16)
        y_ref[...] = jnp.dot(hid, wd_ref[0].astype(BF16), preferred_element_type=F32).astype(y_ref.dtype)


def _experts(xbuf, blk_e, n_used, w_gate, w_up, w_down):
    D = xbuf.shape[1]
    E, _, De = w_gate.shape
    n_blocks = blk_e.shape[0]
    R = n_blocks * MOE_ROWS
    blk = lambda i, be, nu: (jnp.minimum(i, nu[0] - 1), 0)
    wsel = lambda i, be, nu: (be[jnp.minimum(i, nu[0] - 1)], 0, 0)
    return pl.pallas_call(
        _expert_kernel,
        grid_spec=pltpu.PrefetchScalarGridSpec(
            num_scalar_prefetch=2,
            grid=(n_blocks,),
            in_specs=[pl.BlockSpec((MOE_ROWS, D), blk),
                      pl.BlockSpec((1, D, De), wsel), pl.BlockSpec((1, D, De), wsel),
                      pl.BlockSpec((1, De, D), wsel)],
            out_specs=pl.BlockSpec((MOE_ROWS, D), lambda i, be, nu: (i, 0))),
        out_shape=jax.ShapeDtypeStruct((R, D), F32),
        compiler_params=_params(("arbitrary",)),
        name="moe_experts",
    )(blk_e, n_used, xbuf, w_gate, w_up, w_down)


def _combine_kernel(s1_ref, s2_ref, yb_ref, rt_ref, x1_ref, gtf_ref, gpo_ref, o_ref, ya_ref, yc_ref, sem):
    tm = x1_ref.shape[0]
    i = pl.program_id(0)

    def row_copy(src_row, dst_ref, slot, r):
        return pltpu.make_async_copy(yb_ref.at[pl.ds(src_row, 1), :], dst_ref.at[slot, pl.ds(r, 1), :], sem.at[slot])

    def gather(step, slot):
        def issue(r, carry):
            row_copy(s1_ref[step * tm + r], ya_ref, slot, r).start()
            row_copy(s2_ref[step * tm + r], yc_ref, slot, r).start()
            return carry

        lax.fori_loop(0, tm, issue, 0, unroll=DMA_UNROLL)

    @pl.when(i == 0)
    def _():
        gather(0, 0)

    @pl.when(i + 1 < pl.num_programs(0))
    def _():
        gather(i + 1, (i + 1) % 2)

    slot = i % 2

    def drain(r, carry):
        for _ in range(TOP_K):
            row_copy(0, ya_ref, slot, 0).wait()
        return carry

    lax.fori_loop(0, tm, drain, 0, unroll=DMA_UNROLL)
    rt = rt_ref[...]
    f = rt[:, 2:3] * ya_ref[slot] + rt[:, 3:4] * yc_ref[slot]
    o_ref[...] = x1_ref[...] + gtf_ref[0] * (_rms(f) * gpo_ref[...])


def _combine(ybuf, slot1, slot2, route, x1, gt_f, g_post_ffn, B, S):
    T, D = x1.shape
    tm = ROW_TILE
    per_b = S // tm
    row = lambda i, a, b: (i, 0)
    return pl.pallas_call(
        _combine_kernel,
        grid_spec=pltpu.PrefetchScalarGridSpec(
            num_scalar_prefetch=2,
            grid=(T // tm,),
            in_specs=[pl.BlockSpec(memory_space=pl.ANY),
                      pl.BlockSpec((tm, LANES), row), pl.BlockSpec((tm, D), row),
                      pl.BlockSpec((1, 1, D), lambda i, a, b: (i // per_b, 0, 0)),
                      pl.BlockSpec((1, D), lambda i, a, b: (0, 0))],
            out_specs=pl.BlockSpec((tm, D), row),
            scratch_shapes=[pltpu.VMEM((2, tm, D), F32), pltpu.VMEM((2, tm, D), F32),
                            pltpu.SemaphoreType.DMA((2,))]),
        out_shape=jax.ShapeDtypeStruct((T, D), F32),
        compiler_params=_params(("arbitrary",)),
        name="moe_combine",
    )(slot1, slot2, ybuf, route, x1, gt_f.reshape(B, 1, D), g_post_ffn.reshape(1, D))


def _overlap_matrix(S):
    n_rows = S // CMP_STRIDE
    n_cmp = (S - CMP_BLOCK) // CMP_STRIDE + 1
    n_slc = S // SLC_BLOCK
    cstart = np.arange(n_rows) * CMP_STRIDE
    sstart = np.arange(n_slc) * SLC_BLOCK
    ov = (cstart[:, None] < sstart[None, :] + SLC_BLOCK) & (cstart[:, None] + CMP_BLOCK > sstart[None, :])
    ov &= (np.arange(n_rows) < n_cmp)[:, None]
    return jnp.asarray(ov.astype(np.float32), dtype=BF16)


def kernel(x, c, w_ada, b_ada, g_pre_mix, g_post_mix, g_pre_ffn, g_post_ffn, w_in, lam_q1, lam_k1, lam_q2,
           lam_k2, g_subln, cmp_pos_k, cmp_pos_v, cmp_w1_k, cmp_w1_v, cmp_w2_k, cmp_w2_v, w_out,
           w_router_grp, b_router_grp, w_router_exp, b_router_exp, w_exp_gate, w_exp_up, w_exp_down):
    B, S, D = x.shape
    T = B * S
    depth = w_ada.shape[0]
    assert S // SLC_BLOCK == LANES and D == D_MODEL
    n_blocks = (T * TOP_K) // MOE_ROWS + N_EXPERTS
    for l in range(depth):
        lambda_init = 0.8 - 0.6 * math.exp(-0.3 * l)
        mod = _adaln(c, w_ada[l], b_ada[l])
        sh_a, sc_a, gt_a, sh_f, sc_f, gt_f = jnp.split(mod, 6, axis=-1)

        dq, dk, dv_t, nq, ckv, ska, sv_t, wka, wv_t, gates = _in_proj(x, g_pre_mix[l], sc_a, sh_a,
                                                                      _build_w_proj(w_in[l]))
        kca, vc_t = _compress(ckv, cmp_pos_k[l], cmp_pos_v[l], cmp_w1_k[l], cmp_w1_v[l], cmp_w2_k[l], cmp_w2_v[l])
        o_a = _diff_attention(dq, dk, dv_t, (lam_q1[l], lam_k1[l], lam_q2[l], lam_k2[l]), g_subln[l], lambda_init)
        o_b = _nsa_attention(nq, kca, vc_t, ska, sv_t, wka, wv_t, gates)

        w_r = jnp.zeros((D, LANES), F32).at[:, :N_EXPERTS].set(w_router_exp[l]).at[
            :, N_EXPERTS:N_EXPERTS + N_GROUPS].set(w_router_grp[l])
        b_r = jnp.zeros((1, LANES), F32).at[0, :N_EXPERTS].set(b_router_exp[l]).at[
            0, N_EXPERTS:N_EXPERTS + N_GROUPS].set(b_router_grp[l])
        w_r_hi = w_r.astype(BF16)
        w_r = jnp.stack([w_r_hi, (w_r - w_r_hi.astype(F32)).astype(BF16)])
        x1, h2, route = _out_proj(o_a.reshape(T, -1), o_b.reshape(T, -1), x.reshape(T, D), w_out[l].astype(BF16),
                                  g_post_mix[l], gt_a, g_pre_ffn[l], sc_f, sh_f, w_r, b_r, B, S)
        slot1, slot2, blk_e, n_used = _slots(route, n_blocks)
        xbuf = _dispatch(h2, slot1, slot2, n_blocks * MOE_ROWS)
        ybuf = _experts(xbuf, blk_e, n_used, w_exp_gate[l], w_exp_up[l], w_exp_down[l])
        x = _combine(ybuf, slot1, slot2, route, x1, gt_f, g_post_ffn[l], B, S).reshape(B, S, D)
    return x
```

```python
import functools
import math

import numpy as np
import jax
import jax.numpy as jnp
from jax import lax
from jax.experimental import pallas as pl
from jax.experimental.pallas import tpu as pltpu

F32 = jnp.float32
BF16 = jnp.bfloat16
HIGHEST = lax.Precision.HIGHEST

D_MODEL = 1024
HEAD_DIM = 64
LANES = 128
DIFF_HEADS = 4
NSA_HEADS = 8
NSA_KV_HEADS = 2
NSA_GROUP = NSA_HEADS // NSA_KV_HEADS
CMP_BLOCK = 32
CMP_STRIDE = 16
CMP_HIDDEN = 2 * HEAD_DIM
SLC_BLOCK = 64
SLC_TOP = 16
WINDOW = 512
N_GROUPS = 8
EXPERTS_PER_GROUP = 8
N_EXPERTS = N_GROUPS * EXPERTS_PER_GROUP
TOP_K = 2
D_EXPERT = D_MODEL // 2
NEG = -1e30
FORCE_BONUS = 1e4
EPS = 1e-6

ROW_TILE = 512
KEY_TILE = 512
DIFF_TQ = 512
NSA_TQ = 256
COL_SPLIT = 512
SOFTMAX_CHUNK_VREGS = 16
MOE_ROWS = 256
DMA_UNROLL = 32
VMEM_LIMIT = 48 * 1024 * 1024
assert ROW_TILE == KEY_TILE == WINDOW

_OFF_DQ, _OFF_DK, _OFF_DV, _OFF_NQ = 0, 512, 1024, 1536
_OFF_KV = 2048
_OFF_NG = 2816
_N_GATES = 3 * NSA_HEADS
_W_PROJ = 2048 + 768 + LANES


def _alibi_slopes(n):
    return np.array([2.0 ** (-8.0 * (i + 1) / n) for i in range(n)], dtype=np.float32)


def _dot_nt(a, b):
    return lax.dot_general(a, b, (((1,), (1,)), ((), ())), preferred_element_type=F32)


def _rms(v):
    return v * lax.rsqrt(jnp.mean(v * v, axis=-1, keepdims=True) + EPS)


def _params(sem, vmem=VMEM_LIMIT):
    return pltpu.CompilerParams(dimension_semantics=sem, vmem_limit_bytes=vmem)


LOG2E = np.float32(1.0 / math.log(2.0))
N_SLOPE_PIECES = 3
MASK_LANE = 2 * N_SLOPE_PIECES


def _pos_lanes(pos, lane):
    half = lane % HEAD_DIM
    return jnp.where(half < MASK_LANE, jnp.where(half % 2 == 0, pos // LANES, pos % LANES), 0).astype(F32)


def _slope_cols(slopes, width, at):
    cols = np.zeros((len(slopes), width), np.float32)
    rest = (np.asarray(slopes, np.float32) * LOG2E).astype(np.float32)
    for i in range(N_SLOPE_PIECES):
        piece = rest.astype(BF16).astype(np.float32)
        cols[:, at + 2 * i] = piece * LANES
        cols[:, at + 2 * i + 1] = piece
        rest = rest - piece
    cols[:, at + MASK_LANE] = NEG
    return cols


def _adaln_kernel(c_ref, w_ref, b_ref, o_ref):
    c = c_ref[...]
    s = c * jax.nn.sigmoid(c)
    o_ref[...] = jnp.dot(s, w_ref[...], preferred_element_type=F32, precision=HIGHEST) + b_ref[...]


def _adaln(c, w_ada, b_ada):
    B, D = c.shape
    rows = 8
    cp = jnp.zeros((rows, D), F32).at[:B].set(c)
    n_out = w_ada.shape[1]
    out = pl.pallas_call(
        _adaln_kernel,
        grid=(n_out // D,),
        in_specs=[pl.BlockSpec((rows, D), lambda j: (0, 0)),
                  pl.BlockSpec((D, D), lambda j: (0, j)),
                  pl.BlockSpec((1, D), lambda j: (0, j))],
        out_specs=pl.BlockSpec((rows, D), lambda j: (0, j)),
        out_shape=jax.ShapeDtypeStruct((rows, n_out), F32),
        compiler_params=_params(("parallel",)),
        name="adaln",
    )(cp, w_ada, b_ada.reshape(1, n_out))
    return out[:B]


def _inproj_kernel(x_ref, g_ref, sc_ref, sh_ref, w_ref,
                   dq_ref, dk_ref, dvt_ref, nq_ref, ckv_ref, ska_ref, svt_ref, wka_ref, wvt_ref, gt_ref):
    step = pl.program_id(1)
    tm = x_ref.shape[1]
    dh = HEAD_DIM

    @pl.when(step == 0)
    def _():
        lane = lax.broadcasted_iota(jnp.int32, (tm, LANES), 1)
        for g in range(NSA_KV_HEADS):
            wka_ref[0, g] = jnp.where(lane == (1 - g) * dh + MASK_LANE, 1.0, 0.0).astype(BF16)
        wvt_ref[...] = jnp.zeros(wvt_ref.shape, wvt_ref.dtype)

    @pl.when(step > 0)
    def _():
        h = (_rms(x_ref[0]) * g_ref[...] * (1.0 + sc_ref[0]) + sh_ref[0]).astype(BF16)

        def proj(off, width):
            return jnp.dot(h, w_ref[:, off:off + width], preferred_element_type=F32)

        lane = lax.broadcasted_iota(jnp.int32, (tm, LANES), 1)
        pos = (step - 1) * tm + lax.broadcasted_iota(jnp.int32, (tm, 1), 0)
        low = lane < dh
        pcol = _pos_lanes(pos, lane)
        wcol = _pos_lanes(pos + WINDOW, lane)

        dq_ref[0] = proj(_OFF_DQ, 512).astype(BF16)
        k = proj(_OFF_DK, 512)
        v = proj(_OFF_DV, 512)
        for hh in range(DIFF_HEADS):
            dk_ref[0, :, hh * 256:hh * 256 + LANES] = k[:, hh * LANES:(hh + 1) * LANES].astype(BF16)
            dk_ref[0, :, hh * 256 + LANES:(hh + 1) * 256] = pcol.astype(BF16)
            dvt_ref[0, hh, 0] = v[:, hh * LANES:(hh + 1) * LANES].T.astype(BF16)
        nq_ref[0] = proj(_OFF_NQ, 512).astype(BF16)
        kv = proj(_OFF_KV, 768)
        ckv_ref[0, 0] = kv[:, :LANES]
        ckv_ref[0, 1] = kv[:, LANES:2 * LANES]
        sk, sv, wk, wv = (kv[:, i * LANES:(i + 1) * LANES] for i in range(2, 6))
        onehot = jnp.where(lane == pos // SLC_BLOCK, 1.0, 0.0).astype(BF16)
        sv_t = sv.T
        wv_t = wv.T
        for g, (keys_sel, keys_win) in enumerate(((jnp.where(low, sk, pcol), jnp.where(low, wk, wcol)),
                                                  (jnp.where(low, pcol, sk), jnp.where(low, wcol, wk)))):
            ska_ref[0, g, :, :LANES] = keys_sel.astype(BF16)
            ska_ref[0, g, :, LANES:] = onehot
            svt_ref[0, g, 0] = sv_t[g * dh:(g + 1) * dh].astype(BF16)
            wka_ref[0, g] = keys_win.astype(BF16)
            for c in range(tm // LANES):
                wvt_ref[0, g, c] = wv_t[g * dh:(g + 1) * dh, c * LANES:(c + 1) * LANES].astype(BF16)
        gt_ref[0] = jax.nn.sigmoid(proj(_OFF_NG, LANES))


def _build_w_proj(w_in):
    scale = LOG2E * HEAD_DIM ** -0.5
    nq = w_in[:, _OFF_NQ:_OFF_NQ + 512].reshape(D_MODEL, NSA_KV_HEADS, NSA_GROUP, HEAD_DIM)
    nq = nq.transpose(0, 2, 1, 3).reshape(D_MODEL, 512) * scale
    cols = [w_in[:, _OFF_DQ:_OFF_DQ + 512] * scale, w_in[:, _OFF_DK:_OFF_NQ], nq,
            w_in[:, _OFF_KV:_OFF_NG + _N_GATES], jnp.zeros((D_MODEL, LANES - _N_GATES), F32)]
    return jnp.concatenate(cols, axis=1).astype(BF16)


def _in_proj(x, g_pre, sc, sh, w_proj):
    B, S, D = x.shape
    tm = ROW_TILE
    n = S // tm
    KV, dh = NSA_KV_HEADS, HEAD_DIM
    tile = lambda b, s: (b, jnp.maximum(s - 1, 0), 0)
    out_shapes = [
        ((B, S, 512), (1, tm, 512), tile, BF16),
        ((B, S, DIFF_HEADS * 256), (1, tm, DIFF_HEADS * 256), tile, BF16),
        ((B, DIFF_HEADS, n, LANES, tm), (1, DIFF_HEADS, 1, LANES, tm),
         lambda b, s: (b, 0, jnp.maximum(s - 1, 0), 0, 0), BF16),
        ((B, S, 512), (1, tm, 512), tile, BF16),
        ((B, 2, S, LANES), (1, 2, tm, LANES), lambda b, s: (b, 0, jnp.maximum(s - 1, 0), 0), F32),
        ((B, KV, S, 2 * LANES), (1, KV, tm, 2 * LANES),
         lambda b, s: (b, 0, jnp.maximum(s - 1, 0), 0), BF16),
        ((B, KV, n, dh, tm), (1, KV, 1, dh, tm),
         lambda b, s: (b, 0, jnp.maximum(s - 1, 0), 0, 0), BF16),
        ((B, KV, S + WINDOW, LANES), (1, KV, tm, LANES), lambda b, s: (b, 0, s, 0), BF16),
        ((B, KV, (S + WINDOW) // LANES, dh, LANES), (1, KV, tm // LANES, dh, LANES),
         lambda b, s: (b, 0, s, 0, 0), BF16),
        ((B, S, LANES), (1, tm, LANES), tile, F32),
    ]
    return pl.pallas_call(
        _inproj_kernel,
        grid=(B, n + 1),
        in_specs=[pl.BlockSpec((1, tm, D), tile),
                  pl.BlockSpec((1, D), lambda b, s: (0, 0)),
                  pl.BlockSpec((1, 1, D), lambda b, s: (b, 0, 0)),
                  pl.BlockSpec((1, 1, D), lambda b, s: (b, 0, 0)),
                  pl.BlockSpec((D, _W_PROJ), lambda b, s: (0, 0))],
        out_specs=[pl.BlockSpec(blk, idx) for _, blk, idx, _ in out_shapes],
        out_shape=[jax.ShapeDtypeStruct(shape, dt) for shape, _, _, dt in out_shapes],
        compiler_params=_params(("parallel", "arbitrary")),
        name="in_proj",
    )(x, g_pre.reshape(1, D), sc.reshape(B, 1, D), sh.reshape(B, 1, D), w_proj)


def _compress_kernel(x_ref, pos_ref, w1_ref, w2_ref, kca_ref, vct_ref, vc_ref):
    dh = HEAD_DIM
    n_rows = x_ref.shape[2] // CMP_STRIDE

    def mlp(t):
        first = jnp.zeros((n_rows, w1_ref.shape[3]), F32)
        second = jnp.zeros((n_rows, w1_ref.shape[3]), F32)
        for m in range(CMP_STRIDE):
            x = x_ref[0, t, pl.ds(m, n_rows, stride=CMP_STRIDE), :]
            first += jnp.dot((x + pos_ref[t, m:m + 1, :]).astype(BF16), w1_ref[t, m],
                             preferred_element_type=F32)
            second += jnp.dot((x + pos_ref[t, CMP_STRIDE + m:CMP_STRIDE + m + 1, :]).astype(BF16),
                              w1_ref[t, CMP_STRIDE + m], preferred_element_type=F32)
        hid = first + pltpu.roll(second, n_rows - 1, 0)
        return jnp.dot(jax.nn.gelu(hid).astype(BF16), w2_ref[t], preferred_element_type=F32)

    lane = lax.broadcasted_iota(jnp.int32, (n_rows, LANES), 1)
    cend = lax.broadcasted_iota(jnp.int32, (n_rows, 1), 0) * CMP_STRIDE + (CMP_BLOCK - 1)
    pcol = _pos_lanes(cend, lane)
    kc = mlp(0)
    low = lane < dh
    kca_ref[0, 0] = jnp.where(low, kc, pcol).astype(BF16)
    kca_ref[0, 1] = jnp.where(low, pcol, kc).astype(BF16)
    vc_ref[...] = mlp(1)
    vc_t = vc_ref[...].T
    for h in range(NSA_KV_HEADS):
        vct_ref[0, h] = vc_t[h * dh:(h + 1) * dh].astype(BF16)


def _block_diag(blocks):
    n, r, c = blocks.shape[-3:]
    out = jnp.einsum('...irc,ij->...irjc', blocks, jnp.eye(n, dtype=blocks.dtype))
    return out.reshape(blocks.shape[:-3] + (n * r, n * c))


def _compress(ckv, pos_k, pos_v, w1_k, w1_v, w2_k, w2_v):
    B, _, S, W = ckv.shape
    KV, dh = NSA_KV_HEADS, HEAD_DIM
    n_rows = S // CMP_STRIDE
    pos = jnp.stack([jnp.concatenate([p, p], axis=1) for p in (pos_k, pos_v)]).astype(F32)
    w1 = jnp.stack([w.reshape(CMP_BLOCK, dh, -1) for w in (w1_k, w1_v)])
    w1 = _block_diag(jnp.stack([w1, w1], axis=2)).astype(BF16)
    w2 = jnp.stack([w2_k, w2_v])
    w2 = _block_diag(jnp.stack([w2, w2], axis=1)).astype(BF16)
    return pl.pallas_call(
        _compress_kernel,
        grid=(B,),
        in_specs=[pl.BlockSpec((1, 2, S, W), lambda b: (b, 0, 0, 0)),
                  pl.BlockSpec(pos.shape, lambda b: (0, 0, 0)),
                  pl.BlockSpec(w1.shape, lambda b: (0, 0, 0, 0)),
                  pl.BlockSpec(w2.shape, lambda b: (0, 0, 0))],
        out_specs=[pl.BlockSpec((1, KV, n_rows, LANES), lambda b: (b, 0, 0, 0)),
                   pl.BlockSpec((1, KV, dh, n_rows), lambda b: (b, 0, 0, 0))],
        out_shape=[jax.ShapeDtypeStruct((B, KV, n_rows, LANES), BF16),
                   jax.ShapeDtypeStruct((B, KV, dh, n_rows), BF16)],
        scratch_shapes=[pltpu.VMEM((n_rows, W), F32)],
        compiler_params=_params(("parallel",)),
        name="nsa_compress",
    )(ckv, pos, w1, w2)


def _flash_scratch(dv, n_cols):
    tk = KEY_TILE
    return [pltpu.VMEM((tk, n_cols), F32), pltpu.VMEM((tk, n_cols), F32),
            pltpu.VMEM((tk, n_cols), BF16), pltpu.VMEM((tk, n_cols), BF16),
            pltpu.VMEM((1, n_cols), F32), pltpu.VMEM((1, n_cols), F32),
            pltpu.VMEM((1, n_cols), F32), pltpu.VMEM((1, n_cols), F32), pltpu.VMEM((dv, n_cols), F32)]


def _flash_causal(q0, tq, qa_ref, k_tile, v_tile, scratch, tiles=None):
    s_a, s_b, p_a, p_b, a_a, a_b, m_ref, l_ref, acc_ref = scratch
    tk = KEY_TILE
    n_cols = qa_ref.shape[0]
    assert tk % tq == 0
    n_full = q0 // tk
    n = n_full + 1 if tiles is None else tiles[1]
    m_ref[...] = jnp.full(m_ref.shape, NEG, F32)
    l_ref[...] = jnp.zeros(l_ref.shape, F32)
    acc_ref[...] = jnp.zeros(acc_ref.shape, F32)
    p_b[...] = jnp.zeros(p_b.shape, BF16)
    a_b[...] = jnp.ones(a_b.shape, F32)

    def key_tile(t):
        t = jnp.clip(t, 0, n - 1)
        j = jnp.where(t == 0, n_full, t - 1) if tiles is None else tiles[0][t]
        return j, pl.multiple_of(j * tk, tk)

    def scores(t, s_ref, masked):
        j, k0 = key_tile(t)
        kt = k_tile(j, k0)
        for c in range(0, n_cols, COL_SPLIT):
            s = _dot_nt(kt, qa_ref[c:c + COL_SPLIT, :])
            if masked:
                krow = lax.broadcasted_iota(jnp.int32, (tk, 1), 0)
                qcol = (lax.broadcasted_iota(jnp.int32, (1, COL_SPLIT), 1) + c) % tq
                s = jnp.where((krow + (k0 - q0)) <= qcol, s, NEG)
            s_ref[:, c:c + COL_SPLIT] = s

    def softmax(s_ref, p_ref, a_ref):
        ch = SOFTMAX_CHUNK_VREGS * 8 * LANES // n_cols
        chunks = [slice(c, c + ch) for c in range(0, tk, ch)]
        mx = None
        for rows in chunks:
            part = jnp.max(s_ref[rows, :].reshape(ch // 8, 8, n_cols), axis=0)
            mx = part if mx is None else jnp.maximum(mx, part)
        m_old = m_ref[...]
        m_new = jnp.maximum(m_old, jnp.max(mx, axis=0, keepdims=True))
        sm = None
        for rows in chunks:
            p = jnp.exp2(s_ref[rows, :] - m_new)
            p_ref[rows, :] = p.astype(BF16)
            part = jnp.sum(p.reshape(ch // 8, 8, n_cols), axis=0)
            sm = part if sm is None else sm + part
        alpha = jnp.exp2(m_old - m_new)
        a_ref[...] = alpha
        l_ref[...] = alpha * l_ref[...] + jnp.sum(sm, axis=0, keepdims=True)
        m_ref[...] = m_new

    def values(t, p_ref, a_ref):
        j, k0 = key_tile(t)
        vt = v_tile(j, k0)
        for c in range(0, n_cols, COL_SPLIT):
            cs = slice(c, c + COL_SPLIT)
            acc_ref[:, cs] = a_ref[:, cs] * acc_ref[:, cs] + jnp.dot(vt, p_ref[:, cs], preferred_element_type=F32)

    scores(0, s_a, True)

    def pair(u, carry):
        i = 2 * u
        scores(i + 1, s_b, False)
        softmax(s_a, p_a, a_a)
        values(i - 1, p_b, a_b)

        @pl.when(i + 1 < n)
        def _():
            scores(i + 2, s_a, False)
            softmax(s_b, p_b, a_b)
            values(i, p_a, a_a)

        return carry

    lax.fori_loop(0, (n + 1) // 2, pair, 0)

    @pl.when(n % 2 == 1)
    def _():
        values(n - 1, p_a, a_a)

    @pl.when(n % 2 == 0)
    def _():
        values(n - 1, p_b, a_b)

    return acc_ref, l_ref


def _softmax_cols(s, bias):
    e, inv = _softmax_cols_unnormalised(s, bias)
    return e * inv


def _softmax_cols_unnormalised(s, bias):
    s = s + bias
    m = jnp.max(s, axis=0, keepdims=True)
    e = jnp.exp2(s - m)
    inv = jnp.where(m > 0.5 * NEG, 1.0 / jnp.sum(e, axis=0, keepdims=True), 0.0)
    return e, inv


def _diff_kernel(qx_ref, lam_ref, gsub_ref, q_ref, k_ref, v_ref, o_ref, qa_ref, *scratch, lambda_init):
    tq = DIFF_TQ
    q0 = pl.program_id(2) * tq
    q = q_ref[0].astype(F32)
    low = lax.broadcasted_iota(jnp.int32, (tq, LANES), 1) < HEAD_DIM
    ext = jnp.broadcast_to(qx_ref[0], (tq, LANES)).astype(BF16)
    qa_ref[0:tq, :LANES] = jnp.where(low, q, 0.0).astype(BF16)
    qa_ref[tq:, :LANES] = jnp.where(low, 0.0, q).astype(BF16)
    qa_ref[0:tq, LANES:] = ext
    qa_ref[tq:, LANES:] = ext

    acc_ref, l_ref = _flash_causal(q0, tq, qa_ref,
                                   lambda j, k0: k_ref[0, pl.ds(k0, KEY_TILE), :],
                                   lambda j, k0: v_ref[0, 0, j],
                                   scratch)

    lv = lam_ref[...]
    lam = (jnp.exp(jnp.sum(lv[0:1] * lv[1:2], axis=-1, keepdims=True))
           - jnp.exp(jnp.sum(lv[2:3] * lv[3:4], axis=-1, keepdims=True)) + lambda_init)
    o_t = acc_ref[...] / l_ref[...]
    o_t = o_t[:, :tq] - lam * o_t[:, tq:]
    o_t = o_t * lax.rsqrt(jnp.mean(o_t * o_t, axis=0, keepdims=True) + EPS)
    o_ref[0] = (o_t.T * gsub_ref[...] * (1.0 - lambda_init)).astype(o_ref.dtype)


def _diff_attention(dq, dk, dv_t, lam_vecs, g_subln, lambda_init):
    B, S, _ = dq.shape
    tq, tk = DIFF_TQ, KEY_TILE
    H = DIFF_HEADS
    qx = jnp.asarray(_slope_cols(_alibi_slopes(H), LANES, 0)).reshape(H, 1, LANES)
    lam = jnp.zeros((8, LANES), F32).at[:4, :HEAD_DIM].set(jnp.stack(lam_vecs))
    return pl.pallas_call(
        functools.partial(_diff_kernel, lambda_init=lambda_init),
        grid=(B, H, S // tq),
        in_specs=[pl.BlockSpec((1, 1, LANES), lambda b, h, i: (h, 0, 0)),
                  pl.BlockSpec((8, LANES), lambda b, h, i: (0, 0)),
                  pl.BlockSpec((1, LANES), lambda b, h, i: (0, 0)),
                  pl.BlockSpec((1, tq, LANES), lambda b, h, i: (b, i, h)),
                  pl.BlockSpec((1, S, 2 * LANES), lambda b, h, i: (b, 0, h)),
                  pl.BlockSpec((1, 1, S // tk, LANES, tk), lambda b, h, i: (b, h, 0, 0, 0))],
        out_specs=pl.BlockSpec((1, tq, LANES), lambda b, h, i: (b, i, h)),
        out_shape=jax.ShapeDtypeStruct((B, S, H * LANES), BF16),
        scratch_shapes=[pltpu.VMEM((2 * tq, 2 * LANES), BF16)] + _flash_scratch(LANES, 2 * tq),
        compiler_params=_params(("parallel", "parallel", "parallel")),
        name="diff_attention",
    )(qx, lam, g_subln.reshape(1, LANES).astype(F32), dq, dk, dv_t)


def _nsa_kernel(qx_ref, q_ref, kc_ref, vc_ref, ka_ref, sv_ref, wk_ref, wv_ref, g_ref, ov_ref, cb_ref, wb_ref,
                o_ref, qa_ref, out_ref, tiles_ref, *scratch):
    tq, G, dh = NSA_TQ, NSA_GROUP, HEAD_DIM
    n_cmp = kc_ref.shape[2]
    n_win = WINDOW + tq
    grp = pl.program_id(1)
    qi = pl.program_id(2)
    q0 = pl.multiple_of(qi * tq, tq)
    cols = [slice(i * tq, (i + 1) * tq) for i in range(G)]
    g_t = g_ref[0].T

    def gate(i, branch):
        a = 3 * i + branch
        b = 3 * (G + i) + branch
        return jnp.where(grp == 0, g_t[a:a + 1, :], g_t[b:b + 1, :])

    lane_q = lax.broadcasted_iota(jnp.int32, (tq, LANES), 1)
    mine = (lane_q >= grp * dh) & (lane_q < (grp + 1) * dh)
    qx = qx_ref[0]
    for i in range(G):
        q = q_ref[0, :, i * LANES:(i + 1) * LANES].astype(F32)
        qa_ref[cols[i], :LANES] = jnp.where(mine, q, jnp.broadcast_to(qx[i:i + 1], (tq, LANES))).astype(BF16)
    qs = qa_ref[:, :LANES]

    kw = wk_ref[0, 0, pl.ds(q0, n_win), :]
    vw = jnp.concatenate([wv_ref[0, 0, (tq // LANES) * qi + c] for c in range(n_win // LANES)], axis=1)
    e, inv = _softmax_cols_unnormalised(_dot_nt(kw, qs), jnp.tile(wb_ref[...], (1, G)))
    ow = jnp.dot(vw, e.astype(BF16), preferred_element_type=F32) * inv

    cbias = cb_ref[pl.ds(pl.multiple_of(n_cmp - (tq // CMP_STRIDE) * qi, 8), n_cmp), :]
    p = _softmax_cols(_dot_nt(kc_ref[0, 0], qs), jnp.tile(cbias, (1, G)))
    oc = jnp.dot(vc_ref[0, 0], p.astype(BF16), preferred_element_type=F32)
    for i in range(G):
        out_ref[i * dh:(i + 1) * dh, :] = gate(i, 0) * oc[:, cols[i]] + gate(i, 2) * ow[:, cols[i]]

    psum = p[:, cols[0]]
    for i in range(1, G):
        psum = psum + p[:, cols[i]]
    hi = psum.astype(BF16)
    r1 = psum - hi.astype(F32)
    mid = r1.astype(BF16)
    lo = (r1 - mid.astype(F32)).astype(BF16)
    ov = ov_ref[...]
    imp = (jnp.dot(ov, hi, preferred_element_type=F32) + jnp.dot(ov, mid, preferred_element_type=F32)
           + jnp.dot(ov, lo, preferred_element_type=F32))
    n_slc = imp.shape[0]
    blk = lax.broadcasted_iota(jnp.int32, (n_slc, tq), 0)
    cur = (lax.broadcasted_iota(jnp.int32, (1, tq), 1) + q0) // SLC_BLOCK
    forced = (blk == 0) | (blk == cur) | (blk == cur - 1)
    score = jnp.where(forced, -jnp.inf, jnp.where(blk <= cur, imp, NEG))
    blkf = blk.astype(F32)
    for _ in range(SLC_TOP - 3):
        mx = jnp.max(score, axis=0, keepdims=True)
        idx = jnp.min(jnp.where(score == mx, blkf, float(n_slc)), axis=0, keepdims=True)
        score = jnp.where(blkf == idx, -jnp.inf, score)
    sel = jnp.where(score == -jnp.inf, 1.0, 0.0)
    sel_t = sel.T
    blockmask = ((sel_t - 1.0) * 1e30).astype(BF16)
    for i in range(G):
        qa_ref[cols[i], LANES:] = blockmask

    per_tile = KEY_TILE // SLC_BLOCK
    n_tiles = n_slc // per_tile
    used = jnp.max(jnp.max(sel, axis=1, keepdims=True).reshape(n_tiles, per_tile, 1), axis=1)
    bit = jnp.where(used > 0.0, 1 << lax.broadcasted_iota(jnp.int32, (n_tiles, 1), 0), 0)
    bits = jnp.sum(bit, axis=0, keepdims=True)[0, 0]
    n_full = q0 // KEY_TILE
    tiles_ref[0] = n_full
    count = jnp.int32(1)
    for j in range(n_tiles):
        tiles_ref[count] = j
        count = count + jnp.where(j < n_full, (bits >> j) & 1, 0)

    acc_ref, l_ref = _flash_causal(q0, tq, qa_ref,
                                   lambda j, k0: ka_ref[0, 0, pl.ds(k0, KEY_TILE), :],
                                   lambda j, k0: sv_ref[0, 0, j],
                                   scratch, tiles=(tiles_ref, count))
    os_ = acc_ref[...] / l_ref[...]
    for i in range(G):
        out_ref[i * dh:(i + 1) * dh, :] += gate(i, 1) * os_[:, cols[i]]

    o_ref[0] = out_ref[...].T.astype(o_ref.dtype)


def _nsa_attention(nq, kca, vc_t, ska, sv_t, wka, wv_t, gates):
    B, S, _ = nq.shape
    tq, tk, G, dh = NSA_TQ, KEY_TILE, NSA_GROUP, HEAD_DIM
    KV = NSA_KV_HEADS
    n_cmp = kca.shape[2]
    n_slc = S // SLC_BLOCK
    Sp = S + WINDOW
    slopes = _alibi_slopes(NSA_HEADS)
    qx = jnp.asarray(np.stack([_slope_cols(slopes[:G], LANES, dh), _slope_cols(slopes[G:], LANES, 0)]))
    overlap_t = _overlap_matrix(S).T
    r = np.arange(tq)[None, :]
    j = np.arange(2 * n_cmp)[:, None]
    cmp_band = jnp.asarray(np.where(CMP_STRIDE * (j - n_cmp) + CMP_BLOCK - 1 <= r, 0.0, NEG).astype(np.float32))
    c = np.arange(WINDOW + tq)[:, None]
    d = r - c + WINDOW
    win_band = jnp.asarray(np.where((d >= 0) & (d < WINDOW), 0.0, NEG).astype(np.float32))
    const2 = lambda b, g, i: (0, 0)
    per_bg = lambda b, g, i: (b, g, 0, 0)
    per_bg5 = lambda b, g, i: (b, g, 0, 0, 0)
    return pl.pallas_call(
        _nsa_kernel,
        grid=(B, KV, S // tq),
        in_specs=[pl.BlockSpec((1, G, LANES), lambda b, g, i: (g, 0, 0)),
                  pl.BlockSpec((1, tq, G * LANES), lambda b, g, i: (b, i, 0)),
                  pl.BlockSpec((1, 1, n_cmp, LANES), per_bg),
                  pl.BlockSpec((1, 1, dh, n_cmp), per_bg),
                  pl.BlockSpec((1, 1, S, 2 * LANES), per_bg),
                  pl.BlockSpec((1, 1, S // tk, dh, tk), per_bg5),
                  pl.BlockSpec((1, 1, Sp, LANES), per_bg),
                  pl.BlockSpec((1, 1, Sp // LANES, dh, LANES), per_bg5),
                  pl.BlockSpec((1, tq, LANES), lambda b, g, i: (b, i, 0)),
                  pl.BlockSpec((n_slc, n_cmp), const2),
                  pl.BlockSpec(cmp_band.shape, const2), pl.BlockSpec(win_band.shape, const2)],
        out_specs=pl.BlockSpec((1, tq, G * dh), lambda b, g, i: (b, i, g)),
        out_shape=jax.ShapeDtypeStruct((B, S, NSA_HEADS * dh), BF16),
        scratch_shapes=[pltpu.VMEM((G * tq, 2 * LANES), BF16),
                        pltpu.VMEM((G * dh, tq), F32),
                        pltpu.SMEM((2 * (S // tk),), jnp.int32)]
        + _flash_scratch(dh, G * tq),
        compiler_params=_params(("parallel", "parallel", "parallel")),
        name="nsa_attention",
    )(qx, nq, kca, vc_t, ska, sv_t, wka, wv_t, gates, overlap_t, cmp_band, win_band)


def _outproj_kernel(oa_ref, ob_ref, x_ref, wo_ref, gpm_ref, gta_ref, gpf_ref, sc_ref, sh_ref, wr_ref, br_ref,
                    x1_ref, h2_ref, rt_ref, cnt_ref):
    half = oa_ref.shape[1]
    mix = (jnp.dot(oa_ref[...], wo_ref[:half], preferred_element_type=F32)
           + jnp.dot(ob_ref[...], wo_ref[half:], preferred_element_type=F32))
    x1 = x_ref[...] + gta_ref[0] * (_rms(mix) * gpm_ref[...])
    x1_ref[...] = x1
    h2 = _rms(x1) * gpf_ref[...] * (1.0 + sc_ref[0]) + sh_ref[0]
    h2_ref[...] = h2
    h_hi = h2.astype(BF16)
    h_lo = (h2 - h_hi.astype(F32)).astype(BF16)
    logits = (jnp.dot(h_hi, wr_ref[0], preferred_element_type=F32)
              + jnp.dot(h_lo, wr_ref[0], preferred_element_type=F32)
              + jnp.dot(h_hi, wr_ref[1], preferred_element_type=F32)) + br_ref[...]
    tm = logits.shape[0]
    lane = lax.broadcasted_iota(jnp.int32, (tm, LANES), 1)
    lanef = lane.astype(F32)
    is_grp = (lane >= N_EXPERTS) & (lane < N_EXPERTS + N_GROUPS)
    lg = jnp.where(is_grp, logits, -jnp.inf)
    mg = jnp.max(lg, axis=-1, keepdims=True)
    g_lane = jnp.min(jnp.where(lg == mg, lanef, 1e9), axis=-1, keepdims=True)
    p_grp = 1.0 / jnp.sum(jnp.where(is_grp, jnp.exp(lg - mg), 0.0), axis=-1, keepdims=True)
    grp = g_lane.astype(jnp.int32) - N_EXPERTS
    in_grp = (lane < N_EXPERTS) & ((lane // EXPERTS_PER_GROUP) == grp)
    le = jnp.where(in_grp, logits, -jnp.inf)
    v1 = jnp.max(le, axis=-1, keepdims=True)
    i1 = jnp.min(jnp.where(le == v1, lanef, 1e9), axis=-1, keepdims=True)
    le2 = jnp.where(lanef == i1, -jnp.inf, le)
    v2 = jnp.max(le2, axis=-1, keepdims=True)
    i2 = jnp.min(jnp.where(le2 == v2, lanef, 1e9), axis=-1, keepdims=True)
    e2 = jnp.exp(v2 - v1)
    w1 = p_grp / (1.0 + e2)
    w2 = p_grp * e2 / (1.0 + e2)
    rt_ref[...] = jnp.where(lane == 0, i1, jnp.where(lane == 1, i2, jnp.where(lane == 2, w1,
                            jnp.where(lane == 3, w2, 0.0))))

    @pl.when(pl.program_id(0) == 0)
    def _():
        cnt_ref[...] = jnp.zeros(cnt_ref.shape, F32)

    picked = jnp.where(lanef == i1, 1.0, jnp.where(lanef == i2, 1.0, 0.0))
    cnt_ref[...] += jnp.sum(picked, axis=0, keepdims=True)


def _out_proj(o_a, o_b, x2, w_out, g_post_mix, gt_a, g_pre_ffn, sc_f, sh_f, w_r, b_r, B, S):
    T, D = x2.shape
    tm = ROW_TILE
    per_b = S // tm
    row = lambda i: (i, 0)
    bat = lambda i: (i // per_b, 0, 0)
    const = lambda i: (0, 0)
    return pl.pallas_call(
        _outproj_kernel,
        grid=(T // tm,),
        in_specs=[pl.BlockSpec((tm, o_a.shape[1]), row), pl.BlockSpec((tm, o_b.shape[1]), row),
                  pl.BlockSpec((tm, D), row), pl.BlockSpec((D, D), const),
                  pl.BlockSpec((1, D), const), pl.BlockSpec((1, 1, D), bat), pl.BlockSpec((1, D), const),
                  pl.BlockSpec((1, 1, D), bat), pl.BlockSpec((1, 1, D), bat),
                  pl.BlockSpec((2, D, LANES), lambda i: (0, 0, 0)), pl.BlockSpec((1, LANES), const)],
        out_specs=[pl.BlockSpec((tm, D), row), pl.BlockSpec((tm, D), row), pl.BlockSpec((tm, LANES), row),
                   pl.BlockSpec((8, LANES), const)],
        out_shape=[jax.ShapeDtypeStruct((T, D), F32), jax.ShapeDtypeStruct((T, D), F32),
                   jax.ShapeDtypeStruct((T, LANES), F32), jax.ShapeDtypeStruct((8, LANES), F32)],
        compiler_params=_params(("arbitrary",)),
        name="out_proj_router",
    )(o_a, o_b, x2, w_out, g_post_mix.reshape(1, D), gt_a.reshape(B, 1, D), g_pre_ffn.reshape(1, D),
      sc_f.reshape(B, 1, D), sh_f.reshape(B, 1, D), w_r, b_r)


def _slots_kernel(rt_ref, cnt_ref, slot_ref, blk_ref, base_ref, *, n_blk_rows):
    tm = rt_ref.shape[0]
    rt = rt_ref[...]
    lane = lax.broadcasted_iota(jnp.int32, (tm, LANES), 1).astype(F32)
    i1 = rt[:, 0:1]
    i2 = rt[:, 1:2]
    hit1 = lane == i1
    hit2 = lane == i2
    onehot = jnp.where(hit1, 1.0, jnp.where(hit2, 1.0, 0.0))
    colsum = jnp.sum(onehot, axis=0, keepdims=True)

    @pl.when(pl.program_id(0) == 0)
    def _():
        cnt = cnt_ref[0:1, :]
        padded = jnp.floor((cnt + (MOE_ROWS - 1)) / MOE_ROWS) * MOE_ROWS
        lane1 = lax.broadcasted_iota(jnp.int32, (1, LANES), 1)
        incl = padded
        sh = 1
        while sh < LANES:
            incl = incl + jnp.where(lane1 >= sh, pltpu.roll(incl, sh, 1), 0.0)
            sh *= 2
        base_ref[...] = incl - padded
        rows = lax.broadcasted_iota(jnp.int32, (n_blk_rows, LANES), 0).astype(F32) * MOE_ROWS
        lane2 = lax.broadcasted_iota(jnp.int32, (n_blk_rows, LANES), 1)
        ended = jnp.where(lane2 < N_EXPERTS, jnp.where(incl <= rows, 1.0, 0.0), 0.0)
        blk_e = jnp.minimum(jnp.sum(ended, axis=-1, keepdims=True), N_EXPERTS - 1.0)
        total = jnp.sum(jnp.where(lane1 == N_EXPERTS - 1, incl, 0.0), axis=-1, keepdims=True)
        n_used = total / MOE_ROWS
        blk_ref[...] = jnp.where(lane2 == 0, blk_e, jnp.where(lane2 == 1, n_used, 0.0)).astype(jnp.int32)

    r = lax.broadcasted_iota(jnp.int32, (tm, tm), 0)
    c = lax.broadcasted_iota(jnp.int32, (tm, tm), 1)
    tri = jnp.where(c < r, 1.0, 0.0).astype(BF16)
    before = jnp.dot(tri, onehot.astype(BF16), preferred_element_type=F32) + base_ref[...]
    s1 = jnp.sum(jnp.where(hit1, before, 0.0), axis=-1, keepdims=True)
    s2 = jnp.sum(jnp.where(hit2, before, 0.0), axis=-1, keepdims=True)
    lane_i = lax.broadcasted_iota(jnp.int32, (tm, LANES), 1)
    slot_ref[...] = jnp.where(lane_i == 0, s1, jnp.where(lane_i == 1, s2, 0.0)).astype(jnp.int32)
    base_ref[...] += colsum


def _slots(route, counts, n_blocks):
    T = route.shape[0]
    tm = ROW_TILE
    n_blk_rows = -(-n_blocks // 8) * 8
    slot, blk = pl.pallas_call(
        functools.partial(_slots_kernel, n_blk_rows=n_blk_rows),
        grid=(T // tm,),
        in_specs=[pl.BlockSpec((tm, LANES), lambda i: (i, 0)),
                  pl.BlockSpec(counts.shape, lambda i: (0, 0))],
        out_specs=[pl.BlockSpec((tm, LANES), lambda i: (i, 0)),
                   pl.BlockSpec((n_blk_rows, LANES), lambda i: (0, 0))],
        out_shape=[jax.ShapeDtypeStruct((T, LANES), jnp.int32),
                   jax.ShapeDtypeStruct((n_blk_rows, LANES), jnp.int32)],
        scratch_shapes=[pltpu.VMEM((1, LANES), F32)],
        compiler_params=_params(("arbitrary",)),
        name="moe_slots",
    )(route, counts)
    return slot[:, 0], slot[:, 1], blk[:n_blocks, 0], blk[0:1, 1]


def _dispatch_kernel(s1_ref, s2_ref, h_ref, xz_ref, xb_ref, sem):
    del xz_ref
    tm = h_ref.shape[0]
    base = pl.program_id(0) * tm

    def issue(r, carry):
        src = h_ref.at[pl.ds(r, 1), :]
        pltpu.make_async_copy(src, xb_ref.at[pl.ds(s1_ref[base + r], 1), :], sem).start()
        pltpu.make_async_copy(src, xb_ref.at[pl.ds(s2_ref[base + r], 1), :], sem).start()
        return carry

    lax.fori_loop(0, tm, issue, 0, unroll=DMA_UNROLL)

    def drain(r, carry):
        for _ in range(TOP_K):
            pltpu.make_async_copy(h_ref.at[pl.ds(0, 1), :], xb_ref.at[pl.ds(0, 1), :], sem).wait()
        return carry

    lax.fori_loop(0, tm, drain, 0, unroll=DMA_UNROLL)


def _dispatch(h2, slot1, slot2, n_rows):
    T, D = h2.shape
    tm = ROW_TILE
    xz = jnp.zeros((n_rows, D), h2.dtype)
    return pl.pallas_call(
        _dispatch_kernel,
        grid_spec=pltpu.PrefetchScalarGridSpec(
            num_scalar_prefetch=2,
            grid=(T // tm,),
            in_specs=[pl.BlockSpec((tm, D), lambda i, a, b: (i, 0)),
                      pl.BlockSpec(memory_space=pl.ANY)],
            out_specs=pl.BlockSpec(memory_space=pl.ANY),
            scratch_shapes=[pltpu.SemaphoreType.DMA(())]),
        out_shape=jax.ShapeDtypeStruct((n_rows, D), h2.dtype),
        input_output_aliases={3: 0},
        compiler_params=_params(("arbitrary",)),
        name="moe_dispatch",
    )(slot1, slot2, h2, xz)


def _expert_kernel(be_ref, nu_ref, x_ref, wg_ref, wu_ref, wd_ref, y_ref):
    del be_ref
    i = pl.program_id(0)

    @pl.when(i >= nu_ref[0])
    def _():
        y_ref[...] = jnp.zeros(y_ref.shape, y_ref.dtype)

    @pl.when(i < nu_ref[0])
    def _():
        x = x_ref[...].astype(BF16)
        g = jnp.dot(x, wg_ref[0].astype(BF16), preferred_element_type=F32)
        u = jnp.dot(x, wu_ref[0].astype(BF16), preferred_element_type=F32)
        hid = (g * jax.nn.sigmoid(g) * u).astype(BF16)
        y_ref[...] = jnp.dot(hid, wd_ref[0].astype(BF16), preferred_element_type=F32).astype(y_ref.dtype)


def _experts(xbuf, blk_e, n_used, w_gate, w_up, w_down):
    D = xbuf.shape[1]
    E, _, De = w_gate.shape
    n_blocks = blk_e.shape[0]
    R = n_blocks * MOE_ROWS
    blk = lambda i, be, nu: (jnp.minimum(i, nu[0] - 1), 0)
    wsel = lambda i, be, nu: (be[jnp.minimum(i, nu[0] - 1)], 0, 0)
    return pl.pallas_call(
        _expert_kernel,
        grid_spec=pltpu.PrefetchScalarGridSpec(
            num_scalar_prefetch=2,
            grid=(n_blocks,),
            in_specs=[pl.BlockSpec((MOE_ROWS, D), blk),
                      pl.BlockSpec((1, D, De), wsel), pl.BlockSpec((1, D, De), wsel),
                      pl.BlockSpec((1, De, D), wsel)],
            out_specs=pl.BlockSpec((MOE_ROWS, D), lambda i, be, nu: (i, 0))),
        out_shape=jax.ShapeDtypeStruct((R, D), F32),
        compiler_params=_params(("arbitrary",)),
        name="moe_experts",
    )(blk_e, n_used, xbuf, w_gate, w_up, w_down)


def _combine_kernel(s1_ref, s2_ref, yb_ref, rt_ref, x1_ref, gtf_ref, gpo_ref, o_ref, ya_ref, yc_ref, sem):
    tm = x1_ref.shape[0]
    i = pl.program_id(0)

    def row_copy(src_row, dst_ref, slot, r):
        return pltpu.make_async_copy(yb_ref.at[pl.ds(src_row, 1), :], dst_ref.at[slot, pl.ds(r, 1), :], sem.at[slot])

    def gather(step, slot):
        def issue(r, carry):
            row_copy(s1_ref[step * tm + r], ya_ref, slot, r).start()
            row_copy(s2_ref[step * tm + r], yc_ref, slot, r).start()
            return carry

        lax.fori_loop(0, tm, issue, 0, unroll=DMA_UNROLL)

    @pl.when(i == 0)
    def _():
        gather(0, 0)

    @pl.when(i + 1 < pl.num_programs(0))
    def _():
        gather(i + 1, (i + 1) % 2)

    slot = i % 2

    def drain(r, carry):
        for _ in range(TOP_K):
            row_copy(0, ya_ref, slot, 0).wait()
        return carry

    lax.fori_loop(0, tm, drain, 0, unroll=DMA_UNROLL)
    rt = rt_ref[...]
    f = rt[:, 2:3] * ya_ref[slot] + rt[:, 3:4] * yc_ref[slot]
    o_ref[...] = x1_ref[...] + gtf_ref[0] * (_rms(f) * gpo_ref[...])


def _combine(ybuf, slot1, slot2, route, x1, gt_f, g_post_ffn, B, S):
    T, D = x1.shape
    tm = ROW_TILE
    per_b = S // tm
    row = lambda i, a, b: (i, 0)
    return pl.pallas_call(
        _combine_kernel,
        grid_spec=pltpu.PrefetchScalarGridSpec(
            num_scalar_prefetch=2,
            grid=(T // tm,),
            in_specs=[pl.BlockSpec(memory_space=pl.ANY),
                      pl.BlockSpec((tm, LANES), row), pl.BlockSpec((tm, D), row),
                      pl.BlockSpec((1, 1, D), lambda i, a, b: (i // per_b, 0, 0)),
                      pl.BlockSpec((1, D), lambda i, a, b: (0, 0))],
            out_specs=pl.BlockSpec((tm, D), row),
            scratch_shapes=[pltpu.VMEM((2, tm, D), F32), pltpu.VMEM((2, tm, D), F32),
                            pltpu.SemaphoreType.DMA((2,))]),
        out_shape=jax.ShapeDtypeStruct((T, D), F32),
        compiler_params=_params(("arbitrary",)),
        name="moe_combine",
    )(slot1, slot2, ybuf, route, x1, gt_f.reshape(B, 1, D), g_post_ffn.reshape(1, D))


def _overlap_matrix(S):
    n_rows = S // CMP_STRIDE
    n_cmp = (S - CMP_BLOCK) // CMP_STRIDE + 1
    n_slc = S // SLC_BLOCK
    cstart = np.arange(n_rows) * CMP_STRIDE
    sstart = np.arange(n_slc) * SLC_BLOCK
    ov = (cstart[:, None] < sstart[None, :] + SLC_BLOCK) & (cstart[:, None] + CMP_BLOCK > sstart[None, :])
    ov &= (np.arange(n_rows) < n_cmp)[:, None]
    return jnp.asarray(ov.astype(np.float32), dtype=BF16)


def kernel(x, c, w_ada, b_ada, g_pre_mix, g_post_mix, g_pre_ffn, g_post_ffn, w_in, lam_q1, lam_k1, lam_q2,
           lam_k2, g_subln, cmp_pos_k, cmp_pos_v, cmp_w1_k, cmp_w1_v, cmp_w2_k, cmp_w2_v, w_out,
           w_router_grp, b_router_grp, w_router_exp, b_router_exp, w_exp_gate, w_exp_up, w_exp_down):
    B, S, D = x.shape
    T = B * S
    depth = w_ada.shape[0]
    assert S // SLC_BLOCK == LANES and D == D_MODEL
    n_blocks = (T * TOP_K) // MOE_ROWS + N_EXPERTS
    for l in range(depth):
        lambda_init = 0.8 - 0.6 * math.exp(-0.3 * l)
        mod = _adaln(c, w_ada[l], b_ada[l])
        sh_a, sc_a, gt_a, sh_f, sc_f, gt_f = jnp.split(mod, 6, axis=-1)

        dq, dk, dv_t, nq, ckv, ska, sv_t, wka, wv_t, gates = _in_proj(x, g_pre_mix[l], sc_a, sh_a,
                                                                      _build_w_proj(w_in[l]))
        kca, vc_t = _compress(ckv, cmp_pos_k[l], cmp_pos_v[l], cmp_w1_k[l], cmp_w1_v[l], cmp_w2_k[l], cmp_w2_v[l])
        o_a = _diff_attention(dq, dk, dv_t, (lam_q1[l], lam_k1[l], lam_q2[l], lam_k2[l]), g_subln[l], lambda_init)
        o_b = _nsa_attention(nq, kca, vc_t, ska, sv_t, wka, wv_t, gates)

        w_r = jnp.zeros((D, LANES), F32).at[:, :N_EXPERTS].set(w_router_exp[l]).at[
            :, N_EXPERTS:N_EXPERTS + N_GROUPS].set(w_router_grp[l])
        b_r = jnp.zeros((1, LANES), F32).at[0, :N_EXPERTS].set(b_router_exp[l]).at[
            0, N_EXPERTS:N_EXPERTS + N_GROUPS].set(b_router_grp[l])
        w_r_hi = w_r.astype(BF16)
        w_r = jnp.stack([w_r_hi, (w_r - w_r_hi.astype(F32)).astype(BF16)])
        x1, h2, route, counts = _out_proj(o_a.reshape(T, -1), o_b.reshape(T, -1), x.reshape(T, D),
                                          w_out[l].astype(BF16), g_post_mix[l], gt_a, g_pre_ffn[l], sc_f, sh_f,
                                          w_r, b_r, B, S)
        slot1, slot2, blk_e, n_used = _slots(route, counts, n_blocks)
        xbuf = _dispatch(h2, slot1, slot2, n_blocks * MOE_ROWS)
        ybuf = _experts(xbuf, blk_e, n_used, w_exp_gate[l], w_exp_up[l], w_exp_down[l])
        x = _combine(ybuf, slot1, slot2, route, x1, gt_f, g_post_ffn[l], B, S).reshape(B, S, D)
    return x
```

```python
import functools
import math

import numpy as np
import jax
import jax.numpy as jnp
from jax import lax
from jax.experimental import pallas as pl
from jax.experimental.pallas import tpu as pltpu

F32 = jnp.float32
BF16 = jnp.bfloat16
HIGHEST = lax.Precision.HIGHEST

D_MODEL = 1024
HEAD_DIM = 64
LANES = 128
DIFF_HEADS = 4
NSA_HEADS = 8
NSA_KV_HEADS = 2
NSA_GROUP = NSA_HEADS // NSA_KV_HEADS
CMP_BLOCK = 32
CMP_STRIDE = 16
CMP_HIDDEN = 2 * HEAD_DIM
SLC_BLOCK = 64
SLC_TOP = 16
WINDOW = 512
N_GROUPS = 8
EXPERTS_PER_GROUP = 8
N_EXPERTS = N_GROUPS * EXPERTS_PER_GROUP
TOP_K = 2
D_EXPERT = D_MODEL // 2
NEG = -1e30
FORCE_BONUS = 1e4
EPS = 1e-6

ROW_TILE = 512
KEY_TILE = 512
DIFF_TQ = 512
NSA_TQ = 256
COL_SPLIT = 512
SOFTMAX_CHUNK_VREGS = 16
MOE_ROWS = 256
DMA_UNROLL = 32
VMEM_LIMIT = 48 * 1024 * 1024
assert ROW_TILE == KEY_TILE == WINDOW

_OFF_DQ, _OFF_DK, _OFF_DV, _OFF_NQ = 0, 512, 1024, 1536
_OFF_KV = 2048
_OFF_NG = 2816
_N_GATES = 3 * NSA_HEADS
_W_PROJ = 2048 + 768 + LANES


def _alibi_slopes(n):
    return np.array([2.0 ** (-8.0 * (i + 1) / n) for i in range(n)], dtype=np.float32)


def _dot_nt(a, b):
    return lax.dot_general(a, b, (((1,), (1,)), ((), ())), preferred_element_type=F32)


def _rms(v):
    return v * lax.rsqrt(jnp.mean(v * v, axis=-1, keepdims=True) + EPS)


def _params(sem, vmem=VMEM_LIMIT):
    return pltpu.CompilerParams(dimension_semantics=sem, vmem_limit_bytes=vmem)


LOG2E = np.float32(1.0 / math.log(2.0))
N_SLOPE_PIECES = 3
MASK_LANE = 2 * N_SLOPE_PIECES


def _pos_lanes(pos, lane):
    half = lane % HEAD_DIM
    return jnp.where(half < MASK_LANE, jnp.where(half % 2 == 0, pos // LANES, pos % LANES), 0).astype(F32)


def _slope_cols(slopes, width, at):
    cols = np.zeros((len(slopes), width), np.float32)
    rest = (np.asarray(slopes, np.float32) * LOG2E).astype(np.float32)
    for i in range(N_SLOPE_PIECES):
        piece = rest.astype(BF16).astype(np.float32)
        cols[:, at + 2 * i] = piece * LANES
        cols[:, at + 2 * i + 1] = piece
        rest = rest - piece
    cols[:, at + MASK_LANE] = NEG
    return cols


def _adaln_kernel(c_ref, w_ref, b_ref, o_ref):
    c = c_ref[...]
    s = c * jax.nn.sigmoid(c)
    o_ref[...] = jnp.dot(s, w_ref[...], preferred_element_type=F32, precision=HIGHEST) + b_ref[...]


def _adaln(c, w_ada, b_ada):
    B, D = c.shape
    rows = 8
    cp = jnp.zeros((rows, D), F32).at[:B].set(c)
    n_out = w_ada.shape[1]
    out = pl.pallas_call(
        _adaln_kernel,
        grid=(n_out // D,),
        in_specs=[pl.BlockSpec((rows, D), lambda j: (0, 0)),
                  pl.BlockSpec((D, D), lambda j: (0, j)),
                  pl.BlockSpec((1, D), lambda j: (0, j))],
        out_specs=pl.BlockSpec((rows, D), lambda j: (0, j)),
        out_shape=jax.ShapeDtypeStruct((rows, n_out), F32),
        compiler_params=_params(("parallel",)),
        name="adaln",
    )(cp, w_ada, b_ada.reshape(1, n_out))
    return out[:B]


def _inproj_kernel(x_ref, g_ref, sc_ref, sh_ref, w_ref,
                   dq_ref, dk_ref, dvt_ref, nq_ref, ckv_ref, ska_ref, svt_ref, wka_ref, wvt_ref, gt_ref):
    step = pl.program_id(1)
    tm = x_ref.shape[1]
    dh = HEAD_DIM

    @pl.when(step == 0)
    def _():
        lane = lax.broadcasted_iota(jnp.int32, (tm, LANES), 1)
        for g in range(NSA_KV_HEADS):
            wka_ref[0, g] = jnp.where(lane == (1 - g) * dh + MASK_LANE, 1.0, 0.0).astype(BF16)
        wvt_ref[...] = jnp.zeros(wvt_ref.shape, wvt_ref.dtype)

    @pl.when(step > 0)
    def _():
        h = (_rms(x_ref[0]) * g_ref[...] * (1.0 + sc_ref[0]) + sh_ref[0]).astype(BF16)

        def proj(off, width):
            return jnp.dot(h, w_ref[:, off:off + width], preferred_element_type=F32)

        lane = lax.broadcasted_iota(jnp.int32, (tm, LANES), 1)
        pos = (step - 1) * tm + lax.broadcasted_iota(jnp.int32, (tm, 1), 0)
        low = lane < dh
        pcol = _pos_lanes(pos, lane)
        wcol = _pos_lanes(pos + WINDOW, lane)

        dq_ref[0] = proj(_OFF_DQ, 512).astype(BF16)
        k = proj(_OFF_DK, 512)
        v = proj(_OFF_DV, 512)
        for hh in range(DIFF_HEADS):
            dk_ref[0, :, hh * 256:hh * 256 + LANES] = k[:, hh * LANES:(hh + 1) * LANES].astype(BF16)
            dk_ref[0, :, hh * 256 + LANES:(hh + 1) * 256] = pcol.astype(BF16)
            dvt_ref[0, hh, 0] = v[:, hh * LANES:(hh + 1) * LANES].T.astype(BF16)
        nq_ref[0] = proj(_OFF_NQ, 512).astype(BF16)
        kv = proj(_OFF_KV, 768)
        ckv_ref[0, 0] = kv[:, :LANES]
        ckv_ref[0, 1] = kv[:, LANES:2 * LANES]
        sk, sv, wk, wv = (kv[:, i * LANES:(i + 1) * LANES] for i in range(2, 6))
        onehot = jnp.where(lane == pos // SLC_BLOCK, 1.0, 0.0).astype(BF16)
        sv_t = sv.T
        wv_t = wv.T
        for g, (keys_sel, keys_win) in enumerate(((jnp.where(low, sk, pcol), jnp.where(low, wk, wcol)),
                                                  (jnp.where(low, pcol, sk), jnp.where(low, wcol, wk)))):
            ska_ref[0, g, :, :LANES] = keys_sel.astype(BF16)
            ska_ref[0, g, :, LANES:] = onehot
            svt_ref[0, g, 0] = sv_t[g * dh:(g + 1) * dh].astype(BF16)
            wka_ref[0, g] = keys_win.astype(BF16)
            for c in range(tm // LANES):
                wvt_ref[0, g, c] = wv_t[g * dh:(g + 1) * dh, c * LANES:(c + 1) * LANES].astype(BF16)
        gt_ref[0] = jax.nn.sigmoid(proj(_OFF_NG, LANES))


def _build_w_proj(w_in):
    scale = LOG2E * HEAD_DIM ** -0.5
    nq = w_in[:, _OFF_NQ:_OFF_NQ + 512].reshape(D_MODEL, NSA_KV_HEADS, NSA_GROUP, HEAD_DIM)
    nq = nq.transpose(0, 2, 1, 3).reshape(D_MODEL, 512) * scale
    cols = [w_in[:, _OFF_DQ:_OFF_DQ + 512] * scale, w_in[:, _OFF_DK:_OFF_NQ], nq,
            w_in[:, _OFF_KV:_OFF_NG + _N_GATES], jnp.zeros((D_MODEL, LANES - _N_GATES), F32)]
    return jnp.concatenate(cols, axis=1).astype(BF16)


def _in_proj(x, g_pre, sc, sh, w_proj):
    B, S, D = x.shape
    tm = ROW_TILE
    n = S // tm
    KV, dh = NSA_KV_HEADS, HEAD_DIM
    tile = lambda b, s: (b, jnp.maximum(s - 1, 0), 0)
    out_shapes = [
        ((B, S, 512), (1, tm, 512), tile, BF16),
        ((B, S, DIFF_HEADS * 256), (1, tm, DIFF_HEADS * 256), tile, BF16),
        ((B, DIFF_HEADS, n, LANES, tm), (1, DIFF_HEADS, 1, LANES, tm),
         lambda b, s: (b, 0, jnp.maximum(s - 1, 0), 0, 0), BF16),
        ((B, S, 512), (1, tm, 512), tile, BF16),
        ((B, 2, S, LANES), (1, 2, tm, LANES), lambda b, s: (b, 0, jnp.maximum(s - 1, 0), 0), F32),
        ((B, KV, S, 2 * LANES), (1, KV, tm, 2 * LANES),
         lambda b, s: (b, 0, jnp.maximum(s - 1, 0), 0), BF16),
        ((B, KV, n, dh, tm), (1, KV, 1, dh, tm),
         lambda b, s: (b, 0, jnp.maximum(s - 1, 0), 0, 0), BF16),
        ((B, KV, S + WINDOW, LANES), (1, KV, tm, LANES), lambda b, s: (b, 0, s, 0), BF16),
        ((B, KV, (S + WINDOW) // LANES, dh, LANES), (1, KV, tm // LANES, dh, LANES),
         lambda b, s: (b, 0, s, 0, 0), BF16),
        ((B, S, LANES), (1, tm, LANES), tile, F32),
    ]
    return pl.pallas_call(
        _inproj_kernel,
        grid=(B, n + 1),
        in_specs=[pl.BlockSpec((1, tm, D), tile),
                  pl.BlockSpec((1, D), lambda b, s: (0, 0)),
                  pl.BlockSpec((1, 1, D), lambda b, s: (b, 0, 0)),
                  pl.BlockSpec((1, 1, D), lambda b, s: (b, 0, 0)),
                  pl.BlockSpec((D, _W_PROJ), lambda b, s: (0, 0))],
        out_specs=[pl.BlockSpec(blk, idx) for _, blk, idx, _ in out_shapes],
        out_shape=[jax.ShapeDtypeStruct(shape, dt) for shape, _, _, dt in out_shapes],
        compiler_params=_params(("parallel", "arbitrary")),
        name="in_proj",
    )(x, g_pre.reshape(1, D), sc.reshape(B, 1, D), sh.reshape(B, 1, D), w_proj)


def _compress_kernel(x_ref, pos_ref, w1_ref, w2_ref, kca_ref, vct_ref, vc_ref):
    dh = HEAD_DIM
    n_rows = x_ref.shape[2] // CMP_STRIDE

    def mlp(t):
        first = jnp.zeros((n_rows, w1_ref.shape[3]), F32)
        second = jnp.zeros((n_rows, w1_ref.shape[3]), F32)
        for m in range(CMP_STRIDE):
            x = x_ref[0, t, pl.ds(m, n_rows, stride=CMP_STRIDE), :]
            first += jnp.dot((x + pos_ref[t, m:m + 1, :]).astype(BF16), w1_ref[t, m],
                             preferred_element_type=F32)
            second += jnp.dot((x + pos_ref[t, CMP_STRIDE + m:CMP_STRIDE + m + 1, :]).astype(BF16),
                              w1_ref[t, CMP_STRIDE + m], preferred_element_type=F32)
        hid = first + pltpu.roll(second, n_rows - 1, 0)
        return jnp.dot(jax.nn.gelu(hid).astype(BF16), w2_ref[t], preferred_element_type=F32)

    lane = lax.broadcasted_iota(jnp.int32, (n_rows, LANES), 1)
    cend = lax.broadcasted_iota(jnp.int32, (n_rows, 1), 0) * CMP_STRIDE + (CMP_BLOCK - 1)
    pcol = _pos_lanes(cend, lane)
    kc = mlp(0)
    low = lane < dh
    kca_ref[0, 0] = jnp.where(low, kc, pcol).astype(BF16)
    kca_ref[0, 1] = jnp.where(low, pcol, kc).astype(BF16)
    vc_ref[...] = mlp(1)
    vc_t = vc_ref[...].T
    for h in range(NSA_KV_HEADS):
        vct_ref[0, h] = vc_t[h * dh:(h + 1) * dh].astype(BF16)


def _block_diag(blocks):
    n, r, c = blocks.shape[-3:]
    out = jnp.einsum('...irc,ij->...irjc', blocks, jnp.eye(n, dtype=blocks.dtype))
    return out.reshape(blocks.shape[:-3] + (n * r, n * c))


def _compress(ckv, pos_k, pos_v, w1_k, w1_v, w2_k, w2_v):
    B, _, S, W = ckv.shape
    KV, dh = NSA_KV_HEADS, HEAD_DIM
    n_rows = S // CMP_STRIDE
    pos = jnp.stack([jnp.concatenate([p, p], axis=1) for p in (pos_k, pos_v)]).astype(F32)
    w1 = jnp.stack([w.reshape(CMP_BLOCK, dh, -1) for w in (w1_k, w1_v)])
    w1 = _block_diag(jnp.stack([w1, w1], axis=2)).astype(BF16)
    w2 = jnp.stack([w2_k, w2_v])
    w2 = _block_diag(jnp.stack([w2, w2], axis=1)).astype(BF16)
    return pl.pallas_call(
        _compress_kernel,
        grid=(B,),
        in_specs=[pl.BlockSpec((1, 2, S, W), lambda b: (b, 0, 0, 0)),
                  pl.BlockSpec(pos.shape, lambda b: (0, 0, 0)),
                  pl.BlockSpec(w1.shape, lambda b: (0, 0, 0, 0)),
                  pl.BlockSpec(w2.shape, lambda b: (0, 0, 0))],
        out_specs=[pl.BlockSpec((1, KV, n_rows, LANES), lambda b: (b, 0, 0, 0)),
                   pl.BlockSpec((1, KV, dh, n_rows), lambda b: (b, 0, 0, 0))],
        out_shape=[jax.ShapeDtypeStruct((B, KV, n_rows, LANES), BF16),
                   jax.ShapeDtypeStruct((B, KV, dh, n_rows), BF16)],
        scratch_shapes=[pltpu.VMEM((n_rows, W), F32)],
        compiler_params=_params(("parallel",)),
        name="nsa_compress",
    )(ckv, pos, w1, w2)


def _flash_scratch(dv, n_cols):
    tk = KEY_TILE
    return [pltpu.VMEM((tk, n_cols), F32), pltpu.VMEM((tk, n_cols), F32),
            pltpu.VMEM((tk, n_cols), BF16), pltpu.VMEM((tk, n_cols), BF16),
            pltpu.VMEM((1, n_cols), F32), pltpu.VMEM((1, n_cols), F32),
            pltpu.VMEM((1, n_cols), F32), pltpu.VMEM((1, n_cols), F32), pltpu.VMEM((dv, n_cols), F32)]


def _flash_causal(q0, tq, qa_ref, k_tile, v_tile, scratch, tiles=None):
    s_a, s_b, p_a, p_b, a_a, a_b, m_ref, l_ref, acc_ref = scratch
    tk = KEY_TILE
    n_cols = qa_ref.shape[0]
    assert tk % tq == 0
    n_full = q0 // tk
    n = n_full + 1 if tiles is None else tiles[1]
    m_ref[...] = jnp.full(m_ref.shape, NEG, F32)
    l_ref[...] = jnp.zeros(l_ref.shape, F32)
    acc_ref[...] = jnp.zeros(acc_ref.shape, F32)
    p_b[...] = jnp.zeros(p_b.shape, BF16)
    a_b[...] = jnp.ones(a_b.shape, F32)

    def key_tile(t):
        t = jnp.clip(t, 0, n - 1)
        j = jnp.where(t == 0, n_full, t - 1) if tiles is None else tiles[0][t]
        return j, pl.multiple_of(j * tk, tk)

    def scores(t, s_ref, masked):
        j, k0 = key_tile(t)
        kt = k_tile(j, k0)
        for c in range(0, n_cols, COL_SPLIT):
            s = _dot_nt(kt, qa_ref[c:c + COL_SPLIT, :])
            if masked:
                krow = lax.broadcasted_iota(jnp.int32, (tk, 1), 0)
                qcol = (lax.broadcasted_iota(jnp.int32, (1, COL_SPLIT), 1) + c) % tq
                s = jnp.where((krow + (k0 - q0)) <= qcol, s, NEG)
            s_ref[:, c:c + COL_SPLIT] = s

    def softmax(s_ref, p_ref, a_ref):
        ch = SOFTMAX_CHUNK_VREGS * 8 * LANES // n_cols
        chunks = [slice(c, c + ch) for c in range(0, tk, ch)]
        mx = None
        for rows in chunks:
            part = jnp.max(s_ref[rows, :].reshape(ch // 8, 8, n_cols), axis=0)
            mx = part if mx is None else jnp.maximum(mx, part)
        m_old = m_ref[...]
        m_new = jnp.maximum(m_old, jnp.max(mx, axis=0, keepdims=True))
        sm = None
        for rows in chunks:
            p = jnp.exp2(s_ref[rows, :] - m_new)
            p_ref[rows, :] = p.astype(BF16)
            part = jnp.sum(p.reshape(ch // 8, 8, n_cols), axis=0)
            sm = part if sm is None else sm + part
        alpha = jnp.exp2(m_old - m_new)
        a_ref[...] = alpha
        l_ref[...] = alpha * l_ref[...] + jnp.sum(sm, axis=0, keepdims=True)
        m_ref[...] = m_new

    def values(t, p_ref, a_ref):
        j, k0 = key_tile(t)
        vt = v_tile(j, k0)
        for c in range(0, n_cols, COL_SPLIT):
            cs = slice(c, c + COL_SPLIT)
            acc_ref[:, cs] = a_ref[:, cs] * acc_ref[:, cs] + jnp.dot(vt, p_ref[:, cs], preferred_element_type=F32)

    scores(0, s_a, True)

    def pair(u, carry):
        i = 2 * u
        scores(i + 1, s_b, False)
        softmax(s_a, p_a, a_a)
        values(i - 1, p_b, a_b)

        @pl.when(i + 1 < n)
        def _():
            scores(i + 2, s_a, False)
            softmax(s_b, p_b, a_b)
            values(i, p_a, a_a)

        return carry

    lax.fori_loop(0, (n + 1) // 2, pair, 0)

    @pl.when(n % 2 == 1)
    def _():
        values(n - 1, p_a, a_a)

    @pl.when(n % 2 == 0)
    def _():
        values(n - 1, p_b, a_b)

    return acc_ref, l_ref


def _softmax_cols(s, bias):
    e, inv = _softmax_cols_unnormalised(s, bias)
    return e * inv


def _softmax_cols_unnormalised(s, bias):
    s = s + bias
    m = jnp.max(s, axis=0, keepdims=True)
    e = jnp.exp2(s - m)
    inv = jnp.where(m > 0.5 * NEG, 1.0 / jnp.sum(e, axis=0, keepdims=True), 0.0)
    return e, inv


def _diff_kernel(qx_ref, lam_ref, gsub_ref, q_ref, k_ref, v_ref, o_ref, qa_ref, *scratch, lambda_init):
    tq = DIFF_TQ
    q0 = pl.program_id(2) * tq
    q = q_ref[0].astype(F32)
    low = lax.broadcasted_iota(jnp.int32, (tq, LANES), 1) < HEAD_DIM
    ext = jnp.broadcast_to(qx_ref[0], (tq, LANES)).astype(BF16)
    qa_ref[0:tq, :LANES] = jnp.where(low, q, 0.0).astype(BF16)
    qa_ref[tq:, :LANES] = jnp.where(low, 0.0, q).astype(BF16)
    qa_ref[0:tq, LANES:] = ext
    qa_ref[tq:, LANES:] = ext

    acc_ref, l_ref = _flash_causal(q0, tq, qa_ref,
                                   lambda j, k0: k_ref[0, pl.ds(k0, KEY_TILE), :],
                                   lambda j, k0: v_ref[0, 0, j],
                                   scratch)

    lv = lam_ref[...]
    lam = (jnp.exp(jnp.sum(lv[0:1] * lv[1:2], axis=-1, keepdims=True))
           - jnp.exp(jnp.sum(lv[2:3] * lv[3:4], axis=-1, keepdims=True)) + lambda_init)
    o_t = acc_ref[...] / l_ref[...]
    o_t = o_t[:, :tq] - lam * o_t[:, tq:]
    o_t = o_t * lax.rsqrt(jnp.mean(o_t * o_t, axis=0, keepdims=True) + EPS)
    o_ref[0] = (o_t.T * gsub_ref[...] * (1.0 - lambda_init)).astype(o_ref.dtype)


def _diff_attention(dq, dk, dv_t, lam_vecs, g_subln, lambda_init):
    B, S, _ = dq.shape
    tq, tk = DIFF_TQ, KEY_TILE
    H = DIFF_HEADS
    qx = jnp.asarray(_slope_cols(_alibi_slopes(H), LANES, 0)).reshape(H, 1, LANES)
    lam = jnp.zeros((8, LANES), F32).at[:4, :HEAD_DIM].set(jnp.stack(lam_vecs))
    return pl.pallas_call(
        functools.partial(_diff_kernel, lambda_init=lambda_init),
        grid=(B, H, S // tq),
        in_specs=[pl.BlockSpec((1, 1, LANES), lambda b, h, i: (h, 0, 0)),
                  pl.BlockSpec((8, LANES), lambda b, h, i: (0, 0)),
                  pl.BlockSpec((1, LANES), lambda b, h, i: (0, 0)),
                  pl.BlockSpec((1, tq, LANES), lambda b, h, i: (b, i, h)),
                  pl.BlockSpec((1, S, 2 * LANES), lambda b, h, i: (b, 0, h)),
                  pl.BlockSpec((1, 1, S // tk, LANES, tk), lambda b, h, i: (b, h, 0, 0, 0))],
        out_specs=pl.BlockSpec((1, tq, LANES), lambda b, h, i: (b, i, h)),
        out_shape=jax.ShapeDtypeStruct((B, S, H * LANES), BF16),
        scratch_shapes=[pltpu.VMEM((2 * tq, 2 * LANES), BF16)] + _flash_scratch(LANES, 2 * tq),
        compiler_params=_params(("parallel", "parallel", "parallel")),
        name="diff_attention",
    )(qx, lam, g_subln.reshape(1, LANES).astype(F32), dq, dk, dv_t)


def _nsa_kernel(qx_ref, q_ref, kc_ref, vc_ref, ka_ref, sv_ref, wk_ref, wv_ref, g_ref, ov_ref, cb_ref, wb_ref,
                o_ref, qa_ref, out_ref, tiles_ref, *scratch):
    tq, G, dh = NSA_TQ, NSA_GROUP, HEAD_DIM
    n_cmp = kc_ref.shape[2]
    n_win = WINDOW + tq
    grp = pl.program_id(1)
    qi = pl.program_id(2)
    q0 = pl.multiple_of(qi * tq, tq)
    cols = [slice(i * tq, (i + 1) * tq) for i in range(G)]
    g_t = g_ref[0].T

    def gate(i, branch):
        a = 3 * i + branch
        b = 3 * (G + i) + branch
        return jnp.where(grp == 0, g_t[a:a + 1, :], g_t[b:b + 1, :])

    lane_q = lax.broadcasted_iota(jnp.int32, (tq, LANES), 1)
    mine = (lane_q >= grp * dh) & (lane_q < (grp + 1) * dh)
    qx = qx_ref[0]
    for i in range(G):
        q = q_ref[0, :, i * LANES:(i + 1) * LANES].astype(F32)
        qa_ref[cols[i], :LANES] = jnp.where(mine, q, jnp.broadcast_to(qx[i:i + 1], (tq, LANES))).astype(BF16)
    qs = qa_ref[:, :LANES]

    kw = wk_ref[0, 0, pl.ds(q0, n_win), :]
    vw = jnp.concatenate([wv_ref[0, 0, (tq // LANES) * qi + c] for c in range(n_win // LANES)], axis=1)
    e, inv = _softmax_cols_unnormalised(_dot_nt(kw, qs), jnp.tile(wb_ref[...], (1, G)))
    ow = jnp.dot(vw, e.astype(BF16), preferred_element_type=F32) * inv

    cbias = cb_ref[pl.ds(pl.multiple_of(n_cmp - (tq // CMP_STRIDE) * qi, 8), n_cmp), :]
    p = _softmax_cols(_dot_nt(kc_ref[0, 0], qs), jnp.tile(cbias, (1, G)))
    oc = jnp.dot(vc_ref[0, 0], p.astype(BF16), preferred_element_type=F32)
    for i in range(G):
        out_ref[i * dh:(i + 1) * dh, :] = gate(i, 0) * oc[:, cols[i]] + gate(i, 2) * ow[:, cols[i]]

    psum = p[:, cols[0]]
    for i in range(1, G):
        psum = psum + p[:, cols[i]]
    hi = psum.astype(BF16)
    r1 = psum - hi.astype(F32)
    mid = r1.astype(BF16)
    lo = (r1 - mid.astype(F32)).astype(BF16)
    ov = ov_ref[...]
    imp = (jnp.dot(ov, hi, preferred_element_type=F32) + jnp.dot(ov, mid, preferred_element_type=F32)
           + jnp.dot(ov, lo, preferred_element_type=F32))
    n_slc = imp.shape[0]
    blk = lax.broadcasted_iota(jnp.int32, (n_slc, tq), 0)
    cur = (lax.broadcasted_iota(jnp.int32, (1, tq), 1) + q0) // SLC_BLOCK
    forced = (blk == 0) | (blk == cur) | (blk == cur - 1)
    score = jnp.where(forced, -jnp.inf, jnp.where(blk <= cur, imp, NEG))
    blkf = blk.astype(F32)
    for _ in range(SLC_TOP - 3):
        mx = jnp.max(score, axis=0, keepdims=True)
        idx = jnp.min(jnp.where(score == mx, blkf, float(n_slc)), axis=0, keepdims=True)
        score = jnp.where(blkf == idx, -jnp.inf, score)
    sel = jnp.where(score == -jnp.inf, 1.0, 0.0)
    sel_t = sel.T
    blockmask = ((sel_t - 1.0) * 1e30).astype(BF16)
    for i in range(G):
        qa_ref[cols[i], LANES:] = blockmask

    per_tile = KEY_TILE // SLC_BLOCK
    n_tiles = n_slc // per_tile
    used = jnp.max(jnp.max(sel, axis=1, keepdims=True).reshape(n_tiles, per_tile, 1), axis=1)
    bit = jnp.where(used > 0.0, 1 << lax.broadcasted_iota(jnp.int32, (n_tiles, 1), 0), 0)
    bits = jnp.sum(bit, axis=0, keepdims=True)[0, 0]
    n_full = q0 // KEY_TILE
    tiles_ref[0] = n_full
    count = jnp.int32(1)
    for j in range(n_tiles):
        tiles_ref[count] = j
        count = count + jnp.where(j < n_full, (bits >> j) & 1, 0)

    acc_ref, l_ref = _flash_causal(q0, tq, qa_ref,
                                   lambda j, k0: ka_ref[0, 0, pl.ds(k0, KEY_TILE), :],
                                   lambda j, k0: sv_ref[0, 0, j],
                                   scratch, tiles=(tiles_ref, count))
    os_ = acc_ref[...] / l_ref[...]
    for i in range(G):
        out_ref[i * dh:(i + 1) * dh, :] += gate(i, 1) * os_[:, cols[i]]

    o_ref[0] = out_ref[...].T.astype(o_ref.dtype)


def _nsa_attention(nq, kca, vc_t, ska, sv_t, wka, wv_t, gates):
    B, S, _ = nq.shape
    tq, tk, G, dh = NSA_TQ, KEY_TILE, NSA_GROUP, HEAD_DIM
    KV = NSA_KV_HEADS
    n_cmp = kca.shape[2]
    n_slc = S // SLC_BLOCK
    Sp = S + WINDOW
    slopes = _alibi_slopes(NSA_HEADS)
    qx = jnp.asarray(np.stack([_slope_cols(slopes[:G], LANES, dh), _slope_cols(slopes[G:], LANES, 0)]))
    overlap_t = _overlap_matrix(S).T
    r = np.arange(tq)[None, :]
    j = np.arange(2 * n_cmp)[:, None]
    cmp_band = jnp.asarray(np.where(CMP_STRIDE * (j - n_cmp) + CMP_BLOCK - 1 <= r, 0.0, NEG).astype(np.float32))
    c = np.arange(WINDOW + tq)[:, None]
    d = r - c + WINDOW
    win_band = jnp.asarray(np.where((d >= 0) & (d < WINDOW), 0.0, NEG).astype(np.float32))
    const2 = lambda b, g, i: (0, 0)
    per_bg = lambda b, g, i: (b, g, 0, 0)
    per_bg5 = lambda b, g, i: (b, g, 0, 0, 0)
    return pl.pallas_call(
        _nsa_kernel,
        grid=(B, KV, S // tq),
        in_specs=[pl.BlockSpec((1, G, LANES), lambda b, g, i: (g, 0, 0)),
                  pl.BlockSpec((1, tq, G * LANES), lambda b, g, i: (b, i, 0)),
                  pl.BlockSpec((1, 1, n_cmp, LANES), per_bg),
                  pl.BlockSpec((1, 1, dh, n_cmp), per_bg),
                  pl.BlockSpec((1, 1, S, 2 * LANES), per_bg),
                  pl.BlockSpec((1, 1, S // tk, dh, tk), per_bg5),
                  pl.BlockSpec((1, 1, Sp, LANES), per_bg),
                  pl.BlockSpec((1, 1, Sp // LANES, dh, LANES), per_bg5),
                  pl.BlockSpec((1, tq, LANES), lambda b, g, i: (b, i, 0)),
                  pl.BlockSpec((n_slc, n_cmp), const2),
                  pl.BlockSpec(cmp_band.shape, const2), pl.BlockSpec(win_band.shape, const2)],
        out_specs=pl.BlockSpec((1, tq, G * dh), lambda b, g, i: (b, i, g)),
        out_shape=jax.ShapeDtypeStruct((B, S, NSA_HEADS * dh), BF16),
        scratch_shapes=[pltpu.VMEM((G * tq, 2 * LANES), BF16),
                        pltpu.VMEM((G * dh, tq), F32),
                        pltpu.SMEM((2 * (S // tk),), jnp.int32)]
        + _flash_scratch(dh, G * tq),
        compiler_params=_params(("parallel", "parallel", "parallel")),
        name="nsa_attention",
    )(qx, nq, kca, vc_t, ska, sv_t, wka, wv_t, gates, overlap_t, cmp_band, win_band)


def _outproj_kernel(oa_ref, ob_ref, x_ref, wo_ref, gpm_ref, gta_ref, gpf_ref, sc_ref, sh_ref, wr_ref, br_ref,
                    x1_ref, h2_ref, rt_ref, cnt_ref):
    half = oa_ref.shape[1]
    mix = (jnp.dot(oa_ref[...], wo_ref[:half], preferred_element_type=F32)
           + jnp.dot(ob_ref[...], wo_ref[half:], preferred_element_type=F32))
    x1 = x_ref[...] + gta_ref[0] * (_rms(mix) * gpm_ref[...])
    x1_ref[...] = x1
    h2 = _rms(x1) * gpf_ref[...] * (1.0 + sc_ref[0]) + sh_ref[0]
    h2_ref[...] = h2
    h_hi = h2.astype(BF16)
    h_lo = (h2 - h_hi.astype(F32)).astype(BF16)
    logits = (jnp.dot(h_hi, wr_ref[0], preferred_element_type=F32)
              + jnp.dot(h_lo, wr_ref[0], preferred_element_type=F32)
              + jnp.dot(h_hi, wr_ref[1], preferred_element_type=F32)) + br_ref[...]
    tm = logits.shape[0]
    lane = lax.broadcasted_iota(jnp.int32, (tm, LANES), 1)
    lanef = lane.astype(F32)
    is_grp = (lane >= N_EXPERTS) & (lane < N_EXPERTS + N_GROUPS)
    lg = jnp.where(is_grp, logits, -jnp.inf)
    mg = jnp.max(lg, axis=-1, keepdims=True)
    g_lane = jnp.min(jnp.where(lg == mg, lanef, 1e9), axis=-1, keepdims=True)
    p_grp = 1.0 / jnp.sum(jnp.where(is_grp, jnp.exp(lg - mg), 0.0), axis=-1, keepdims=True)
    grp = g_lane.astype(jnp.int32) - N_EXPERTS
    in_grp = (lane < N_EXPERTS) & ((lane // EXPERTS_PER_GROUP) == grp)
    le = jnp.where(in_grp, logits, -jnp.inf)
    v1 = jnp.max(le, axis=-1, keepdims=True)
    i1 = jnp.min(jnp.where(le == v1, lanef, 1e9), axis=-1, keepdims=True)
    le2 = jnp.where(lanef == i1, -jnp.inf, le)
    v2 = jnp.max(le2, axis=-1, keepdims=True)
    i2 = jnp.min(jnp.where(le2 == v2, lanef, 1e9), axis=-1, keepdims=True)
    e2 = jnp.exp(v2 - v1)
    w1 = p_grp / (1.0 + e2)
    w2 = p_grp * e2 / (1.0 + e2)
    rt_ref[...] = jnp.where(lane == 0, i1, jnp.where(lane == 1, i2, jnp.where(lane == 2, w1,
                            jnp.where(lane == 3, w2, 0.0))))

    @pl.when(pl.program_id(0) == 0)
    def _():
        cnt_ref[...] = jnp.zeros(cnt_ref.shape, F32)

    picked = jnp.where(lanef == i1, 1.0, jnp.where(lanef == i2, 1.0, 0.0))
    cnt_ref[...] += jnp.sum(picked, axis=0, keepdims=True)


def _out_proj(o_a, o_b, x2, w_out, g_post_mix, gt_a, g_pre_ffn, sc_f, sh_f, w_r, b_r, B, S):
    T, D = x2.shape
    tm = ROW_TILE
    per_b = S // tm
    row = lambda i: (i, 0)
    bat = lambda i: (i // per_b, 0, 0)
    const = lambda i: (0, 0)
    return pl.pallas_call(
        _outproj_kernel,
        grid=(T // tm,),
        in_specs=[pl.BlockSpec((tm, o_a.shape[1]), row), pl.BlockSpec((tm, o_b.shape[1]), row),
                  pl.BlockSpec((tm, D), row), pl.BlockSpec((D, D), const),
                  pl.BlockSpec((1, D), const), pl.BlockSpec((1, 1, D), bat), pl.BlockSpec((1, D), const),
                  pl.BlockSpec((1, 1, D), bat), pl.BlockSpec((1, 1, D), bat),
                  pl.BlockSpec((2, D, LANES), lambda i: (0, 0, 0)), pl.BlockSpec((1, LANES), const)],
        out_specs=[pl.BlockSpec((tm, D), row), pl.BlockSpec((tm, D), row), pl.BlockSpec((tm, LANES), row),
                   pl.BlockSpec((8, LANES), const)],
        out_shape=[jax.ShapeDtypeStruct((T, D), F32), jax.ShapeDtypeStruct((T, D), F32),
                   jax.ShapeDtypeStruct((T, LANES), F32), jax.ShapeDtypeStruct((8, LANES), F32)],
        compiler_params=_params(("arbitrary",)),
        name="out_proj_router",
    )(o_a, o_b, x2, w_out, g_post_mix.reshape(1, D), gt_a.reshape(B, 1, D), g_pre_ffn.reshape(1, D),
      sc_f.reshape(B, 1, D), sh_f.reshape(B, 1, D), w_r, b_r)


def _slots_kernel(rt_ref, cnt_ref, slot_ref, blk_ref, base_ref, *, n_blk_rows):
    tm = rt_ref.shape[0]
    rt = rt_ref[...]
    lane = lax.broadcasted_iota(jnp.int32, (tm, LANES), 1).astype(F32)
    i1 = rt[:, 0:1]
    i2 = rt[:, 1:2]
    hit1 = lane == i1
    hit2 = lane == i2
    onehot = jnp.where(hit1, 1.0, jnp.where(hit2, 1.0, 0.0))
    colsum = jnp.sum(onehot, axis=0, keepdims=True)

    @pl.when(pl.program_id(0) == 0)
    def _():
        cnt = cnt_ref[0:1, :]
        padded = jnp.floor((cnt + (MOE_ROWS - 1)) / MOE_ROWS) * MOE_ROWS
        lane1 = lax.broadcasted_iota(jnp.int32, (1, LANES), 1)
        incl = padded
        sh = 1
        while sh < LANES:
            incl = incl + jnp.where(lane1 >= sh, pltpu.roll(incl, sh, 1), 0.0)
            sh *= 2
        base_ref[...] = incl - padded
        rows = lax.broadcasted_iota(jnp.int32, (n_blk_rows, LANES), 0).astype(F32) * MOE_ROWS
        lane2 = lax.broadcasted_iota(jnp.int32, (n_blk_rows, LANES), 1)
        ended = jnp.where(lane2 < N_EXPERTS, jnp.where(incl <= rows, 1.0, 0.0), 0.0)
        blk_e = jnp.minimum(jnp.sum(ended, axis=-1, keepdims=True), N_EXPERTS - 1.0)
        total = jnp.sum(jnp.where(lane1 == N_EXPERTS - 1, incl, 0.0), axis=-1, keepdims=True)
        n_used = total / MOE_ROWS
        blk_ref[...] = jnp.where(lane2 == 0, blk_e, jnp.where(lane2 == 1, n_used, 0.0)).astype(jnp.int32)

    r = lax.broadcasted_iota(jnp.int32, (tm, tm), 0)
    c = lax.broadcasted_iota(jnp.int32, (tm, tm), 1)
    tri = jnp.where(c < r, 1.0, 0.0).astype(BF16)
    before = jnp.dot(tri, onehot.astype(BF16), preferred_element_type=F32) + base_ref[...]
    s1 = jnp.sum(jnp.where(hit1, before, 0.0), axis=-1, keepdims=True)
    s2 = jnp.sum(jnp.where(hit2, before, 0.0), axis=-1, keepdims=True)
    lane_i = lax.broadcasted_iota(jnp.int32, (tm, LANES), 1)
    slot_ref[...] = jnp.where(lane_i == 0, s1, jnp.where(lane_i == 1, s2, 0.0)).astype(jnp.int32)
    base_ref[...] += colsum


def _slots(route, counts, n_blocks):
    T = route.shape[0]
    tm = ROW_TILE
    n_blk_rows = -(-n_blocks // 8) * 8
    slot, blk = pl.pallas_call(
        functools.partial(_slots_kernel, n_blk_rows=n_blk_rows),
        grid=(T // tm,),
        in_specs=[pl.BlockSpec((tm, LANES), lambda i: (i, 0)),
                  pl.BlockSpec(counts.shape, lambda i: (0, 0))],
        out_specs=[pl.BlockSpec((tm, LANES), lambda i: (i, 0)),
                   pl.BlockSpec((n_blk_rows, LANES), lambda i: (0, 0))],
        out_shape=[jax.ShapeDtypeStruct((T, LANES), jnp.int32),
                   jax.ShapeDtypeStruct((n_blk_rows, LANES), jnp.int32)],
        scratch_shapes=[pltpu.VMEM((1, LANES), F32)],
        compiler_params=_params(("arbitrary",)),
        name="moe_slots",
    )(route, counts)
    return slot[:, 0], slot[:, 1], blk[:n_blocks, 0], blk[0:1, 1]


def _dispatch_kernel(s1_ref, s2_ref, h_ref, xz_ref, xb_ref, sem):
    del xz_ref
    tm = h_ref.shape[0]
    base = pl.program_id(0) * tm

    def issue(r, carry):
        src = h_ref.at[pl.ds(r, 1), :]
        pltpu.make_async_copy(src, xb_ref.at[pl.ds(s1_ref[base + r], 1), :], sem).start(priority=0)
        pltpu.make_async_copy(src, xb_ref.at[pl.ds(s2_ref[base + r], 1), :], sem).start(priority=1)
        return carry

    lax.fori_loop(0, tm, issue, 0, unroll=DMA_UNROLL)

    def drain(r, carry):
        for _ in range(TOP_K):
            pltpu.make_async_copy(h_ref.at[pl.ds(0, 1), :], xb_ref.at[pl.ds(0, 1), :], sem).wait()
        return carry

    lax.fori_loop(0, tm, drain, 0, unroll=DMA_UNROLL)


def _dispatch(h2, slot1, slot2, n_rows):
    T, D = h2.shape
    tm = ROW_TILE
    xz = jnp.zeros((n_rows, D), h2.dtype)
    return pl.pallas_call(
        _dispatch_kernel,
        grid_spec=pltpu.PrefetchScalarGridSpec(
            num_scalar_prefetch=2,
            grid=(T // tm,),
            in_specs=[pl.BlockSpec((tm, D), lambda i, a, b: (i, 0)),
                      pl.BlockSpec(memory_space=pl.ANY)],
            out_specs=pl.BlockSpec(memory_space=pl.ANY),
            scratch_shapes=[pltpu.SemaphoreType.DMA(())]),
        out_shape=jax.ShapeDtypeStruct((n_rows, D), h2.dtype),
        input_output_aliases={3: 0},
        compiler_params=_params(("arbitrary",)),
        name="moe_dispatch",
    )(slot1, slot2, h2, xz)


def _expert_kernel(be_ref, nu_ref, x_ref, wg_ref, wu_ref, wd_ref, y_ref):
    del be_ref
    i = pl.program_id(0)

    @pl.when(i >= nu_ref[0])
    def _():
        y_ref[...] = jnp.zeros(y_ref.shape, y_ref.dtype)

    @pl.when(i < nu_ref[0])
    def _():
        x = x_ref[...].astype(BF16)
        g = jnp.dot(x, wg_ref[0].astype(BF16), preferred_element_type=F32)
        u = jnp.dot(x, wu_ref[0].astype(BF16), preferred_element_type=F32)
        hid = (g * jax.nn.sigmoid(g) * u).astype(BF16)
        y_ref[...] = jnp.dot(hid, wd_ref[0].astype(BF16), preferred_element_type=F32).astype(y_ref.dtype)


def _experts(xbuf, blk_e, n_used, w_gate, w_up, w_down):
    D = xbuf.shape[1]
    E, _, De = w_gate.shape
    n_blocks = blk_e.shape[0]
    R = n_blocks * MOE_ROWS
    blk = lambda i, be, nu: (jnp.minimum(i, nu[0] - 1), 0)
    wsel = lambda i, be, nu: (be[jnp.minimum(i, nu[0] - 1)], 0, 0)
    return pl.pallas_call(
        _expert_kernel,
        grid_spec=pltpu.PrefetchScalarGridSpec(
            num_scalar_prefetch=2,
            grid=(n_blocks,),
            in_specs=[pl.BlockSpec((MOE_ROWS, D), blk),
                      pl.BlockSpec((1, D, De), wsel), pl.BlockSpec((1, D, De), wsel),
                      pl.BlockSpec((1, De, D), wsel)],
            out_specs=pl.BlockSpec((MOE_ROWS, D), lambda i, be, nu: (i, 0))),
        out_shape=jax.ShapeDtypeStruct((R, D), F32),
        compiler_params=_params(("arbitrary",)),
        name="moe_experts",
    )(blk_e, n_used, xbuf, w_gate, w_up, w_down)


def _combine_kernel(s1_ref, s2_ref, yb_ref, rt_ref, x1_ref, gtf_ref, gpo_ref, o_ref, ya_ref, yc_ref, sem):
    tm = x1_ref.shape[0]
    i = pl.program_id(0)

    def row_copy(src_row, dst_ref, slot, r):
        return pltpu.make_async_copy(yb_ref.at[pl.ds(src_row, 1), :], dst_ref.at[slot, pl.ds(r, 1), :], sem.at[slot])

    def gather(step, slot):
        def issue(r, carry):
            row_copy(s1_ref[step * tm + r], ya_ref, slot, r).start(priority=0)
            row_copy(s2_ref[step * tm + r], yc_ref, slot, r).start(priority=1)
            return carry

        lax.fori_loop(0, tm, issue, 0, unroll=DMA_UNROLL)

    @pl.when(i == 0)
    def _():
        gather(0, 0)

    @pl.when(i + 1 < pl.num_programs(0))
    def _():
        gather(i + 1, (i + 1) % 2)

    slot = i % 2

    def drain(r, carry):
        for _ in range(TOP_K):
            row_copy(0, ya_ref, slot, 0).wait()
        return carry

    lax.fori_loop(0, tm, drain, 0, unroll=DMA_UNROLL)
    rt = rt_ref[...]
    f = rt[:, 2:3] * ya_ref[slot] + rt[:, 3:4] * yc_ref[slot]
    o_ref[...] = x1_ref[...] + gtf_ref[0] * (_rms(f) * gpo_ref[...])


def _combine(ybuf, slot1, slot2, route, x1, gt_f, g_post_ffn, B, S):
    T, D = x1.shape
    tm = ROW_TILE
    per_b = S // tm
    row = lambda i, a, b: (i, 0)
    return pl.pallas_call(
        _combine_kernel,
        grid_spec=pltpu.PrefetchScalarGridSpec(
            num_scalar_prefetch=2,
            grid=(T // tm,),
            in_specs=[pl.BlockSpec(memory_space=pl.ANY),
                      pl.BlockSpec((tm, LANES), row), pl.BlockSpec((tm, D), row),
                      pl.BlockSpec((1, 1, D), lambda i, a, b: (i // per_b, 0, 0)),
                      pl.BlockSpec((1, D), lambda i, a, b: (0, 0))],
            out_specs=pl.BlockSpec((tm, D), row),
            scratch_shapes=[pltpu.VMEM((2, tm, D), F32), pltpu.VMEM((2, tm, D), F32),
                            pltpu.SemaphoreType.DMA((2,))]),
        out_shape=jax.ShapeDtypeStruct((T, D), F32),
        compiler_params=_params(("arbitrary",)),
        name="moe_combine",
    )(slot1, slot2, ybuf, route, x1, gt_f.reshape(B, 1, D), g_post_ffn.reshape(1, D))


def _overlap_matrix(S):
    n_rows = S // CMP_STRIDE
    n_cmp = (S - CMP_BLOCK) // CMP_STRIDE + 1
    n_slc = S // SLC_BLOCK
    cstart = np.arange(n_rows) * CMP_STRIDE
    sstart = np.arange(n_slc) * SLC_BLOCK
    ov = (cstart[:, None] < sstart[None, :] + SLC_BLOCK) & (cstart[:, None] + CMP_BLOCK > sstart[None, :])
    ov &= (np.arange(n_rows) < n_cmp)[:, None]
    return jnp.asarray(ov.astype(np.float32), dtype=BF16)


def kernel(x, c, w_ada, b_ada, g_pre_mix, g_post_mix, g_pre_ffn, g_post_ffn, w_in, lam_q1, lam_k1, lam_q2,
           lam_k2, g_subln, cmp_pos_k, cmp_pos_v, cmp_w1_k, cmp_w1_v, cmp_w2_k, cmp_w2_v, w_out,
           w_router_grp, b_router_grp, w_router_exp, b_router_exp, w_exp_gate, w_exp_up, w_exp_down):
    B, S, D = x.shape
    T = B * S
    depth = w_ada.shape[0]
    assert S // SLC_BLOCK == LANES and D == D_MODEL
    n_blocks = (T * TOP_K) // MOE_ROWS + N_EXPERTS
    for l in range(depth):
        lambda_init = 0.8 - 0.6 * math.exp(-0.3 * l)
        mod = _adaln(c, w_ada[l], b_ada[l])
        sh_a, sc_a, gt_a, sh_f, sc_f, gt_f = jnp.split(mod, 6, axis=-1)

        dq, dk, dv_t, nq, ckv, ska, sv_t, wka, wv_t, gates = _in_proj(x, g_pre_mix[l], sc_a, sh_a,
                                                                      _build_w_proj(w_in[l]))
        kca, vc_t = _compress(ckv, cmp_pos_k[l], cmp_pos_v[l], cmp_w1_k[l], cmp_w1_v[l], cmp_w2_k[l], cmp_w2_v[l])
        o_a = _diff_attention(dq, dk, dv_t, (lam_q1[l], lam_k1[l], lam_q2[l], lam_k2[l]), g_subln[l], lambda_init)
        o_b = _nsa_attention(nq, kca, vc_t, ska, sv_t, wka, wv_t, gates)

        w_r = jnp.zeros((D, LANES), F32).at[:, :N_EXPERTS].set(w_router_exp[l]).at[
            :, N_EXPERTS:N_EXPERTS + N_GROUPS].set(w_router_grp[l])
        b_r = jnp.zeros((1, LANES), F32).at[0, :N_EXPERTS].set(b_router_exp[l]).at[
            0, N_EXPERTS:N_EXPERTS + N_GROUPS].set(b_router_grp[l])
        w_r_hi = w_r.astype(BF16)
        w_r = jnp.stack([w_r_hi, (w_r - w_r_hi.astype(F32)).astype(BF16)])
        x1, h2, route, counts = _out_proj(o_a.reshape(T, -1), o_b.reshape(T, -1), x.reshape(T, D),
                                          w_out[l].astype(BF16), g_post_mix[l], gt_a, g_pre_ffn[l], sc_f, sh_f,
                                          w_r, b_r, B, S)
        slot1, slot2, blk_e, n_used = _slots(route, counts, n_blocks)
        xbuf = _dispatch(h2, slot1, slot2, n_blocks * MOE_ROWS)
        ybuf = _experts(xbuf, blk_e, n_used, w_exp_gate[l], w_exp_up[l], w_exp_down[l])
        x = _combine(ybuf, slot1, slot2, route, x1, gt_f, g_post_ffn[l], B, S).reshape(B, S, D)
    return x
```
